```python
import jax
import jax.numpy as jnp
from jax import lax
import numpy as np

D_MODEL = 1024
BATCH = 32
SEQ = 2048
DEPTH = 1

CHUNK = 64
HEAD_DIM = 64
RMS_EPS = 1e-6
NEG_INF = -1e30

SWA_Q_HEADS = 8
SWA_KV_HEADS = 2
SWA_GROUP = SWA_Q_HEADS // SWA_KV_HEADS
SWA_WINDOW = 128
SWA_WINDOW_CHUNKS = SWA_WINDOW // CHUNK
SWA_KEYS = (SWA_WINDOW_CHUNKS + 1) * CHUNK
SWA_WIDTH = SWA_Q_HEADS * HEAD_DIM
SWA_KV_WIDTH = SWA_KV_HEADS * HEAD_DIM

SB_HEADS = 8
SB_WIDTH = SB_HEADS * HEAD_DIM
SB_BLOCK = 128

N_BRANCHES = 2
SPLIT_POINTS = (
    SWA_WIDTH,
    SWA_WIDTH + SWA_KV_WIDTH,
    SWA_WIDTH + 2 * SWA_KV_WIDTH,
    SWA_WIDTH + 2 * SWA_KV_WIDTH + SB_WIDTH,
    SWA_WIDTH + 2 * SWA_KV_WIDTH + 2 * SB_WIDTH,
    SWA_WIDTH + 2 * SWA_KV_WIDTH + 3 * SB_WIDTH,
)
IN_WIDTH = SWA_WIDTH + 2 * SWA_KV_WIDTH + 3 * SB_WIDTH + N_BRANCHES * D_MODEL

PEER_HEADS = 8
PEER_N_KEYS = 128
PEER_N_EXPERTS = PEER_N_KEYS * PEER_N_KEYS
PEER_QUERY_DIM = 256
PEER_HALF = PEER_QUERY_DIM // 2
PEER_TOPK = 16
PEER_BLOCK = 128

kernel_name = 'hybrid_swa_stickbreaking_peer_block'


def rms_norm(x, gain):
    xf = x.astype(jnp.float32)
    y = xf * lax.rsqrt(jnp.mean(xf * xf, axis=-1, keepdims=True) + RMS_EPS)
    return (y * gain.astype(jnp.float32)).astype(x.dtype)


def alibi_slopes(n_heads):
    return jnp.exp2(-8.0 * jnp.arange(1, n_heads + 1, dtype=jnp.float32) / n_heads)


def sliding_window_sink_attention(q, k, v, q_gain, k_gain, sinks):
    b, s = q.shape[0], q.shape[1]
    nc = s // CHUNK
    pad = SWA_WINDOW_CHUNKS * CHUNK
    q = rms_norm(q, q_gain)
    k = rms_norm(k, k_gain)
    kp = jnp.pad(k, ((0, 0), (pad, 0), (0, 0), (0, 0))).reshape(
        b, nc + SWA_WINDOW_CHUNKS, CHUNK, SWA_KV_HEADS, HEAD_DIM)
    vp = jnp.pad(v, ((0, 0), (pad, 0), (0, 0), (0, 0))).reshape(
        b, nc + SWA_WINDOW_CHUNKS, CHUNK, SWA_KV_HEADS, HEAD_DIM)
    kw = jnp.concatenate([kp[:, i:i + nc] for i in range(SWA_WINDOW_CHUNKS + 1)], axis=2)
    vw = jnp.concatenate([vp[:, i:i + nc] for i in range(SWA_WINDOW_CHUNKS + 1)], axis=2)
    qc = q.reshape(b, nc, CHUNK, SWA_KV_HEADS, SWA_GROUP, HEAD_DIM)
    logits = jnp.einsum('bcqhgd,bckhd->bhgcqk', qc, kw).astype(jnp.float32) * (HEAD_DIM ** -0.5)
    dist = jnp.abs(jnp.arange(CHUNK)[:, None] + pad - jnp.arange(SWA_KEYS)[None, :]).astype(jnp.float32)
    slopes = alibi_slopes(SWA_Q_HEADS).reshape(SWA_KV_HEADS, SWA_GROUP)
    logits = logits - slopes[None, :, :, None, None, None] * dist
    key_pos = jnp.arange(nc)[:, None] * CHUNK - pad + jnp.arange(SWA_KEYS)[None, :]
    logits = jnp.where((key_pos >= 0)[:, None, :], logits, NEG_INF)
    sink = jnp.broadcast_to(
        sinks.astype(jnp.float32).reshape(1, SWA_KV_HEADS, SWA_GROUP, 1, 1, 1),
        logits.shape[:-1] + (1,))
    probs = jax.nn.softmax(jnp.concatenate([logits, sink], axis=-1), axis=-1)[..., :-1]
    out = jnp.einsum('bhgcqk,bckhd->bcqhgd', probs.astype(v.dtype), vw)
    return out.reshape(b, s, SWA_WIDTH)


def stick_breaking_attention(q, k, v):
    b, s, h, dh = q.shape
    nb = s // SB_BLOCK
    qb = q.reshape(b, nb, SB_BLOCK, h, dh).transpose(1, 0, 2, 3, 4)
    key_pos = jnp.arange(s)

    def block(args):
        q_blk, blk_idx = args
        z = jnp.einsum('bqhd,bkhd->bhqk', q_blk, k).astype(jnp.float32) * (dh ** -0.5)
        q_pos = blk_idx * SB_BLOCK + jnp.arange(SB_BLOCK)
        before = key_pos[None, :] < q_pos[:, None]
        log_beta = jax.nn.log_sigmoid(z)
        log_keep = jnp.where(before, jax.nn.log_sigmoid(-z), 0.0)
        suffix = lax.cumsum(log_keep, axis=3, reverse=True) - log_keep
        weights = jnp.where(before, jnp.exp(log_beta + suffix), 0.0)
        return jnp.einsum('bhqk,bkhd->bqhd', weights.astype(v.dtype), v)

    out = lax.map(block, (qb, jnp.arange(nb)))
    return out.transpose(1, 0, 2, 3, 4).reshape(b, s, h * dh)


def peer_ffn(x, w_q, sub_keys, u, v):
    b, s, d = x.shape
    tokens = x.reshape(-1, PEER_BLOCK, d)

    def block(xb):
        q = (xb @ w_q).reshape(PEER_BLOCK, PEER_HEADS, 2, PEER_HALF)
        scores = jnp.einsum('thpd,hpkd->thpk', q, sub_keys).astype(jnp.float32)
        top_s, top_i = lax.top_k(scores, PEER_TOPK)
        cand_s = (top_s[:, :, 0, :, None] + top_s[:, :, 1, None, :]).reshape(
            PEER_BLOCK, PEER_HEADS, PEER_TOPK * PEER_TOPK)
        cand_i = (top_i[:, :, 0, :, None] * PEER_N_KEYS + top_i[:, :, 1, None, :]).reshape(
            PEER_BLOCK, PEER_HEADS, PEER_TOPK * PEER_TOPK)
        best_s, best_pos = lax.top_k(cand_s, PEER_TOPK)
        expert = jnp.take_along_axis(cand_i, best_pos, axis=-1)
        gate = jax.nn.softmax(best_s, axis=-1)
        act = jax.nn.gelu(jnp.einsum('thkd,td->thk', u[expert], xb).astype(jnp.float32),
                          approximate=False)
        return jnp.einsum('thk,thkd->td', (gate * act).astype(v.dtype), v[expert])

    return lax.map(block, tokens).reshape(b, s, d)


def setup_inputs(seed: int = 0) -> dict:
    key = jax.random.key(seed)
    ks = jax.random.split(key, 16)

    def normal(k, shape, scale):
        return jax.random.normal(k, shape, jnp.float32) * scale

    return {
        'x': normal(ks[0], (BATCH, SEQ, D_MODEL), 1.0),
        'mix_norm_gain': 1.0 + normal(ks[1], (DEPTH, D_MODEL), 0.02),
        'w_in': normal(ks[2], (DEPTH, D_MODEL, IN_WIDTH), D_MODEL ** -0.5),
        'gate_bias': normal(ks[3], (DEPTH, N_BRANCHES * D_MODEL), 0.1),
        'swa_q_gain': 1.0 + normal(ks[4], (DEPTH, HEAD_DIM), 0.02),
        'swa_k_gain': 1.0 + normal(ks[5], (DEPTH, HEAD_DIM), 0.02),
        'swa_sinks': normal(ks[6], (DEPTH, SWA_Q_HEADS), 0.5),
        'w_up_swa': normal(ks[7], (DEPTH, SWA_WIDTH, D_MODEL), SWA_WIDTH ** -0.5),
        'w_up_sb': normal(ks[8], (DEPTH, SB_WIDTH, D_MODEL), SB_WIDTH ** -0.5),
        'w_out': normal(ks[9], (DEPTH, D_MODEL, D_MODEL), D_MODEL ** -0.5),
        'ffn_norm_gain': 1.0 + normal(ks[10], (DEPTH, D_MODEL), 0.02),
        'peer_w_q': normal(ks[11], (DEPTH, D_MODEL, PEER_HEADS * PEER_QUERY_DIM), D_MODEL ** -0.5),
        'peer_sub_keys': normal(ks[12], (DEPTH, PEER_HEADS, 2, PEER_N_KEYS, PEER_HALF), PEER_HALF ** -0.5),
        'peer_u': normal(ks[13], (DEPTH, PEER_N_EXPERTS, D_MODEL), D_MODEL ** -0.5),
        'peer_v': normal(ks[14], (DEPTH, PEER_N_EXPERTS, D_MODEL), PEER_HEADS ** -0.5),
    }


def reference(x, mix_norm_gain, w_in, gate_bias, swa_q_gain, swa_k_gain, swa_sinks,
              w_up_swa, w_up_sb, w_out, ffn_norm_gain, peer_w_q, peer_sub_keys, peer_u, peer_v):
    b, s, d = x.shape
    for layer in range(DEPTH):
        h = rms_norm(x, mix_norm_gain[layer])
        proj = h @ w_in[layer]
        q_a, k_a, v_a, q_b, k_b, v_b, gate_logits = jnp.split(proj, SPLIT_POINTS, axis=-1)
        y_a = sliding_window_sink_attention(
            q_a.reshape(b, s, SWA_Q_HEADS, HEAD_DIM),
            k_a.reshape(b, s, SWA_KV_HEADS, HEAD_DIM),
            v_a.reshape(b, s, SWA_KV_HEADS, HEAD_DIM),
            swa_q_gain[layer], swa_k_gain[layer], swa_sinks[layer])
        y_b = stick_breaking_attention(
            q_b.reshape(b, s, SB_HEADS, HEAD_DIM),
            k_b.reshape(b, s, SB_HEADS, HEAD_DIM),
            v_b.reshape(b, s, SB_HEADS, HEAD_DIM))
        gates = jax.nn.sigmoid(gate_logits + gate_bias[layer]).reshape(b, s, N_BRANCHES, d)
        merged = gates[:, :, 0] * (y_a @ w_up_swa[layer]) + gates[:, :, 1] * (y_b @ w_up_sb[layer])
        x = x + merged @ w_out[layer]
        h = rms_norm(x, ffn_norm_gain[layer])
        x = x + peer_ffn(h, peer_w_q[layer], peer_sub_keys[layer], peer_u[layer], peer_v[layer])
    return x
```

```python
import functools
import math

import jax
import jax.numpy as jnp
from jax import lax
from jax.experimental import pallas as pl
from jax.experimental.pallas import tpu as pltpu

F32 = jnp.float32
BF16 = jnp.bfloat16

D_MODEL = 1024
HEAD_DIM = 64
CHUNK = 64
RMS_EPS = 1e-6
NEG_INF = -1e30

SWA_Q_HEADS = 8
SWA_GROUP = 4
SWA_WIDTH = 512
SWA_KV_WIDTH = 128
SB_HEADS = 8
SB_WIDTH = 512
GATE_WIDTH = 2 * D_MODEL
IN_WIDTH = SWA_WIDTH + 2 * SWA_KV_WIDTH + 3 * SB_WIDTH + GATE_WIDTH

PEER_HEADS = 8
PEER_KEYS = 128
PEER_HALF = 128
PEER_TOPK = 16
PEER_EXPERTS = PEER_KEYS * PEER_KEYS

LANES = 128
SUBLANES = 8
BLK = 128

INPROJ_TM = 512
MERGE_TM = 256
SELECT_TT = 256
EXPERT_TT = 512
EXPERT_EB = 512
VMEM_LIMIT = 56 * 1024 * 1024

_NT = (((1,), (1,)), ((), ()))


def _cparams(sem):
    return pltpu.CompilerParams(dimension_semantics=sem, vmem_limit_bytes=VMEM_LIMIT)


def _inproj_kernel(x_ref, g_ref, w_ref, qa_ref, ka_ref, va_ref, qb_ref, kb_ref, vb_ref, gl_ref):
    x = x_ref[...]
    ms = jnp.mean(x * x, axis=-1, keepdims=True)
    h = (x * lax.rsqrt(ms + RMS_EPS) * g_ref[...]).astype(BF16)

    def proj(lo, hi):
        return jnp.dot(h, w_ref[:, lo:hi], preferred_element_type=F32)

    o = 0
    qa_ref[...] = proj(o, o + SWA_WIDTH); o += SWA_WIDTH
    ka_ref[...] = proj(o, o + SWA_KV_WIDTH); o += SWA_KV_WIDTH
    va_ref[...] = proj(o, o + SWA_KV_WIDTH).astype(BF16); o += SWA_KV_WIDTH
    qb_ref[...] = (proj(o, o + SB_WIDTH) * (HEAD_DIM ** -0.5)).astype(BF16); o += SB_WIDTH
    kb_ref[...] = proj(o, o + SB_WIDTH).astype(BF16); o += SB_WIDTH
    vb_ref[...] = proj(o, o + SB_WIDTH).astype(BF16); o += SB_WIDTH
    gl_ref[...] = proj(o, o + GATE_WIDTH)


def _inproj(x2, gain, w_in):
    t = x2.shape[0]
    tm = INPROJ_TM
    row = lambda w: pl.BlockSpec((tm, w), lambda i: (i, 0))
    full = lambda a: pl.BlockSpec(a.shape, lambda i: (0,) * a.ndim)
    widths = (SWA_WIDTH, SWA_KV_WIDTH, SWA_KV_WIDTH, SB_WIDTH, SB_WIDTH, SB_WIDTH, GATE_WIDTH)
    dtypes = (F32, F32, BF16, BF16, BF16, BF16, F32)
    return pl.pallas_call(
        _inproj_kernel,
        grid=(t // tm,),
        in_specs=[row(D_MODEL), full(gain), full(w_in)],
        out_specs=[row(w) for w in widths],
        out_shape=[jax.ShapeDtypeStruct((t, w), dt) for w, dt in zip(widths, dtypes)],
        compiler_params=_cparams(("parallel",)),
        name="inproj",
    )(x2, gain, w_in)


def _half_rms(x, lo):
    sq = x * x
    s_lo = jnp.sum(jnp.where(lo, sq, 0.0), axis=-1, keepdims=True)
    s_hi = jnp.sum(jnp.where(lo, 0.0, sq), axis=-1, keepdims=True)
    inv = jnp.where(lo, lax.rsqrt(s_lo / HEAD_DIM + RMS_EPS), lax.rsqrt(s_hi / HEAD_DIM + RMS_EPS))
    return x * inv


def _swa_kernel(sink_ref, q_ref, k_ref, v_ref, qg_ref, kg_ref, o_ref,
                kl0, kh0, kl1, kh1, vl0, vh0, vl1, vh1):
    p = pl.program_id(1)
    lo = lax.broadcasted_iota(jnp.int32, (BLK, LANES), 1) < HEAD_DIM
    kv_scratch = (kl0, kh0, kl1, kh1, vl0, vh0, vl1, vh1)

    @pl.when(p == 0)
    def _():
        for r in kv_scratch:
            r[0:BLK, :] = jnp.zeros((BLK, LANES), BF16)

    row = pl.multiple_of((p + 1) * BLK, BLK)
    kn = _half_rms(k_ref[...], lo) * kg_ref[...]
    kr = pltpu.roll(kn, HEAD_DIM, 1)
    v = v_ref[...].astype(F32)
    vr = pltpu.roll(v, HEAD_DIM, 1)
    kl0[pl.ds(row, BLK), :] = jnp.where(lo, kn, 0.0).astype(BF16)
    kh0[pl.ds(row, BLK), :] = jnp.where(lo, 0.0, kr).astype(BF16)
    kl1[pl.ds(row, BLK), :] = jnp.where(lo, kr, 0.0).astype(BF16)
    kh1[pl.ds(row, BLK), :] = jnp.where(lo, 0.0, kn).astype(BF16)
    vl0[pl.ds(row, BLK), :] = jnp.where(lo, v, 0.0).astype(BF16)
    vh0[pl.ds(row, BLK), :] = jnp.where(lo, 0.0, vr).astype(BF16)
    vl1[pl.ds(row, BLK), :] = jnp.where(lo, vr, 0.0).astype(BF16)
    vh1[pl.ds(row, BLK), :] = jnp.where(lo, 0.0, v).astype(BF16)

    qi = lax.broadcasted_iota(jnp.int32, (BLK, 2 * BLK), 0)
    kj = lax.broadcasted_iota(jnp.int32, (BLK, 2 * BLK), 1)
    dist = jnp.abs(qi + BLK - kj).astype(F32)
    qc = qi // CHUNK
    kc = kj // CHUNK
    vis = (kc >= qc) & (kc <= qc + 2) & ((kj >= BLK) | (p > 0))
    win = pl.ds(pl.multiple_of(p * BLK, BLK), 2 * BLK)

    for m in range(SWA_Q_HEADS // 2):
        g = (2 * m) // SWA_GROUP
        kv = ((kl0, vl0), (kh0, vh0)) if g == 0 else ((kl1, vl1), (kh1, vh1))
        qn = _half_rms(q_ref[:, m * LANES:(m + 1) * LANES], lo) * qg_ref[...]
        qs = (qn * (HEAD_DIM ** -0.5)).astype(BF16)
        acc = jnp.zeros((BLK, LANES), F32)
        for which, (kref, vref) in enumerate(kv):
            h = 2 * m + which
            slope = 2.0 ** (-(h + 1))
            logits = lax.dot_general(qs, kref[win, :], _NT, preferred_element_type=F32)
            logits = jnp.where(vis, logits - slope * dist, NEG_INF)
            sink = sink_ref[h]
            mx = jnp.maximum(jnp.max(logits, axis=-1, keepdims=True), sink)
            pe = jnp.exp(logits - mx)
            den = jnp.sum(pe, axis=-1, keepdims=True) + jnp.exp(sink - mx)
            probs = (pe / den).astype(BF16)
            acc = acc + jnp.dot(probs, vref[win, :], preferred_element_type=F32)
        o_ref[:, m * LANES:(m + 1) * LANES] = acc.astype(BF16)


def _swa(qa, ka, va, qgain, kgain, sinks, batch, seq):
    nb = seq // BLK
    qg2 = jnp.concatenate([qgain, qgain]).reshape(1, LANES)
    kg2 = jnp.concatenate([kgain, kgain]).reshape(1, LANES)
    blk = lambda w: pl.BlockSpec((BLK, w), lambda b, p: (b * nb + p, 0))
    one = pl.BlockSpec((1, LANES), lambda b, p: (0, 0))
    return pl.pallas_call(
        _swa_kernel,
        grid=(batch, nb),
        in_specs=[pl.BlockSpec(memory_space=pltpu.SMEM), blk(SWA_WIDTH), blk(SWA_KV_WIDTH),
                  blk(SWA_KV_WIDTH), one, one],
        out_specs=blk(SWA_WIDTH),
        out_shape=jax.ShapeDtypeStruct((batch * seq, SWA_WIDTH), BF16),
        scratch_shapes=[pltpu.VMEM((seq + BLK, LANES), BF16)] * 8,
        compiler_params=_cparams(("parallel", "arbitrary")),
        name="swa",
    )(sinks, qa, ka, va, qg2, kg2)


def _sb_kernel(q_ref, k_ref, v_ref, tri_ref, o_ref, kl, kh, vl, vh, carry, acc):
    seq = q_ref.shape[0]
    nb = seq // BLK
    lo = lax.broadcasted_iota(jnp.int32, (seq, LANES), 1) < HEAD_DIM
    k = k_ref[...]
    v = v_ref[...]
    zero = jnp.zeros_like(k)
    kl[...] = jnp.where(lo, k, zero)
    kh[...] = jnp.where(lo, zero, k)
    vl[...] = jnp.where(lo, v, zero)
    vh[...] = jnp.where(lo, zero, v)

    ri = lax.broadcasted_iota(jnp.int32, (BLK, BLK), 0)
    ci = lax.broadcasted_iota(jnp.int32, (BLK, BLK), 1)
    before = ci < ri

    def tile(qi, kj, diag):
        q = q_ref[pl.ds(pl.multiple_of(qi * BLK, BLK), BLK), :]
        ks = pl.ds(pl.multiple_of(kj * BLK, BLK), BLK)
        out = acc[...]
        for hh, (kref, vref) in enumerate(((kl, vl), (kh, vh))):
            z = lax.dot_general(q, kref[ks, :], _NT, preferred_element_type=F32)
            log_keep = -(jnp.maximum(z, 0.0) + jnp.log1p(jnp.exp(-jnp.abs(z))))
            log_beta = log_keep + z
            if diag:
                log_keep = jnp.where(before, log_keep, 0.0)
            hi = log_keep.astype(BF16)
            lw = (log_keep - hi.astype(F32)).astype(BF16)
            sc = jnp.dot(jnp.concatenate([hi, lw], axis=1), tri_ref[...], preferred_element_type=F32)
            suffix = sc[:, :BLK] + carry[hh]
            carry[hh] = carry[hh] + sc[:, BLK:]
            w = jnp.exp(log_beta + suffix)
            if diag:
                w = jnp.where(before, w, 0.0)
            out = out + jnp.dot(w.astype(BF16), vref[ks, :], preferred_element_type=F32)
        acc[...] = out

    def qblock(qi, c):
        carry[...] = jnp.zeros(carry.shape, F32)
        acc[...] = jnp.zeros(acc.shape, F32)
        tile(qi, qi, True)

        def inner(t, c2):
            tile(qi, qi - 1 - t, False)
            return c2

        lax.fori_loop(0, qi, inner, 0)
        o_ref[pl.ds(pl.multiple_of(qi * BLK, BLK), BLK), :] = acc[...].astype(BF16)
        return c

    lax.fori_loop(0, nb, qblock, 0)


def _suffix_matrix():
    j = jnp.arange(BLK)[:, None]
    s = jnp.arange(BLK)[None, :]
    strict = (j > s).astype(BF16)
    half = jnp.concatenate([strict, jnp.ones((BLK, BLK), BF16)], axis=1)
    return jnp.concatenate([half, half], axis=0)


def _stickbreak(qb, kb, vb, batch, seq):
    npair = SB_HEADS // 2
    blk = pl.BlockSpec((seq, LANES), lambda b, m: (b, m))
    tri = _suffix_matrix()
    return pl.pallas_call(
        _sb_kernel,
        grid=(batch, npair),
        in_specs=[blk, blk, blk, pl.BlockSpec(tri.shape, lambda b, m: (0, 0))],
        out_specs=blk,
        out_shape=jax.ShapeDtypeStruct((batch * seq, SB_WIDTH), BF16),
        scratch_shapes=[pltpu.VMEM((seq, LANES), BF16)] * 4
        + [pltpu.VMEM((2, BLK, BLK), F32), pltpu.VMEM((BLK, LANES), F32)],
        compiler_params=_cparams(("parallel", "parallel")),
        name="stickbreak",
    )(qb, kb, vb, tri)


def _merge_kernel(ya_ref, yb_ref, gl_ref, bias_ref, x_ref, wa_ref, wb_ref, wo_ref, g2_ref, wq_ref,
                  x1_ref, h2_ref, qt_ref):
    ua = jnp.dot(ya_ref[...], wa_ref[...], preferred_element_type=F32)
    ub = jnp.dot(yb_ref[...], wb_ref[...], preferred_element_type=F32)
    gates = jax.nn.sigmoid(gl_ref[...] + bias_ref[...])
    merged = gates[:, :D_MODEL] * ua + gates[:, D_MODEL:] * ub
    x1 = x_ref[...] + jnp.dot(merged.astype(BF16), wo_ref[...], preferred_element_type=F32)
    x1_ref[...] = x1
    ms = jnp.mean(x1 * x1, axis=-1, keepdims=True)
    h2 = (x1 * lax.rsqrt(ms + RMS_EPS) * g2_ref[...]).astype(BF16)
    h2_ref[...] = h2
    qt_ref[...] = lax.dot_general(wq_ref[...], h2, _NT, preferred_element_type=F32).astype(BF16)


def _merge(ya, yb, gl, bias, x2, wa, wb, wo, g2, wq_t):
    t = x2.shape[0]
    tm = MERGE_TM
    row = lambda w: pl.BlockSpec((tm, w), lambda i: (i, 0))
    full = lambda a: pl.BlockSpec(a.shape, lambda i: (0,) * a.ndim)
    nq = wq_t.shape[0]
    return pl.pallas_call(
        _merge_kernel,
        grid=(t // tm,),
        in_specs=[row(SWA_WIDTH), row(SB_WIDTH), row(GATE_WIDTH), full(bias), row(D_MODEL),
                  full(wa), full(wb), full(wo), full(g2), full(wq_t)],
        out_specs=[row(D_MODEL), row(D_MODEL), pl.BlockSpec((nq, tm), lambda i: (0, i))],
        out_shape=[jax.ShapeDtypeStruct((t, D_MODEL), F32),
                   jax.ShapeDtypeStruct((t, D_MODEL), BF16),
                   jax.ShapeDtypeStruct((nq, t), BF16)],
        compiler_params=_cparams(("parallel",)),
        name="merge",
    )(ya, yb, gl, bias, x2, wa, wb, wo, g2, wq_t)


_CAND_GROUPS = ((0, 16), (1, 8), (2, 5), (3, 4), (4, 3), (5, 2), (6, 2), (7, 2))
_CAND_ROWS = 16 + 7 * SUBLANES + SUBLANES


def _top16(s):
    n, tt = s.shape
    pos = lax.broadcasted_iota(jnp.int32, (n, tt), 0)
    rank = jnp.full((n, tt), float(PEER_TOPK), F32)
    vals, idxs = [], []
    for r in range(PEER_TOPK):
        m = jnp.max(s, axis=0, keepdims=True)
        idx = jnp.min(jnp.where(s == m, pos, n), axis=0, keepdims=True)
        hit = pos == idx
        rank = jnp.where(hit, float(r), rank)
        s = jnp.where(hit, -jnp.inf, s)
        vals.append(m)
        idxs.append(idx)
    return vals, idxs, rank


def _select_kernel(q_ref, sk_ref, a_ref, c_ref, b_ref, r_ref):
    tt = q_ref.shape[1]
    s0 = jnp.dot(sk_ref[0, 0], q_ref[0:PEER_HALF, :], preferred_element_type=F32)
    s1 = jnp.dot(sk_ref[0, 1], q_ref[PEER_HALF:2 * PEER_HALF, :], preferred_element_type=F32)
    v0, i0, _ = _top16(s0)
    v1, _, rank1 = _top16(s1)

    row8 = lax.broadcasted_iota(jnp.int32, (SUBLANES, tt), 0)
    stack8 = lambda rows: functools.reduce(
        lambda acc, kv: jnp.where(row8 == kv[0], kv[1], acc), enumerate(rows), jnp.zeros((SUBLANES, tt), F32))
    v1_lo, v1_hi = stack8(v1[:8]), stack8(v1[8:])
    v0_hi = stack8(v0[8:])

    pieces = []
    for r0, n in _CAND_GROUPS:
        lo_piece = v0[r0] + v1_lo
        pieces.append(lo_piece if n >= SUBLANES else jnp.where(row8 < n, lo_piece, -jnp.inf))
        if n > SUBLANES:
            pieces.append(v0[r0] + v1_hi)
    pieces.append(v0_hi + v1[0])
    cand = jnp.concatenate(pieces, axis=0)

    pos = lax.broadcasted_iota(jnp.int32, cand.shape, 0)
    chosen = jnp.zeros(cand.shape, F32)
    c = cand
    for _ in range(PEER_TOPK):
        m = jnp.max(c, axis=0, keepdims=True)
        idx = jnp.min(jnp.where(c == m, pos, _CAND_ROWS), axis=0, keepdims=True)
        hit = pos == idx
        chosen = jnp.where(hit, 1.0, chosen)
        c = jnp.where(hit, -jnp.inf, c)

    best = v0[0] + v1[0]
    z = jnp.sum(jnp.where(chosen > 0.0, jnp.exp(cand - best), 0.0), axis=0, keepdims=True)

    counts = []
    off = 0
    for r0, n in _CAND_GROUPS:
        rows = SUBLANES * ((n + SUBLANES - 1) // SUBLANES)
        counts.append(jnp.sum(chosen[off:off + rows], axis=0, keepdims=True))
        off += rows
    for r in range(SUBLANES):
        counts.append(chosen[off + r:off + r + 1])

    key = lax.broadcasted_iota(jnp.int32, (PEER_KEYS, tt), 0)
    c0 = jnp.zeros((PEER_KEYS, tt), F32)
    for r in range(PEER_TOPK):
        c0 = jnp.where(key == i0[r], counts[r], c0)

    a_ref[0] = jnp.exp(s0 - v0[0]) * (1.0 / z)
    c_ref[0] = c0
    b_ref[0] = jnp.exp(s1 - v1[0])
    r_ref[0] = rank1


def _peer_select(qt, sub_keys):
    t = qt.shape[1]
    tt = SELECT_TT
    out = pl.BlockSpec((1, PEER_KEYS, tt), lambda i, h: (h, 0, i))
    shp = jax.ShapeDtypeStruct((PEER_HEADS, PEER_KEYS, t), F32)
    return pl.pallas_call(
        _select_kernel,
        grid=(t // tt, PEER_HEADS),
        in_specs=[pl.BlockSpec((2 * PEER_HALF, tt), lambda i, h: (h, i)),
                  pl.BlockSpec((1, 2, PEER_KEYS, PEER_HALF), lambda i, h: (h, 0, 0, 0))],
        out_specs=[out] * 4,
        out_shape=[shp] * 4,
        compiler_params=_cparams(("parallel", "parallel")),
        name="peer_select",
    )(qt, sub_keys)


def _experts_kernel(h_ref, a_ref, c_ref, b_ref, r_ref, u_ref, vt_ref, x1_ref, o_ref, acc_ref, g_ref):
    e = pl.program_id(1)
    nsub = u_ref.shape[0] // PEER_KEYS

    @pl.when(e == 0)
    def _():
        acc_ref[...] = jnp.zeros(acc_ref.shape, F32)

    act = lax.dot_general(u_ref[...], h_ref[...], _NT, preferred_element_type=F32)
    for ii in range(nsub):
        i = e * nsub + ii
        w = jnp.zeros((PEER_KEYS, act.shape[1]), F32)
        for h in range(PEER_HEADS):
            a_row = a_ref[h, pl.ds(i, 1), :]
            c_row = c_ref[h, pl.ds(i, 1), :]
            w = w + jnp.where(r_ref[h] < c_row, a_row * b_ref[h], 0.0)
        x = act[ii * PEER_KEYS:(ii + 1) * PEER_KEYS]
        gelu = 0.5 * x * (1.0 + lax.erf(x * math.sqrt(0.5)))
        g_ref[ii * PEER_KEYS:(ii + 1) * PEER_KEYS, :] = (w * gelu).astype(BF16)
    acc_ref[...] += jnp.dot(vt_ref[...], g_ref[...], preferred_element_type=F32)

    @pl.when(e == pl.num_programs(1) - 1)
    def _():
        o_ref[...] = x1_ref[...] + acc_ref[...].T


def _peer_experts(h2, a, c0, b, r1, u, vt, x1):
    t = h2.shape[0]
    tt, eb = EXPERT_TT, EXPERT_EB
    sel = pl.BlockSpec((PEER_HEADS, PEER_KEYS, tt), lambda i, e: (0, 0, i))
    row = pl.BlockSpec((tt, D_MODEL), lambda i, e: (i, 0))
    return pl.pallas_call(
        _experts_kernel,
        grid=(t // tt, PEER_EXPERTS // eb),
        in_specs=[row, sel, sel, sel, sel,
                  pl.BlockSpec((eb, D_MODEL), lambda i, e: (e, 0)),
                  pl.BlockSpec((D_MODEL, eb), lambda i, e: (0, e)),
                  row],
        out_specs=row,
        out_shape=jax.ShapeDtypeStruct((t, D_MODEL), F32),
        scratch_shapes=[pltpu.VMEM((D_MODEL, tt), F32), pltpu.VMEM((eb, tt), BF16)],
        compiler_params=_cparams(("parallel", "arbitrary")),
        name="peer_experts",
    )(h2, a, c0, b, r1, u, vt, x1)


def _layer(x2, batch, seq, mix_gain, w_in, gate_bias, q_gain, k_gain, sinks,
           w_up_swa, w_up_sb, w_out, ffn_gain, w_q, sub_keys, u, v):
    qa, ka, va, qb, kb, vb, gl = _inproj(x2, mix_gain.reshape(1, -1), w_in.astype(BF16))
    ya = _swa(qa, ka, va, q_gain, k_gain, sinks, batch, seq)
    yb = _stickbreak(qb, kb, vb, batch, seq)
    x1, h2, qt = _merge(ya, yb, gl, gate_bias.reshape(1, -1), x2,
                        w_up_swa.astype(BF16), w_up_sb.astype(BF16), w_out.astype(BF16),
                        ffn_gain.reshape(1, -1), w_q.T.astype(BF16))
    a, c0, b, r1 = _peer_select(qt, sub_keys.astype(BF16))
    return _peer_experts(h2, a, c0, b, r1, u.astype(BF16), v.T.astype(BF16), x1)


def kernel(x, mix_norm_gain, w_in, gate_bias, swa_q_gain, swa_k_gain, swa_sinks, w_up_swa, w_up_sb,
           w_out, ffn_norm_gain, peer_w_q, peer_sub_keys, peer_u, peer_v):
    batch, seq, d = x.shape
    x2 = x.reshape(batch * seq, d)
    for layer in range(mix_norm_gain.shape[0]):
        x2 = _layer(x2, batch, seq, mix_norm_gain[layer], w_in[layer], gate_bias[layer],
                    swa_q_gain[layer], swa_k_gain[layer], swa_sinks[layer], w_up_swa[layer],
                    w_up_sb[layer], w_out[layer], ffn_norm_gain[layer], peer_w_q[layer],
                    peer_sub_keys[layer], peer_u[layer], peer_v[layer])
    return x2.reshape(batch, seq, d)
```

```python
import functools
import math

import jax
import jax.numpy as jnp
from jax import lax
from jax.experimental import pallas as pl
from jax.experimental.pallas import tpu as pltpu

F32 = jnp.float32
BF16 = jnp.bfloat16

D_MODEL = 1024
HEAD_DIM = 64
CHUNK = 64
RMS_EPS = 1e-6
NEG_INF = -1e30

SWA_Q_HEADS = 8
SWA_GROUP = 4
SWA_WIDTH = 512
SWA_KV_WIDTH = 128
SB_HEADS = 8
SB_WIDTH = 512
GATE_WIDTH = 2 * D_MODEL
IN_WIDTH = SWA_WIDTH + 2 * SWA_KV_WIDTH + 3 * SB_WIDTH + GATE_WIDTH

PEER_HEADS = 8
PEER_KEYS = 128
PEER_HALF = 128
PEER_TOPK = 16
PEER_EXPERTS = PEER_KEYS * PEER_KEYS

LANES = 128
SUBLANES = 8
BLK = 128

INPROJ_TM = 512
MERGE_TM = 256
SELECT_TT = 256
EXPERT_TT = 512
EXPERT_EB = 512
EXPERT_JCHUNK = 32
SB_QROWS = 512
VMEM_LIMIT = 56 * 1024 * 1024

_NT = (((1,), (1,)), ((), ()))


def _cparams(sem):
    return pltpu.CompilerParams(dimension_semantics=sem, vmem_limit_bytes=VMEM_LIMIT)


def _inproj_kernel(x_ref, g_ref, w_ref, qa_ref, ka_ref, va_ref, qb_ref, kb_ref, vb_ref, gl_ref):
    x = x_ref[...]
    ms = jnp.mean(x * x, axis=-1, keepdims=True)
    h = (x * lax.rsqrt(ms + RMS_EPS) * g_ref[...]).astype(BF16)

    def proj(lo, hi):
        return jnp.dot(h, w_ref[:, lo:hi], preferred_element_type=F32)

    o = 0
    qa_ref[...] = proj(o, o + SWA_WIDTH); o += SWA_WIDTH
    ka_ref[...] = proj(o, o + SWA_KV_WIDTH); o += SWA_KV_WIDTH
    va_ref[...] = proj(o, o + SWA_KV_WIDTH).astype(BF16); o += SWA_KV_WIDTH
    qb_ref[...] = (proj(o, o + SB_WIDTH) * (HEAD_DIM ** -0.5)).astype(BF16); o += SB_WIDTH
    kb_ref[...] = proj(o, o + SB_WIDTH).astype(BF16); o += SB_WIDTH
    vb_ref[...] = proj(o, o + SB_WIDTH).astype(BF16); o += SB_WIDTH
    gl_ref[...] = proj(o, o + GATE_WIDTH)


def _inproj(x2, gain, w_in):
    t = x2.shape[0]
    tm = INPROJ_TM
    row = lambda w: pl.BlockSpec((tm, w), lambda i: (i, 0))
    full = lambda a: pl.BlockSpec(a.shape, lambda i: (0,) * a.ndim)
    widths = (SWA_WIDTH, SWA_KV_WIDTH, SWA_KV_WIDTH, SB_WIDTH, SB_WIDTH, SB_WIDTH, GATE_WIDTH)
    dtypes = (F32, F32, BF16, BF16, BF16, BF16, F32)
    return pl.pallas_call(
        _inproj_kernel,
        grid=(t // tm,),
        in_specs=[row(D_MODEL), full(gain), full(w_in)],
        out_specs=[row(w) for w in widths],
        out_shape=[jax.ShapeDtypeStruct((t, w), dt) for w, dt in zip(widths, dtypes)],
        compiler_params=_cparams(("parallel",)),
        name="inproj",
    )(x2, gain, w_in)


def _half_rms(x, lo):
    sq = x * x
    s_lo = jnp.sum(jnp.where(lo, sq, 0.0), axis=-1, keepdims=True)
    s_hi = jnp.sum(jnp.where(lo, 0.0, sq), axis=-1, keepdims=True)
    inv = jnp.where(lo, lax.rsqrt(s_lo / HEAD_DIM + RMS_EPS), lax.rsqrt(s_hi / HEAD_DIM + RMS_EPS))
    return x * inv


def _swa_kernel(sink_ref, q_ref, k_ref, v_ref, qg_ref, kg_ref, o_ref,
                kl0, kh0, kl1, kh1, vl0, vh0, vl1, vh1):
    p = pl.program_id(1)
    lo = lax.broadcasted_iota(jnp.int32, (BLK, LANES), 1) < HEAD_DIM
    kv_scratch = (kl0, kh0, kl1, kh1, vl0, vh0, vl1, vh1)

    @pl.when(p == 0)
    def _():
        for r in kv_scratch:
            r[0:BLK, :] = jnp.zeros((BLK, LANES), BF16)

    row = pl.multiple_of((p + 1) * BLK, BLK)
    kn = _half_rms(k_ref[...], lo) * kg_ref[...]
    kr = pltpu.roll(kn, HEAD_DIM, 1)
    v = v_ref[...].astype(F32)
    vr = pltpu.roll(v, HEAD_DIM, 1)
    kl0[pl.ds(row, BLK), :] = jnp.where(lo, kn, 0.0).astype(BF16)
    kh0[pl.ds(row, BLK), :] = jnp.where(lo, 0.0, kr).astype(BF16)
    kl1[pl.ds(row, BLK), :] = jnp.where(lo, kr, 0.0).astype(BF16)
    kh1[pl.ds(row, BLK), :] = jnp.where(lo, 0.0, kn).astype(BF16)
    vl0[pl.ds(row, BLK), :] = jnp.where(lo, v, 0.0).astype(BF16)
    vh0[pl.ds(row, BLK), :] = jnp.where(lo, 0.0, vr).astype(BF16)
    vl1[pl.ds(row, BLK), :] = jnp.where(lo, vr, 0.0).astype(BF16)
    vh1[pl.ds(row, BLK), :] = jnp.where(lo, 0.0, v).astype(BF16)

    qi = lax.broadcasted_iota(jnp.int32, (BLK, 2 * BLK), 0)
    kj = lax.broadcasted_iota(jnp.int32, (BLK, 2 * BLK), 1)
    dist = jnp.abs(qi + BLK - kj).astype(F32)
    qc = qi // CHUNK
    kc = kj // CHUNK
    vis = (kc >= qc) & (kc <= qc + 2) & ((kj >= BLK) | (p > 0))
    win = pl.ds(pl.multiple_of(p * BLK, BLK), 2 * BLK)

    for m in range(SWA_Q_HEADS // 2):
        g = (2 * m) // SWA_GROUP
        kv = ((kl0, vl0), (kh0, vh0)) if g == 0 else ((kl1, vl1), (kh1, vh1))
        qn = _half_rms(q_ref[:, m * LANES:(m + 1) * LANES], lo) * qg_ref[...]
        qs = (qn * (HEAD_DIM ** -0.5)).astype(BF16)
        acc = jnp.zeros((BLK, LANES), F32)
        for which, (kref, vref) in enumerate(kv):
            h = 2 * m + which
            slope = 2.0 ** (-(h + 1))
            logits = lax.dot_general(qs, kref[win, :], _NT, preferred_element_type=F32)
            logits = jnp.where(vis, logits - slope * dist, NEG_INF)
            sink = sink_ref[h]
            mx = jnp.maximum(jnp.max(logits, axis=-1, keepdims=True), sink)
            pe = jnp.exp(logits - mx)
            den = jnp.sum(pe, axis=-1, keepdims=True) + jnp.exp(sink - mx)
            probs = (pe / den).astype(BF16)
            acc = acc + jnp.dot(probs, vref[win, :], preferred_element_type=F32)
        o_ref[:, m * LANES:(m + 1) * LANES] = acc.astype(BF16)


def _swa(qa, ka, va, qgain, kgain, sinks, batch, seq):
    nb = seq // BLK
    qg2 = jnp.concatenate([qgain, qgain]).reshape(1, LANES)
    kg2 = jnp.concatenate([kgain, kgain]).reshape(1, LANES)
    blk = lambda w: pl.BlockSpec((BLK, w), lambda b, p: (b * nb + p, 0))
    one = pl.BlockSpec((1, LANES), lambda b, p: (0, 0))
    return pl.pallas_call(
        _swa_kernel,
        grid=(batch, nb),
        in_specs=[pl.BlockSpec(memory_space=pltpu.SMEM), blk(SWA_WIDTH), blk(SWA_KV_WIDTH),
                  blk(SWA_KV_WIDTH), one, one],
        out_specs=blk(SWA_WIDTH),
        out_shape=jax.ShapeDtypeStruct((batch * seq, SWA_WIDTH), BF16),
        scratch_shapes=[pltpu.VMEM((seq + BLK, LANES), BF16)] * 8,
        compiler_params=_cparams(("parallel", "arbitrary")),
        name="swa",
    )(sinks, qa, ka, va, qg2, kg2)


def _sb_kernel(q_ref, k_ref, v_ref, tri_ref, o_ref, kk, vv, carry, acc):
    seq = q_ref.shape[0]
    nb = seq // BLK
    per = SB_QROWS // BLK
    lo = lax.broadcasted_iota(jnp.int32, (nb, BLK, LANES), 2) < HEAD_DIM
    k3 = k_ref[...].reshape(nb, BLK, LANES)
    v3 = v_ref[...].reshape(nb, BLK, LANES)
    zero = jnp.zeros_like(k3)
    kk[:, 0:BLK, :] = jnp.where(lo, k3, zero)
    kk[:, BLK:2 * BLK, :] = jnp.where(lo, zero, k3)
    vv[:, 0:BLK, :] = jnp.where(lo, v3, zero)
    vv[:, BLK:2 * BLK, :] = jnp.where(lo, zero, v3)

    def tile(qs, kj, d):
        r0 = 0 if d is None else d * BLK
        m = SB_QROWS - r0
        q = q_ref[pl.ds(pl.multiple_of(qs * SB_QROWS + r0, BLK), m), :]
        z2 = lax.dot_general(q, kk[kj], _NT, preferred_element_type=F32)
        if d is not None:
            before = (lax.broadcasted_iota(jnp.int32, (m, BLK), 1)
                      < lax.broadcasted_iota(jnp.int32, (m, BLK), 0))
        ws = []
        for hh in range(2):
            z = z2[:, hh * BLK:(hh + 1) * BLK]
            log_keep = -(jnp.maximum(z, 0.0) + jnp.log(1.0 + jnp.exp(-jnp.abs(z))))
            log_beta = log_keep + z
            if d is not None:
                log_keep = jnp.where(before, log_keep, 0.0)
            hi = log_keep.astype(BF16)
            lw = (log_keep - hi.astype(F32)).astype(BF16)
            sc = jnp.dot(jnp.concatenate([hi, lw], axis=1), tri_ref[...], preferred_element_type=F32)
            suffix = sc[:, :BLK] + carry[hh, r0:, :]
            carry[hh, r0:, :] = carry[hh, r0:, :] + sc[:, BLK:]
            w = jnp.exp(log_beta + suffix)
            if d is not None:
                w = jnp.where(before, w, 0.0)
            ws.append(w.astype(BF16))
        acc[r0:, :] = acc[r0:, :] + jnp.dot(jnp.concatenate(ws, axis=1), vv[kj],
                                           preferred_element_type=F32)

    def qblock(qs, c):
        carry[...] = jnp.zeros(carry.shape, F32)
        acc[...] = jnp.zeros(acc.shape, F32)
        for d in reversed(range(per)):
            tile(qs, qs * per + d, d)

        def inner(t, c2):
            tile(qs, qs * per - 1 - t, None)
            return c2

        lax.fori_loop(0, qs * per, inner, 0)
        o_ref[pl.ds(pl.multiple_of(qs * SB_QROWS, BLK), SB_QROWS), :] = acc[...].astype(BF16)
        return c

    lax.fori_loop(0, seq // SB_QROWS, qblock, 0)


def _suffix_matrix():
    j = jnp.arange(BLK)[:, None]
    s = jnp.arange(BLK)[None, :]
    strict = (j > s).astype(BF16)
    half = jnp.concatenate([strict, jnp.ones((BLK, BLK), BF16)], axis=1)
    return jnp.concatenate([half, half], axis=0)


def _stickbreak(qb, kb, vb, batch, seq):
    npair = SB_HEADS // 2
    blk = pl.BlockSpec((seq, LANES), lambda b, m: (b, m))
    tri = _suffix_matrix()
    return pl.pallas_call(
        _sb_kernel,
        grid=(batch, npair),
        in_specs=[blk, blk, blk, pl.BlockSpec(tri.shape, lambda b, m: (0, 0))],
        out_specs=blk,
        out_shape=jax.ShapeDtypeStruct((batch * seq, SB_WIDTH), BF16),
        scratch_shapes=[pltpu.VMEM((seq // BLK, 2 * BLK, LANES), BF16)] * 2
        + [pltpu.VMEM((2, SB_QROWS, BLK), F32), pltpu.VMEM((SB_QROWS, LANES), F32)],
        compiler_params=_cparams(("parallel", "parallel")),
        name="stickbreak",
    )(qb, kb, vb, tri)


def _merge_kernel(ya_ref, yb_ref, gl_ref, bias_ref, x_ref, wa_ref, wb_ref, wo_ref, g2_ref, wq_ref,
                  x1_ref, h2_ref, qt_ref):
    ua = jnp.dot(ya_ref[...], wa_ref[...], preferred_element_type=F32)
    ub = jnp.dot(yb_ref[...], wb_ref[...], preferred_element_type=F32)
    gates = jax.nn.sigmoid(gl_ref[...] + bias_ref[...])
    merged = gates[:, :D_MODEL] * ua + gates[:, D_MODEL:] * ub
    x1 = x_ref[...] + jnp.dot(merged.astype(BF16), wo_ref[...], preferred_element_type=F32)
    x1_ref[...] = x1
    ms = jnp.mean(x1 * x1, axis=-1, keepdims=True)
    h2 = (x1 * lax.rsqrt(ms + RMS_EPS) * g2_ref[...]).astype(BF16)
    h2_ref[...] = h2
    qt_ref[...] = lax.dot_general(wq_ref[...], h2, _NT, preferred_element_type=F32).astype(BF16)


def _merge(ya, yb, gl, bias, x2, wa, wb, wo, g2, wq_t):
    t = x2.shape[0]
    tm = MERGE_TM
    row = lambda w: pl.BlockSpec((tm, w), lambda i: (i, 0))
    full = lambda a: pl.BlockSpec(a.shape, lambda i: (0,) * a.ndim)
    nq = wq_t.shape[0]
    return pl.pallas_call(
        _merge_kernel,
        grid=(t // tm,),
        in_specs=[row(SWA_WIDTH), row(SB_WIDTH), row(GATE_WIDTH), full(bias), row(D_MODEL),
                  full(wa), full(wb), full(wo), full(g2), full(wq_t)],
        out_specs=[row(D_MODEL), row(D_MODEL), pl.BlockSpec((nq, tm), lambda i: (0, i))],
        out_shape=[jax.ShapeDtypeStruct((t, D_MODEL), F32),
                   jax.ShapeDtypeStruct((t, D_MODEL), BF16),
                   jax.ShapeDtypeStruct((nq, t), BF16)],
        compiler_params=_cparams(("parallel",)),
        name="merge",
    )(ya, yb, gl, bias, x2, wa, wb, wo, g2, wq_t)


_CAND_GROUPS = ((0, 16), (1, 8), (2, 5), (3, 4), (4, 3), (5, 2), (6, 2), (7, 2))
_CAND_ROWS = 16 + 7 * SUBLANES + SUBLANES


def _top16(s):
    n, tt = s.shape
    pos = lax.broadcasted_iota(jnp.int32, (n, tt), 0)
    rank = jnp.full((n, tt), float(PEER_TOPK), F32)
    vals, idxs = [], []
    for r in range(PEER_TOPK):
        m = jnp.max(s, axis=0, keepdims=True)
        idx = jnp.min(jnp.where(s == m, pos, n), axis=0, keepdims=True)
        hit = pos == idx
        rank = jnp.where(hit, float(r), rank)
        s = jnp.where(hit, -jnp.inf, s)
        vals.append(m)
        idxs.append(idx)
    return vals, idxs, rank


def _select_kernel(q_ref, sk_ref, a_ref, c_ref, b_ref, r_ref):
    tt = q_ref.shape[1]
    s0 = jnp.dot(sk_ref[0, 0], q_ref[0:PEER_HALF, :], preferred_element_type=F32)
    s1 = jnp.dot(sk_ref[0, 1], q_ref[PEER_HALF:2 * PEER_HALF, :], preferred_element_type=F32)
    v0, i0, _ = _top16(s0)
    v1, _, rank1 = _top16(s1)

    row8 = lax.broadcasted_iota(jnp.int32, (SUBLANES, tt), 0)
    stack8 = lambda rows: functools.reduce(
        lambda acc, kv: jnp.where(row8 == kv[0], kv[1], acc), enumerate(rows), jnp.zeros((SUBLANES, tt), F32))
    v1_lo, v1_hi = stack8(v1[:8]), stack8(v1[8:])
    v0_hi = stack8(v0[8:])

    pieces = []
    for r0, n in _CAND_GROUPS:
        lo_piece = v0[r0] + v1_lo
        pieces.append(lo_piece if n >= SUBLANES else jnp.where(row8 < n, lo_piece, -jnp.inf))
        if n > SUBLANES:
            pieces.append(v0[r0] + v1_hi)
    pieces.append(v0_hi + v1[0])
    cand = jnp.concatenate(pieces, axis=0)

    pos = lax.broadcasted_iota(jnp.int32, cand.shape, 0)
    chosen = jnp.zeros(cand.shape, F32)
    c = cand
    for _ in range(PEER_TOPK):
        m = jnp.max(c, axis=0, keepdims=True)
        idx = jnp.min(jnp.where(c == m, pos, _CAND_ROWS), axis=0, keepdims=True)
        hit = pos == idx
        chosen = jnp.where(hit, 1.0, chosen)
        c = jnp.where(hit, -jnp.inf, c)

    best = v0[0] + v1[0]
    z = jnp.sum(jnp.where(chosen > 0.0, jnp.exp(cand - best), 0.0), axis=0, keepdims=True)

    counts = []
    off = 0
    for r0, n in _CAND_GROUPS:
        rows = SUBLANES * ((n + SUBLANES - 1) // SUBLANES)
        counts.append(jnp.sum(chosen[off:off + rows], axis=0, keepdims=True))
        off += rows
    for r in range(SUBLANES):
        counts.append(chosen[off + r:off + r + 1])

    key = lax.broadcasted_iota(jnp.int32, (PEER_KEYS, tt), 0)
    c0 = jnp.zeros((PEER_KEYS, tt), F32)
    for r in range(PEER_TOPK):
        c0 = jnp.where(key == i0[r], counts[r], c0)

    a_ref[0] = jnp.exp(s0 - v0[0]) * (1.0 / z)
    c_ref[0] = c0
    b_ref[0] = jnp.exp(s1 - v1[0])
    r_ref[0] = rank1


def _peer_select(qt, sub_keys):
    t = qt.shape[1]
    tt = SELECT_TT
    out = pl.BlockSpec((1, PEER_KEYS, tt), lambda i, h: (h, 0, i))
    shp = lambda dt: jax.ShapeDtypeStruct((PEER_HEADS, PEER_KEYS, t), dt)
    return pl.pallas_call(
        _select_kernel,
        grid=(t // tt, PEER_HEADS),
        in_specs=[pl.BlockSpec((2 * PEER_HALF, tt), lambda i, h: (h, i)),
                  pl.BlockSpec((1, 2, PEER_KEYS, PEER_HALF), lambda i, h: (h, 0, 0, 0))],
        out_specs=[out] * 4,
        out_shape=[shp(F32)] * 4,
        compiler_params=_cparams(("parallel", "parallel")),
        name="peer_select",
    )(qt, sub_keys)


def _experts_kernel(zero_ref, h_ref, a_ref, c_ref, b_ref, r_ref, u_ref, vt_ref, x1_ref, o_ref,
                    acc_ref, g_ref, abc_ref, stage_ref):
    e = pl.program_id(1)
    nsub = u_ref.shape[0] // PEER_KEYS

    @pl.when(e == 0)
    def _():
        acc_ref[...] = jnp.zeros(acc_ref.shape, F32)

    tt = h_ref.shape[0]
    half = tt // 2
    lt_per_half = half // LANES
    for ii in range(nsub):
        i = e * nsub + ii
        for h in range(PEER_HEADS):
            abc_ref[0, ii, h] = jnp.broadcast_to(a_ref[h, pl.ds(i, 1), :], (SUBLANES, tt))
            abc_ref[1, ii, h] = jnp.broadcast_to(c_ref[h, pl.ds(i, 1), :], (SUBLANES, tt))

    nv = EXPERT_JCHUNK // SUBLANES
    plane = [zero_ref[k] + k for k in range(4)]

    def gate_weights(lt):
        hf, ll = divmod(lt, lt_per_half)
        ls = slice(lt * LANES, (lt + 1) * LANES)
        for jc in range(PEER_KEYS // EXPERT_JCHUNK):
            j0 = jc * EXPERT_JCHUNK
            w = [[jnp.zeros((SUBLANES, LANES), F32)] * nv for _ in range(nsub)]
            for h in range(PEER_HEADS):
                bs = [b_ref[h, j0 + v * SUBLANES:j0 + (v + 1) * SUBLANES, ls] for v in range(nv)]
                rs = [r_ref[h, j0 + v * SUBLANES:j0 + (v + 1) * SUBLANES, ls] for v in range(nv)]
                for ii in range(nsub):
                    a8 = abc_ref[0, ii, h, :, ls]
                    c8 = abc_ref[1, ii, h, :, ls]
                    for v in range(nv):
                        w[ii][v] = w[ii][v] + jnp.where(rs[v] < c8, a8 * bs[v], 0.0)
            for ii in range(nsub):
                r0 = ii * PEER_KEYS + j0
                stage_ref[plane[2 + hf], r0:r0 + EXPERT_JCHUNK, ll * LANES:(ll + 1) * LANES] = (
                    jnp.concatenate(w[ii], axis=0))

    def activations(hf):
        hs = slice(hf * half, (hf + 1) * half)
        stage_ref[plane[hf]] = lax.dot_general(u_ref[...], h_ref[hs, :], _NT,
                                               preferred_element_type=F32)

    def gated(lt):
        hf, ll = divmod(lt, lt_per_half)
        x = stage_ref[plane[hf], :, ll * LANES:(ll + 1) * LANES]
        gelu = 0.5 * x * (1.0 + lax.erf(x * math.sqrt(0.5)))
        w = stage_ref[plane[2 + hf], :, ll * LANES:(ll + 1) * LANES]
        g_ref[:, lt * LANES:(lt + 1) * LANES] = (w * gelu).astype(BF16)

    def project(hf):
        hs = slice(hf * half, (hf + 1) * half)
        acc_ref[:, hs] += jnp.dot(vt_ref[...], g_ref[:, hs], preferred_element_type=F32)

    activations(0)
    for lt in range(lt_per_half):
        gate_weights(lt)
    activations(1)
    for lt in range(lt_per_half):
        gated(lt)
    for lt in range(lt_per_half, 2 * lt_per_half):
        gate_weights(lt)
    project(0)
    for lt in range(lt_per_half, 2 * lt_per_half):
        gated(lt)
    project(1)

    @pl.when(e == pl.num_programs(1) - 1)
    def _():
        o_ref[...] = x1_ref[...] + acc_ref[...].T


def _peer_experts(h2, a, c0, b, r1, u, vt, x1):
    t = h2.shape[0]
    tt, eb = EXPERT_TT, EXPERT_EB
    sel = pl.BlockSpec((PEER_HEADS, PEER_KEYS, tt), lambda i, e: (0, 0, i))
    row = pl.BlockSpec((tt, D_MODEL), lambda i, e: (i, 0))
    return pl.pallas_call(
        _experts_kernel,
        grid=(t // tt, PEER_EXPERTS // eb),
        in_specs=[pl.BlockSpec(memory_space=pltpu.SMEM), row, sel, sel, sel, sel,
                  pl.BlockSpec((eb, D_MODEL), lambda i, e: (e, 0)),
                  pl.BlockSpec((D_MODEL, eb), lambda i, e: (0, e)),
                  row],
        out_specs=row,
        out_shape=jax.ShapeDtypeStruct((t, D_MODEL), F32),
        scratch_shapes=[pltpu.VMEM((D_MODEL, tt), F32), pltpu.VMEM((eb, tt), BF16),
                        pltpu.VMEM((2, eb // PEER_KEYS, PEER_HEADS, SUBLANES, tt), F32),
                        pltpu.VMEM((4, eb, tt // 2), F32)],
        compiler_params=_cparams(("parallel", "arbitrary")),
        name="peer_experts",
    )(jnp.zeros((4,), jnp.int32), h2, a, c0, b, r1, u, vt, x1)


def _layer(x2, batch, seq, mix_gain, w_in, gate_bias, q_gain, k_gain, sinks,
           w_up_swa, w_up_sb, w_out, ffn_gain, w_q, sub_keys, u, v):
    qa, ka, va, qb, kb, vb, gl = _inproj(x2, mix_gain.reshape(1, -1), w_in.astype(BF16))
    ya = _swa(qa, ka, va, q_gain, k_gain, sinks, batch, seq)
    yb = _stickbreak(qb, kb, vb, batch, seq)
    x1, h2, qt = _merge(ya, yb, gl, gate_bias.reshape(1, -1), x2,
                        w_up_swa.astype(BF16), w_up_sb.astype(BF16), w_out.astype(BF16),
                        ffn_gain.reshape(1, -1), w_q.T.astype(BF16))
    a, c0, b, r1 = _peer_select(qt, sub_keys.astype(BF16))
    return _peer_experts(h2, a, c0, b, r1, u.astype(BF16), v.T.astype(BF16), x1)


def kernel(x, mix_norm_gain, w_in, gate_bias, swa_q_gain, swa_k_gain, swa_sinks, w_up_swa, w_up_sb,
           w_out, ffn_norm_gain, peer_w_q, peer_sub_keys, peer_u, peer_v):
    batch, seq, d = x.shape
    x2 = x.reshape(batch * seq, d)
    for layer in range(mix_norm_gain.shape[0]):
        x2 = _layer(x2, batch, seq, mix_norm_gain[layer], w_in[layer], gate_bias[layer],
                    swa_q_gain[layer], swa_k_gain[layer], swa_sinks[layer], w_up_swa[layer],
                    w_up_sb[layer], w_out[layer], ffn_norm_gain[layer], peer_w_q[layer],
                    peer_sub_keys[layer], peer_u[layer], peer_v[layer])
    return x2.reshape(batch, seq, d)
```

```python
import functools
import math

import jax
import jax.numpy as jnp
from jax import lax
from jax.experimental import pallas as pl
from jax.experimental.pallas import tpu as pltpu

F32 = jnp.float32
BF16 = jnp.bfloat16

D_MODEL = 1024
HEAD_DIM = 64
CHUNK = 64
RMS_EPS = 1e-6
NEG_INF = -1e30

SWA_Q_HEADS = 8
SWA_GROUP = 4
SWA_WIDTH = 512
SWA_KV_WIDTH = 128
SB_HEADS = 8
SB_WIDTH = 512
GATE_WIDTH = 2 * D_MODEL
IN_WIDTH = SWA_WIDTH + 2 * SWA_KV_WIDTH + 3 * SB_WIDTH + GATE_WIDTH

PEER_HEADS = 8
PEER_KEYS = 128
PEER_HALF = 128
PEER_TOPK = 16
PEER_EXPERTS = PEER_KEYS * PEER_KEYS

LANES = 128
SUBLANES = 8
BLK = 128

INPROJ_TM = 512
MERGE_TM = 256
SELECT_TT = 512
SELECT_LANES = 256
EXPERT_TT = 512
EXPERT_EB = 512
EXPERT_JCHUNK = 32
SB_QROWS = 512
VMEM_LIMIT = 56 * 1024 * 1024

_NT = (((1,), (1,)), ((), ()))


def _cparams(sem):
    return pltpu.CompilerParams(dimension_semantics=sem, vmem_limit_bytes=VMEM_LIMIT)


def _inproj_kernel(x_ref, g_ref, w_ref, qa_ref, ka_ref, va_ref, qb_ref, kb_ref, vb_ref, gl_ref):
    x = x_ref[...]
    ms = jnp.mean(x * x, axis=-1, keepdims=True)
    h = (x * lax.rsqrt(ms + RMS_EPS) * g_ref[...]).astype(BF16)

    def proj(lo, hi):
        return jnp.dot(h, w_ref[:, lo:hi], preferred_element_type=F32)

    o = 0
    qa_ref[...] = proj(o, o + SWA_WIDTH); o += SWA_WIDTH
    ka_ref[...] = proj(o, o + SWA_KV_WIDTH); o += SWA_KV_WIDTH
    va_ref[...] = proj(o, o + SWA_KV_WIDTH).astype(BF16); o += SWA_KV_WIDTH
    qb_ref[...] = (proj(o, o + SB_WIDTH) * (HEAD_DIM ** -0.5)).astype(BF16); o += SB_WIDTH
    kb_ref[...] = proj(o, o + SB_WIDTH).astype(BF16); o += SB_WIDTH
    vb_ref[...] = proj(o, o + SB_WIDTH).astype(BF16); o += SB_WIDTH
    gl_ref[...] = proj(o, o + GATE_WIDTH)


def _inproj(x2, gain, w_in):
    t = x2.shape[0]
    tm = INPROJ_TM
    row = lambda w: pl.BlockSpec((tm, w), lambda i: (i, 0))
    full = lambda a: pl.BlockSpec(a.shape, lambda i: (0,) * a.ndim)
    widths = (SWA_WIDTH, SWA_KV_WIDTH, SWA_KV_WIDTH, SB_WIDTH, SB_WIDTH, SB_WIDTH, GATE_WIDTH)
    dtypes = (F32, F32, BF16, BF16, BF16, BF16, F32)
    return pl.pallas_call(
        _inproj_kernel,
        grid=(t // tm,),
        in_specs=[row(D_MODEL), full(gain), full(w_in)],
        out_specs=[row(w) for w in widths],
        out_shape=[jax.ShapeDtypeStruct((t, w), dt) for w, dt in zip(widths, dtypes)],
        compiler_params=_cparams(("parallel",)),
        name="inproj",
    )(x2, gain, w_in)


def _half_rms(x, lo):
    sq = x * x
    s_lo = jnp.sum(jnp.where(lo, sq, 0.0), axis=-1, keepdims=True)
    s_hi = jnp.sum(jnp.where(lo, 0.0, sq), axis=-1, keepdims=True)
    inv = jnp.where(lo, lax.rsqrt(s_lo / HEAD_DIM + RMS_EPS), lax.rsqrt(s_hi / HEAD_DIM + RMS_EPS))
    return x * inv


def _swa_kernel(sink_ref, q_ref, k_ref, v_ref, qg_ref, kg_ref, o_ref,
                kl0, kh0, kl1, kh1, vl0, vh0, vl1, vh1):
    p = pl.program_id(1)
    lo = lax.broadcasted_iota(jnp.int32, (BLK, LANES), 1) < HEAD_DIM
    kv_scratch = (kl0, kh0, kl1, kh1, vl0, vh0, vl1, vh1)

    @pl.when(p == 0)
    def _():
        for r in kv_scratch:
            r[0:BLK, :] = jnp.zeros((BLK, LANES), BF16)

    row = pl.multiple_of((p + 1) * BLK, BLK)
    kn = _half_rms(k_ref[...], lo) * kg_ref[...]
    kr = pltpu.roll(kn, HEAD_DIM, 1)
    v = v_ref[...].astype(F32)
    vr = pltpu.roll(v, HEAD_DIM, 1)
    kl0[pl.ds(row, BLK), :] = jnp.where(lo, kn, 0.0).astype(BF16)
    kh0[pl.ds(row, BLK), :] = jnp.where(lo, 0.0, kr).astype(BF16)
    kl1[pl.ds(row, BLK), :] = jnp.where(lo, kr, 0.0).astype(BF16)
    kh1[pl.ds(row, BLK), :] = jnp.where(lo, 0.0, kn).astype(BF16)
    vl0[pl.ds(row, BLK), :] = jnp.where(lo, v, 0.0).astype(BF16)
    vh0[pl.ds(row, BLK), :] = jnp.where(lo, 0.0, vr).astype(BF16)
    vl1[pl.ds(row, BLK), :] = jnp.where(lo, vr, 0.0).astype(BF16)
    vh1[pl.ds(row, BLK), :] = jnp.where(lo, 0.0, v).astype(BF16)

    qi = lax.broadcasted_iota(jnp.int32, (BLK, 2 * BLK), 0)
    kj = lax.broadcasted_iota(jnp.int32, (BLK, 2 * BLK), 1)
    dist = jnp.abs(qi + BLK - kj).astype(F32)
    qc = qi // CHUNK
    kc = kj // CHUNK
    vis = (kc >= qc) & (kc <= qc + 2) & ((kj >= BLK) | (p > 0))
    win = pl.ds(pl.multiple_of(p * BLK, BLK), 2 * BLK)

    for m in range(SWA_Q_HEADS // 2):
        g = (2 * m) // SWA_GROUP
        kv = ((kl0, vl0), (kh0, vh0)) if g == 0 else ((kl1, vl1), (kh1, vh1))
        qn = _half_rms(q_ref[:, m * LANES:(m + 1) * LANES], lo) * qg_ref[...]
        qs = (qn * (HEAD_DIM ** -0.5)).astype(BF16)
        acc = jnp.zeros((BLK, LANES), F32)
        for which, (kref, vref) in enumerate(kv):
            h = 2 * m + which
            slope = 2.0 ** (-(h + 1))
            logits = lax.dot_general(qs, kref[win, :], _NT, preferred_element_type=F32)
            logits = jnp.where(vis, logits - slope * dist, NEG_INF)
            sink = sink_ref[h]
            mx = jnp.maximum(jnp.max(logits, axis=-1, keepdims=True), sink)
            pe = jnp.exp(logits - mx)
            den = jnp.sum(pe, axis=-1, keepdims=True) + jnp.exp(sink - mx)
            probs = (pe / den).astype(BF16)
            acc = acc + jnp.dot(probs, vref[win, :], preferred_element_type=F32)
        o_ref[:, m * LANES:(m + 1) * LANES] = acc.astype(BF16)


def _swa(qa, ka, va, qgain, kgain, sinks, batch, seq):
    nb = seq // BLK
    qg2 = jnp.concatenate([qgain, qgain]).reshape(1, LANES)
    kg2 = jnp.concatenate([kgain, kgain]).reshape(1, LANES)
    blk = lambda w: pl.BlockSpec((BLK, w), lambda b, p: (b * nb + p, 0))
    one = pl.BlockSpec((1, LANES), lambda b, p: (0, 0))
    return pl.pallas_call(
        _swa_kernel,
        grid=(batch, nb),
        in_specs=[pl.BlockSpec(memory_space=pltpu.SMEM), blk(SWA_WIDTH), blk(SWA_KV_WIDTH),
                  blk(SWA_KV_WIDTH), one, one],
        out_specs=blk(SWA_WIDTH),
        out_shape=jax.ShapeDtypeStruct((batch * seq, SWA_WIDTH), BF16),
        scratch_shapes=[pltpu.VMEM((seq + BLK, LANES), BF16)] * 8,
        compiler_params=_cparams(("parallel", "arbitrary")),
        name="swa",
    )(sinks, qa, ka, va, qg2, kg2)


def _sb_kernel(q_ref, k_ref, v_ref, tri_ref, o_ref, kk, vv, carry, acc):
    seq = q_ref.shape[0]
    nb = seq // BLK
    per = SB_QROWS // BLK
    lo = lax.broadcasted_iota(jnp.int32, (nb, BLK, LANES), 2) < HEAD_DIM
    k3 = k_ref[...].reshape(nb, BLK, LANES)
    v3 = v_ref[...].reshape(nb, BLK, LANES)
    zero = jnp.zeros_like(k3)
    kk[:, 0:BLK, :] = jnp.where(lo, k3, zero)
    kk[:, BLK:2 * BLK, :] = jnp.where(lo, zero, k3)
    vv[:, 0:BLK, :] = jnp.where(lo, v3, zero)
    vv[:, BLK:2 * BLK, :] = jnp.where(lo, zero, v3)

    def tile(qs, kj, d):
        r0 = 0 if d is None else d * BLK
        m = SB_QROWS - r0
        q = q_ref[pl.ds(pl.multiple_of(qs * SB_QROWS + r0, BLK), m), :]
        z2 = lax.dot_general(q, kk[kj], _NT, preferred_element_type=F32)
        if d is not None:
            before = (lax.broadcasted_iota(jnp.int32, (m, BLK), 1)
                      < lax.broadcasted_iota(jnp.int32, (m, BLK), 0))
        ws = []
        for hh in range(2):
            z = z2[:, hh * BLK:(hh + 1) * BLK]
            log_keep = -(jnp.maximum(z, 0.0) + jnp.log(1.0 + jnp.exp(-jnp.abs(z))))
            log_beta = log_keep + z
            if d is not None:
                log_keep = jnp.where(before, log_keep, 0.0)
            hi = log_keep.astype(BF16)
            lw = (log_keep - hi.astype(F32)).astype(BF16)
            sc = jnp.dot(jnp.concatenate([hi, lw], axis=1), tri_ref[...], preferred_element_type=F32)
            suffix = sc[:, :BLK] + carry[hh, r0:, :]
            carry[hh, r0:, :] = carry[hh, r0:, :] + sc[:, BLK:]
            w = jnp.exp(log_beta + suffix)
            if d is not None:
                w = jnp.where(before, w, 0.0)
            ws.append(w.astype(BF16))
        acc[r0:, :] = acc[r0:, :] + jnp.dot(jnp.concatenate(ws, axis=1), vv[kj],
                                           preferred_element_type=F32)

    def qblock(qs, c):
        carry[...] = jnp.zeros(carry.shape, F32)
        acc[...] = jnp.zeros(acc.shape, F32)
        for d in reversed(range(per)):
            tile(qs, qs * per + d, d)

        def inner(t, c2):
            tile(qs, qs * per - 1 - t, None)
            return c2

        lax.fori_loop(0, qs * per, inner, 0)
        o_ref[pl.ds(pl.multiple_of(qs * SB_QROWS, BLK), SB_QROWS), :] = acc[...].astype(BF16)
        return c

    lax.fori_loop(0, seq // SB_QROWS, qblock, 0)


def _suffix_matrix():
    j = jnp.arange(BLK)[:, None]
    s = jnp.arange(BLK)[None, :]
    strict = (j > s).astype(BF16)
    half = jnp.concatenate([strict, jnp.ones((BLK, BLK), BF16)], axis=1)
    return jnp.concatenate([half, half], axis=0)


def _stickbreak(qb, kb, vb, batch, seq):
    npair = SB_HEADS // 2
    blk = pl.BlockSpec((seq, LANES), lambda b, m: (b, m))
    tri = _suffix_matrix()
    return pl.pallas_call(
        _sb_kernel,
        grid=(batch, npair),
        in_specs=[blk, blk, blk, pl.BlockSpec(tri.shape, lambda b, m: (0, 0))],
        out_specs=blk,
        out_shape=jax.ShapeDtypeStruct((batch * seq, SB_WIDTH), BF16),
        scratch_shapes=[pltpu.VMEM((seq // BLK, 2 * BLK, LANES), BF16)] * 2
        + [pltpu.VMEM((2, SB_QROWS, BLK), F32), pltpu.VMEM((SB_QROWS, LANES), F32)],
        compiler_params=_cparams(("parallel", "parallel")),
        name="stickbreak",
    )(qb, kb, vb, tri)


def _merge_kernel(ya_ref, yb_ref, gl_ref, bias_ref, x_ref, wa_ref, wb_ref, wo_ref, g2_ref, wq_ref,
                  x1_ref, h2_ref, qt_ref):
    ua = jnp.dot(ya_ref[...], wa_ref[...], preferred_element_type=F32)
    ub = jnp.dot(yb_ref[...], wb_ref[...], preferred_element_type=F32)
    gates = jax.nn.sigmoid(gl_ref[...] + bias_ref[...])
    merged = gates[:, :D_MODEL] * ua + gates[:, D_MODEL:] * ub
    x1 = x_ref[...] + jnp.dot(merged.astype(BF16), wo_ref[...], preferred_element_type=F32)
    x1_ref[...] = x1
    ms = jnp.mean(x1 * x1, axis=-1, keepdims=True)
    h2 = (x1 * lax.rsqrt(ms + RMS_EPS) * g2_ref[...]).astype(BF16)
    h2_ref[...] = h2
    qt_ref[...] = lax.dot_general(wq_ref[...], h2, _NT, preferred_element_type=F32).astype(BF16)


def _merge(ya, yb, gl, bias, x2, wa, wb, wo, g2, wq_t):
    t = x2.shape[0]
    tm = MERGE_TM
    row = lambda w: pl.BlockSpec((tm, w), lambda i: (i, 0))
    full = lambda a: pl.BlockSpec(a.shape, lambda i: (0,) * a.ndim)
    nq = wq_t.shape[0]
    return pl.pallas_call(
        _merge_kernel,
        grid=(t // tm,),
        in_specs=[row(SWA_WIDTH), row(SB_WIDTH), row(GATE_WIDTH), full(bias), row(D_MODEL),
                  full(wa), full(wb), full(wo), full(g2), full(wq_t)],
        out_specs=[row(D_MODEL), row(D_MODEL), pl.BlockSpec((nq, tm), lambda i: (0, i))],
        out_shape=[jax.ShapeDtypeStruct((t, D_MODEL), F32),
                   jax.ShapeDtypeStruct((t, D_MODEL), BF16),
                   jax.ShapeDtypeStruct((nq, t), BF16)],
        compiler_params=_cparams(("parallel",)),
        name="merge",
    )(ya, yb, gl, bias, x2, wa, wb, wo, g2, wq_t)


_CAND_GROUPS = ((0, 16), (1, 8), (2, 5), (3, 4), (4, 3), (5, 2), (6, 2), (7, 2))
_CAND_ROWS = 16 + 7 * SUBLANES + SUBLANES


def _top16(s, want_rank):
    n, tt = s.shape
    pos = lax.broadcasted_iota(jnp.int32, (n, tt), 0).astype(F32)
    rank = jnp.full((n, tt), float(PEER_TOPK), F32) if want_rank else None
    vals, idxs = [], []
    for r in range(PEER_TOPK):
        m = jnp.max(s, axis=0, keepdims=True)
        idx = jnp.min(jnp.where(s == m, pos, float(n)), axis=0, keepdims=True)
        hit = pos == idx
        if want_rank:
            rank = jnp.where(hit, float(r), rank)
        s = jnp.where(hit, -jnp.inf, s)
        vals.append(m)
        idxs.append(idx)
    return vals, idxs, rank


def _select_lanes(s0, s1):
    tt = s0.shape[1]
    v0, i0, _ = _top16(s0, False)
    v1, _, rank1 = _top16(s1, True)

    row8 = lax.broadcasted_iota(jnp.int32, (SUBLANES, tt), 0)
    stack8 = lambda rows: functools.reduce(
        lambda acc, kv: jnp.where(row8 == kv[0], kv[1], acc), enumerate(rows), jnp.zeros((SUBLANES, tt), F32))
    v1_lo, v1_hi = stack8(v1[:8]), stack8(v1[8:])
    v0_hi = stack8(v0[8:])

    pieces = []
    for r0, n in _CAND_GROUPS:
        lo_piece = v0[r0] + v1_lo
        pieces.append(lo_piece if n >= SUBLANES else jnp.where(row8 < n, lo_piece, -jnp.inf))
        if n > SUBLANES:
            pieces.append(v0[r0] + v1_hi)
    pieces.append(v0_hi + v1[0])
    cand = jnp.concatenate(pieces, axis=0)

    pos = lax.broadcasted_iota(jnp.int32, cand.shape, 0).astype(F32)
    chosen = jnp.zeros(cand.shape, F32)
    c = cand
    for _ in range(PEER_TOPK):
        m = jnp.max(c, axis=0, keepdims=True)
        idx = jnp.min(jnp.where(c == m, pos, float(_CAND_ROWS)), axis=0, keepdims=True)
        hit = pos == idx
        chosen = jnp.where(hit, 1.0, chosen)
        c = jnp.where(hit, -jnp.inf, c)

    best = v0[0] + v1[0]
    z = jnp.sum(jnp.where(chosen > 0.0, jnp.exp(cand - best), 0.0), axis=0, keepdims=True)

    counts = []
    off = 0
    for r0, n in _CAND_GROUPS:
        rows = SUBLANES * ((n + SUBLANES - 1) // SUBLANES)
        counts.append(jnp.sum(chosen[off:off + rows], axis=0, keepdims=True))
        off += rows
    for r in range(SUBLANES):
        counts.append(chosen[off + r:off + r + 1])

    key = lax.broadcasted_iota(jnp.int32, s0.shape, 0).astype(F32)
    c0 = jnp.zeros(s0.shape, F32)
    for r in range(PEER_TOPK):
        c0 = jnp.where(key == i0[r], counts[r], c0)

    return jnp.exp(s0 - v0[0]) * (1.0 / z), c0, jnp.exp(s1 - v1[0]), rank1


def _select_kernel(q_ref, sk_ref, a_ref, c_ref, b_ref, r_ref, s_ref):
    s_ref[0] = jnp.dot(sk_ref[0, 0], q_ref[0:PEER_HALF, :], preferred_element_type=F32)
    s_ref[1] = jnp.dot(sk_ref[0, 1], q_ref[PEER_HALF:2 * PEER_HALF, :], preferred_element_type=F32)

    def lane_tile(lt, carry):
        ls = pl.ds(pl.multiple_of(lt * SELECT_LANES, SELECT_LANES), SELECT_LANES)
        a, c0, b, rank1 = _select_lanes(s_ref[0, :, ls], s_ref[1, :, ls])
        a_ref[0, :, ls] = a
        c_ref[0, :, ls] = c0
        b_ref[0, :, ls] = b
        r_ref[0, :, ls] = rank1
        return carry

    lax.fori_loop(0, q_ref.shape[1] // SELECT_LANES, lane_tile, 0)


def _peer_select(qt, sub_keys):
    t = qt.shape[1]
    tt = SELECT_TT
    out = pl.BlockSpec((1, PEER_KEYS, tt), lambda i, h: (h, 0, i))
    shp = lambda dt: jax.ShapeDtypeStruct((PEER_HEADS, PEER_KEYS, t), dt)
    return pl.pallas_call(
        _select_kernel,
        grid=(t // tt, PEER_HEADS),
        in_specs=[pl.BlockSpec((2 * PEER_HALF, tt), lambda i, h: (h, i)),
                  pl.BlockSpec((1, 2, PEER_KEYS, PEER_HALF), lambda i, h: (h, 0, 0, 0))],
        out_specs=[out] * 4,
        out_shape=[shp(F32)] * 4,
        scratch_shapes=[pltpu.VMEM((2, PEER_KEYS, tt), F32)],
        compiler_params=_cparams(("parallel", "parallel")),
        name="peer_select",
    )(qt, sub_keys)


def _experts_kernel(zero_ref, h_ref, a_ref, c_ref, b_ref, r_ref, u0_ref, un_ref, vt_ref, x1_ref, o_ref,
                    acc_ref, g_ref, abc_ref, stage_ref):
    e = pl.program_id(1)
    last = pl.num_programs(1) - 1
    nsub = un_ref.shape[0] // PEER_KEYS
    tt = h_ref.shape[0]
    half = tt // 2
    lt_per_half = half // LANES
    nv = EXPERT_JCHUNK // SUBLANES
    plane = [zero_ref[k] + k for k in range(4)]

    def gate_weights(lt):
        hf, ll = divmod(lt, lt_per_half)
        ls = slice(lt * LANES, (lt + 1) * LANES)
        for jc in range(PEER_KEYS // EXPERT_JCHUNK):
            j0 = jc * EXPERT_JCHUNK
            w = [[jnp.zeros((SUBLANES, LANES), F32)] * nv for _ in range(nsub)]
            for h in range(PEER_HEADS):
                bs = [b_ref[h, j0 + v * SUBLANES:j0 + (v + 1) * SUBLANES, ls] for v in range(nv)]
                rs = [r_ref[h, j0 + v * SUBLANES:j0 + (v + 1) * SUBLANES, ls] for v in range(nv)]
                for ii in range(nsub):
                    a8 = abc_ref[0, ii, h, :, ls]
                    c8 = abc_ref[1, ii, h, :, ls]
                    for v in range(nv):
                        w[ii][v] = w[ii][v] + jnp.where(rs[v] < c8, a8 * bs[v], 0.0)
            for ii in range(nsub):
                r0 = ii * PEER_KEYS + j0
                stage_ref[plane[2 + hf], r0:r0 + EXPERT_JCHUNK, ll * LANES:(ll + 1) * LANES] = (
                    jnp.concatenate(w[ii], axis=0))

    def stage_block(u_ref, blk):
        for ii in range(nsub):
            i = blk * nsub + ii
            for h in range(PEER_HEADS):
                abc_ref[0, ii, h] = jnp.broadcast_to(a_ref[h, pl.ds(i, 1), :], (SUBLANES, tt))
                abc_ref[1, ii, h] = jnp.broadcast_to(c_ref[h, pl.ds(i, 1), :], (SUBLANES, tt))
        for lt in range(2 * lt_per_half):
            gate_weights(lt)
        for hf in range(2):
            hs = slice(hf * half, (hf + 1) * half)
            stage_ref[plane[hf]] = lax.dot_general(u_ref[...], h_ref[hs, :], _NT,
                                                   preferred_element_type=F32)

    @pl.when(e == 0)
    def _():
        acc_ref[...] = jnp.zeros(acc_ref.shape, F32)
        stage_block(u0_ref, 0)

    for lt in range(2 * lt_per_half):
        hf, ll = divmod(lt, lt_per_half)
        x = stage_ref[plane[hf], :, ll * LANES:(ll + 1) * LANES]
        gelu = 0.5 * x * (1.0 + lax.erf(x * math.sqrt(0.5)))
        w = stage_ref[plane[2 + hf], :, ll * LANES:(ll + 1) * LANES]
        g_ref[:, lt * LANES:(lt + 1) * LANES] = (w * gelu).astype(BF16)
    for hf in range(2):
        hs = slice(hf * half, (hf + 1) * half)
        acc_ref[:, hs] += jnp.dot(vt_ref[...], g_ref[:, hs], preferred_element_type=F32)

    stage_block(un_ref, jnp.minimum(e + 1, last))

    @pl.when(e == last)
    def _():
        o_ref[...] = x1_ref[...] + acc_ref[...].T


def _peer_experts(h2, a, c0, b, r1, u, vt, x1):
    t = h2.shape[0]
    tt, eb = EXPERT_TT, EXPERT_EB
    nblk = PEER_EXPERTS // eb
    sel = pl.BlockSpec((PEER_HEADS, PEER_KEYS, tt), lambda i, e: (0, 0, i))
    row = pl.BlockSpec((tt, D_MODEL), lambda i, e: (i, 0))
    return pl.pallas_call(
        _experts_kernel,
        grid=(t // tt, nblk),
        in_specs=[pl.BlockSpec(memory_space=pltpu.SMEM), row, sel, sel, sel, sel,
                  pl.BlockSpec((eb, D_MODEL), lambda i, e: (0, 0)),
                  pl.BlockSpec((eb, D_MODEL), lambda i, e: (jnp.minimum(e + 1, nblk - 1), 0)),
                  pl.BlockSpec((D_MODEL, eb), lambda i, e: (0, e)),
                  row],
        out_specs=row,
        out_shape=jax.ShapeDtypeStruct((t, D_MODEL), F32),
        scratch_shapes=[pltpu.VMEM((D_MODEL, tt), F32), pltpu.VMEM((eb, tt), BF16),
                        pltpu.VMEM((2, eb // PEER_KEYS, PEER_HEADS, SUBLANES, tt), F32),
                        pltpu.VMEM((4, eb, tt // 2), F32)],
        compiler_params=_cparams(("parallel", "arbitrary")),
        name="peer_experts",
    )(jnp.zeros((4,), jnp.int32), h2, a, c0, b, r1, u, u, vt, x1)


def _layer(x2, batch, seq, mix_gain, w_in, gate_bias, q_gain, k_gain, sinks,
           w_up_swa, w_up_sb, w_out, ffn_gain, w_q, sub_keys, u, v):
    qa, ka, va, qb, kb, vb, gl = _inproj(x2, mix_gain.reshape(1, -1), w_in.astype(BF16))
    ya = _swa(qa, ka, va, q_gain, k_gain, sinks, batch, seq)
    yb = _stickbreak(qb, kb, vb, batch, seq)
    x1, h2, qt = _merge(ya, yb, gl, gate_bias.reshape(1, -1), x2,
                        w_up_swa.astype(BF16), w_up_sb.astype(BF16), w_out.astype(BF16),
                        ffn_gain.reshape(1, -1), w_q.T.astype(BF16))
    a, c0, b, r1 = _peer_select(qt, sub_keys.astype(BF16))
    return _peer_experts(h2, a, c0, b, r1, u.astype(BF16), v.T.astype(BF16), x1)


def kernel(x, mix_norm_gain, w_in, gate_bias, swa_q_gain, swa_k_gain, swa_sinks, w_up_swa, w_up_sb,
           w_out, ffn_norm_gain, peer_w_q, peer_sub_keys, peer_u, peer_v):
    batch, seq, d = x.shape
    x2 = x.reshape(batch * seq, d)
    for layer in range(mix_norm_gain.shape[0]):
        x2 = _layer(x2, batch, seq, mix_norm_gain[layer], w_in[layer], gate_bias[layer],
                    swa_q_gain[layer], swa_k_gain[layer], swa_sinks[layer], w_up_swa[layer],
                    w_up_sb[layer], w_out[layer], ffn_norm_gain[layer], peer_w_q[layer],
                    peer_sub_keys[layer], peer_u[layer], peer_v[layer])
    return x2.reshape(batch, seq, d)
```

```python
import functools
import math

import jax
import jax.numpy as jnp
from jax import lax
from jax.experimental import pallas as pl
from jax.experimental.pallas import tpu as pltpu

F32 = jnp.float32
BF16 = jnp.bfloat16

D_MODEL = 1024
HEAD_DIM = 64
CHUNK = 64
RMS_EPS = 1e-6
NEG_INF = -1e30

SWA_Q_HEADS = 8
SWA_GROUP = 4
SWA_WIDTH = 512
SWA_KV_WIDTH = 128
SB_HEADS = 8
SB_WIDTH = 512
GATE_WIDTH = 2 * D_MODEL
IN_WIDTH = SWA_WIDTH + 2 * SWA_KV_WIDTH + 3 * SB_WIDTH + GATE_WIDTH

PEER_HEADS = 8
PEER_KEYS = 128
PEER_HALF = 128
PEER_TOPK = 16
PEER_EXPERTS = PEER_KEYS * PEER_KEYS

LANES = 128
SUBLANES = 8
BLK = 128

INPROJ_TM = 512
MERGE_TM = 256
SELECT_TT = 512
SELECT_LANES = 256
EXPERT_TT = 512
EXPERT_EB = 512
EXPERT_JCHUNK = 32
SB_QROWS = 512
VMEM_LIMIT = 56 * 1024 * 1024

_NT = (((1,), (1,)), ((), ()))


def _cparams(sem):
    return pltpu.CompilerParams(dimension_semantics=sem, vmem_limit_bytes=VMEM_LIMIT)


def _inproj_kernel(x_ref, g_ref, w_ref, qa_ref, ka_ref, va_ref, qb_ref, kb_ref, vb_ref, gl_ref):
    x = x_ref[...]
    ms = jnp.mean(x * x, axis=-1, keepdims=True)
    h = (x * lax.rsqrt(ms + RMS_EPS) * g_ref[...]).astype(BF16)

    def proj(lo, hi):
        return jnp.dot(h, w_ref[:, lo:hi], preferred_element_type=F32)

    o = 0
    qa_ref[...] = proj(o, o + SWA_WIDTH); o += SWA_WIDTH
    ka_ref[...] = proj(o, o + SWA_KV_WIDTH); o += SWA_KV_WIDTH
    va_ref[...] = proj(o, o + SWA_KV_WIDTH).astype(BF16); o += SWA_KV_WIDTH
    qb_ref[...] = (proj(o, o + SB_WIDTH) * (HEAD_DIM ** -0.5)).astype(BF16); o += SB_WIDTH
    kb_ref[...] = proj(o, o + SB_WIDTH).astype(BF16); o += SB_WIDTH
    vb_ref[...] = proj(o, o + SB_WIDTH).astype(BF16); o += SB_WIDTH
    gl_ref[...] = proj(o, o + GATE_WIDTH)


def _inproj(x2, gain, w_in):
    t = x2.shape[0]
    tm = INPROJ_TM
    row = lambda w: pl.BlockSpec((tm, w), lambda i: (i, 0))
    full = lambda a: pl.BlockSpec(a.shape, lambda i: (0,) * a.ndim)
    widths = (SWA_WIDTH, SWA_KV_WIDTH, SWA_KV_WIDTH, SB_WIDTH, SB_WIDTH, SB_WIDTH, GATE_WIDTH)
    dtypes = (F32, F32, BF16, BF16, BF16, BF16, F32)
    return pl.pallas_call(
        _inproj_kernel,
        grid=(t // tm,),
        in_specs=[row(D_MODEL), full(gain), full(w_in)],
        out_specs=[row(w) for w in widths],
        out_shape=[jax.ShapeDtypeStruct((t, w), dt) for w, dt in zip(widths, dtypes)],
        compiler_params=_cparams(("parallel",)),
        name="inproj",
    )(x2, gain, w_in)


def _half_rms(x, lo):
    sq = x * x
    s_lo = jnp.sum(jnp.where(lo, sq, 0.0), axis=-1, keepdims=True)
    s_hi = jnp.sum(jnp.where(lo, 0.0, sq), axis=-1, keepdims=True)
    inv = jnp.where(lo, lax.rsqrt(s_lo / HEAD_DIM + RMS_EPS), lax.rsqrt(s_hi / HEAD_DIM + RMS_EPS))
    return x * inv


def _swa_kernel(sink_ref, q_ref, k_ref, v_ref, qg_ref, kg_ref, o_ref,
                kl0, kh0, kl1, kh1, vl0, vh0, vl1, vh1):
    p = pl.program_id(1)
    lo = lax.broadcasted_iota(jnp.int32, (BLK, LANES), 1) < HEAD_DIM
    kv_scratch = (kl0, kh0, kl1, kh1, vl0, vh0, vl1, vh1)

    @pl.when(p == 0)
    def _():
        for r in kv_scratch:
            r[0:BLK, :] = jnp.zeros((BLK, LANES), BF16)

    row = pl.multiple_of((p + 1) * BLK, BLK)
    kn = _half_rms(k_ref[...], lo) * kg_ref[...]
    kr = pltpu.roll(kn, HEAD_DIM, 1)
    v = v_ref[...].astype(F32)
    vr = pltpu.roll(v, HEAD_DIM, 1)
    kl0[pl.ds(row, BLK), :] = jnp.where(lo, kn, 0.0).astype(BF16)
    kh0[pl.ds(row, BLK), :] = jnp.where(lo, 0.0, kr).astype(BF16)
    kl1[pl.ds(row, BLK), :] = jnp.where(lo, kr, 0.0).astype(BF16)
    kh1[pl.ds(row, BLK), :] = jnp.where(lo, 0.0, kn).astype(BF16)
    vl0[pl.ds(row, BLK), :] = jnp.where(lo, v, 0.0).astype(BF16)
    vh0[pl.ds(row, BLK), :] = jnp.where(lo, 0.0, vr).astype(BF16)
    vl1[pl.ds(row, BLK), :] = jnp.where(lo, vr, 0.0).astype(BF16)
    vh1[pl.ds(row, BLK), :] = jnp.where(lo, 0.0, v).astype(BF16)

    qi = lax.broadcasted_iota(jnp.int32, (BLK, 2 * BLK), 0)
    kj = lax.broadcasted_iota(jnp.int32, (BLK, 2 * BLK), 1)
    dist = jnp.abs(qi + BLK - kj).astype(F32)
    qc = qi // CHUNK
    kc = kj // CHUNK
    vis = (kc >= qc) & (kc <= qc + 2) & ((kj >= BLK) | (p > 0))
    win = pl.ds(pl.multiple_of(p * BLK, BLK), 2 * BLK)

    for m in range(SWA_Q_HEADS // 2):
        g = (2 * m) // SWA_GROUP
        kv = ((kl0, vl0), (kh0, vh0)) if g == 0 else ((kl1, vl1), (kh1, vh1))
        qn = _half_rms(q_ref[:, m * LANES:(m + 1) * LANES], lo) * qg_ref[...]
        qs = (qn * (HEAD_DIM ** -0.5)).astype(BF16)
        acc = jnp.zeros((BLK, LANES), F32)
        for which, (kref, vref) in enumerate(kv):
            h = 2 * m + which
            slope = 2.0 ** (-(h + 1))
            logits = lax.dot_general(qs, kref[win, :], _NT, preferred_element_type=F32)
            logits = jnp.where(vis, logits - slope * dist, NEG_INF)
            sink = sink_ref[h]
            mx = jnp.maximum(jnp.max(logits, axis=-1, keepdims=True), sink)
            pe = jnp.exp(logits - mx)
            den = jnp.sum(pe, axis=-1, keepdims=True) + jnp.exp(sink - mx)
            probs = (pe / den).astype(BF16)
            acc = acc + jnp.dot(probs, vref[win, :], preferred_element_type=F32)
        o_ref[:, m * LANES:(m + 1) * LANES] = acc.astype(BF16)


def _swa(qa, ka, va, qgain, kgain, sinks, batch, seq):
    nb = seq // BLK
    qg2 = jnp.concatenate([qgain, qgain]).reshape(1, LANES)
    kg2 = jnp.concatenate([kgain, kgain]).reshape(1, LANES)
    blk = lambda w: pl.BlockSpec((BLK, w), lambda b, p: (b * nb + p, 0))
    one = pl.BlockSpec((1, LANES), lambda b, p: (0, 0))
    return pl.pallas_call(
        _swa_kernel,
        grid=(batch, nb),
        in_specs=[pl.BlockSpec(memory_space=pltpu.SMEM), blk(SWA_WIDTH), blk(SWA_KV_WIDTH),
                  blk(SWA_KV_WIDTH), one, one],
        out_specs=blk(SWA_WIDTH),
        out_shape=jax.ShapeDtypeStruct((batch * seq, SWA_WIDTH), BF16),
        scratch_shapes=[pltpu.VMEM((seq + BLK, LANES), BF16)] * 8,
        compiler_params=_cparams(("parallel", "arbitrary")),
        name="swa",
    )(sinks, qa, ka, va, qg2, kg2)


def _sb_kernel(q_ref, k_ref, v_ref, tri_ref, o_ref, kk, vv, carry, acc):
    seq = q_ref.shape[0]
    nb = seq // BLK
    per = SB_QROWS // BLK
    lo = lax.broadcasted_iota(jnp.int32, (nb, BLK, LANES), 2) < HEAD_DIM
    k3 = k_ref[...].reshape(nb, BLK, LANES)
    v3 = v_ref[...].reshape(nb, BLK, LANES)
    zero = jnp.zeros_like(k3)
    kk[:, 0:BLK, :] = jnp.where(lo, k3, zero)
    kk[:, BLK:2 * BLK, :] = jnp.where(lo, zero, k3)
    vv[:, 0:BLK, :] = jnp.where(lo, v3, zero)
    vv[:, BLK:2 * BLK, :] = jnp.where(lo, zero, v3)

    def tile(qs, kj, d):
        r0 = 0 if d is None else d * BLK
        m = SB_QROWS - r0
        q = q_ref[pl.ds(pl.multiple_of(qs * SB_QROWS + r0, BLK), m), :]
        z2 = lax.dot_general(q, kk[kj], _NT, preferred_element_type=F32)
        if d is not None:
            before = (lax.broadcasted_iota(jnp.int32, (m, BLK), 1)
                      < lax.broadcasted_iota(jnp.int32, (m, BLK), 0))
        ws = []
        for hh in range(2):
            z = z2[:, hh * BLK:(hh + 1) * BLK]
            log_keep = -(jnp.maximum(z, 0.0) + jnp.log(1.0 + jnp.exp(-jnp.abs(z))))
            log_beta = log_keep + z
            if d is not None:
                log_keep = jnp.where(before, log_keep, 0.0)
            hi = log_keep.astype(BF16)
            lw = (log_keep - hi.astype(F32)).astype(BF16)
            sc = jnp.dot(jnp.concatenate([hi, lw], axis=1), tri_ref[...], preferred_element_type=F32)
            suffix = sc[:, :BLK] + carry[hh, r0:, :]
            carry[hh, r0:, :] = carry[hh, r0:, :] + sc[:, BLK:]
            w = jnp.exp(log_beta + suffix)
            if d is not None:
                w = jnp.where(before, w, 0.0)
            ws.append(w.astype(BF16))
        acc[r0:, :] = acc[r0:, :] + jnp.dot(jnp.concatenate(ws, axis=1), vv[kj],
                                           preferred_element_type=F32)

    def qblock(qs, c):
        carry[...] = jnp.zeros(carry.shape, F32)
        acc[...] = jnp.zeros(acc.shape, F32)
        for d in reversed(range(per)):
            tile(qs, qs * per + d, d)

        def inner(t, c2):
            tile(qs, qs * per - 1 - t, None)
            return c2

        lax.fori_loop(0, qs * per, inner, 0)
        o_ref[pl.ds(pl.multiple_of(qs * SB_QROWS, BLK), SB_QROWS), :] = acc[...].astype(BF16)
        return c

    lax.fori_loop(0, seq // SB_QROWS, qblock, 0)


def _suffix_matrix():
    j = jnp.arange(BLK)[:, None]
    s = jnp.arange(BLK)[None, :]
    strict = (j > s).astype(BF16)
    half = jnp.concatenate([strict, jnp.ones((BLK, BLK), BF16)], axis=1)
    return jnp.concatenate([half, half], axis=0)


def _stickbreak(qb, kb, vb, batch, seq):
    npair = SB_HEADS // 2
    blk = pl.BlockSpec((seq, LANES), lambda b, m: (b, m))
    tri = _suffix_matrix()
    return pl.pallas_call(
        _sb_kernel,
        grid=(batch, npair),
        in_specs=[blk, blk, blk, pl.BlockSpec(tri.shape, lambda b, m: (0, 0))],
        out_specs=blk,
        out_shape=jax.ShapeDtypeStruct((batch * seq, SB_WIDTH), BF16),
        scratch_shapes=[pltpu.VMEM((seq // BLK, 2 * BLK, LANES), BF16)] * 2
        + [pltpu.VMEM((2, SB_QROWS, BLK), F32), pltpu.VMEM((SB_QROWS, LANES), F32)],
        compiler_params=_cparams(("parallel", "parallel")),
        name="stickbreak",
    )(qb, kb, vb, tri)


def _merge_kernel(ya_ref, yb_ref, gl_ref, bias_ref, x_ref, wa_ref, wb_ref, wo_ref, g2_ref, wq_ref,
                  x1_ref, h2_ref, qt_ref):
    ua = jnp.dot(ya_ref[...], wa_ref[...], preferred_element_type=F32)
    ub = jnp.dot(yb_ref[...], wb_ref[...], preferred_element_type=F32)
    gates = jax.nn.sigmoid(gl_ref[...] + bias_ref[...])
    merged = gates[:, :D_MODEL] * ua + gates[:, D_MODEL:] * ub
    x1 = x_ref[...] + jnp.dot(merged.astype(BF16), wo_ref[...], preferred_element_type=F32)
    x1_ref[...] = x1
    ms = jnp.mean(x1 * x1, axis=-1, keepdims=True)
    h2t = (x1 * lax.rsqrt(ms + RMS_EPS) * g2_ref[...]).T.astype(BF16)
    h2_ref[...] = h2t
    qt_ref[...] = jnp.dot(wq_ref[...], h2t, preferred_element_type=F32).astype(BF16)


def _merge(ya, yb, gl, bias, x2, wa, wb, wo, g2, wq_t):
    t = x2.shape[0]
    tm = MERGE_TM
    row = lambda w: pl.BlockSpec((tm, w), lambda i: (i, 0))
    full = lambda a: pl.BlockSpec(a.shape, lambda i: (0,) * a.ndim)
    nq = wq_t.shape[0]
    return pl.pallas_call(
        _merge_kernel,
        grid=(t // tm,),
        in_specs=[row(SWA_WIDTH), row(SB_WIDTH), row(GATE_WIDTH), full(bias), row(D_MODEL),
                  full(wa), full(wb), full(wo), full(g2), full(wq_t)],
        out_specs=[row(D_MODEL), pl.BlockSpec((D_MODEL, tm), lambda i: (0, i)),
                   pl.BlockSpec((nq, tm), lambda i: (0, i))],
        out_shape=[jax.ShapeDtypeStruct((t, D_MODEL), F32),
                   jax.ShapeDtypeStruct((D_MODEL, t), BF16),
                   jax.ShapeDtypeStruct((nq, t), BF16)],
        compiler_params=_cparams(("parallel",)),
        name="merge",
    )(ya, yb, gl, bias, x2, wa, wb, wo, g2, wq_t)


_CAND_GROUPS = ((0, 16), (1, 8), (2, 5), (3, 4), (4, 3), (5, 2), (6, 2), (7, 2))
_CAND_ROWS = 16 + 7 * SUBLANES + SUBLANES


def _top16(s, want_rank):
    n, tt = s.shape
    pos = lax.broadcasted_iota(jnp.int32, (n, tt), 0).astype(F32)
    rank = jnp.full((n, tt), float(PEER_TOPK), F32) if want_rank else None
    vals, idxs = [], []
    for r in range(PEER_TOPK):
        m = jnp.max(s, axis=0, keepdims=True)
        idx = jnp.min(jnp.where(s == m, pos, float(n)), axis=0, keepdims=True)
        hit = pos == idx
        if want_rank:
            rank = jnp.where(hit, float(r), rank)
        s = jnp.where(hit, -jnp.inf, s)
        vals.append(m)
        idxs.append(idx)
    return vals, idxs, rank


def _select_lanes(s0, s1):
    tt = s0.shape[1]
    v0, i0, _ = _top16(s0, False)
    v1, _, rank1 = _top16(s1, True)

    row8 = lax.broadcasted_iota(jnp.int32, (SUBLANES, tt), 0)
    stack8 = lambda rows: functools.reduce(
        lambda acc, kv: jnp.where(row8 == kv[0], kv[1], acc), enumerate(rows), jnp.zeros((SUBLANES, tt), F32))
    v1_lo, v1_hi = stack8(v1[:8]), stack8(v1[8:])
    v0_hi = stack8(v0[8:])

    pieces = []
    for r0, n in _CAND_GROUPS:
        lo_piece = v0[r0] + v1_lo
        pieces.append(lo_piece if n >= SUBLANES else jnp.where(row8 < n, lo_piece, -jnp.inf))
        if n > SUBLANES:
            pieces.append(v0[r0] + v1_hi)
    pieces.append(v0_hi + v1[0])
    cand = jnp.concatenate(pieces, axis=0)

    pos = lax.broadcasted_iota(jnp.int32, cand.shape, 0).astype(F32)
    chosen = jnp.zeros(cand.shape, F32)
    c = cand
    for _ in range(PEER_TOPK):
        m = jnp.max(c, axis=0, keepdims=True)
        idx = jnp.min(jnp.where(c == m, pos, float(_CAND_ROWS)), axis=0, keepdims=True)
        hit = pos == idx
        chosen = jnp.where(hit, 1.0, chosen)
        c = jnp.where(hit, -jnp.inf, c)

    best = v0[0] + v1[0]
    z = jnp.sum(jnp.where(chosen > 0.0, jnp.exp(cand - best), 0.0), axis=0, keepdims=True)

    counts = []
    off = 0
    for r0, n in _CAND_GROUPS:
        rows = SUBLANES * ((n + SUBLANES - 1) // SUBLANES)
        counts.append(jnp.sum(chosen[off:off + rows], axis=0, keepdims=True))
        off += rows
    for r in range(SUBLANES):
        counts.append(chosen[off + r:off + r + 1])

    key = lax.broadcasted_iota(jnp.int32, s0.shape, 0).astype(F32)
    c0 = jnp.zeros(s0.shape, F32)
    for r in range(PEER_TOPK):
        c0 = jnp.where(key == i0[r], counts[r], c0)

    return jnp.exp(s0 - v0[0]) * (0.5 / z), c0, jnp.exp(s1 - v1[0]), rank1


def _select_kernel(q_ref, sk_ref, a_ref, c_ref, b_ref, r_ref, s_ref):
    s_ref[0] = jnp.dot(sk_ref[0, 0], q_ref[0:PEER_HALF, :], preferred_element_type=F32)
    s_ref[1] = jnp.dot(sk_ref[0, 1], q_ref[PEER_HALF:2 * PEER_HALF, :], preferred_element_type=F32)

    def lane_tile(lt, carry):
        ls = pl.ds(pl.multiple_of(lt * SELECT_LANES, SELECT_LANES), SELECT_LANES)
        a, c0, b, rank1 = _select_lanes(s_ref[0, :, ls], s_ref[1, :, ls])
        a_ref[0, :, ls] = a
        c_ref[0, :, ls] = c0
        b_ref[0, :, ls] = b
        r_ref[0, :, ls] = rank1
        return carry

    lax.fori_loop(0, q_ref.shape[1] // SELECT_LANES, lane_tile, 0)


def _peer_select(qt, sub_keys):
    t = qt.shape[1]
    tt = SELECT_TT
    out = pl.BlockSpec((1, PEER_KEYS, tt), lambda i, h: (h, 0, i))
    shp = lambda dt: jax.ShapeDtypeStruct((PEER_HEADS, PEER_KEYS, t), dt)
    return pl.pallas_call(
        _select_kernel,
        grid=(t // tt, PEER_HEADS),
        in_specs=[pl.BlockSpec((2 * PEER_HALF, tt), lambda i, h: (h, i)),
                  pl.BlockSpec((1, 2, PEER_KEYS, PEER_HALF), lambda i, h: (h, 0, 0, 0))],
        out_specs=[out] * 4,
        out_shape=[shp(F32)] * 4,
        scratch_shapes=[pltpu.VMEM((2, PEER_KEYS, tt), F32)],
        compiler_params=_cparams(("parallel", "parallel")),
        name="peer_select",
    )(qt, sub_keys)


def _experts_kernel(zero_ref, h_ref, a_ref, c_ref, b_ref, r_ref, u0_ref, un_ref, vt_ref, x1_ref, o_ref,
                    acc_ref, g_ref, abc_ref, stage_ref):
    e = pl.program_id(1)
    last = pl.num_programs(1) - 1
    nsub = un_ref.shape[0] // PEER_KEYS
    tt = h_ref.shape[1]
    half = tt // 2
    lt_per_half = half // LANES
    nv = EXPERT_JCHUNK // SUBLANES
    plane = [zero_ref[k] + k for k in range(4)]

    def gate_weights(lt):
        hf, ll = divmod(lt, lt_per_half)
        ls = slice(lt * LANES, (lt + 1) * LANES)
        for jc in range(PEER_KEYS // EXPERT_JCHUNK):
            j0 = jc * EXPERT_JCHUNK
            w = [[jnp.zeros((SUBLANES, LANES), F32)] * nv for _ in range(nsub)]
            for h in range(PEER_HEADS):
                bs = [b_ref[h, j0 + v * SUBLANES:j0 + (v + 1) * SUBLANES, ls] for v in range(nv)]
                rs = [r_ref[h, j0 + v * SUBLANES:j0 + (v + 1) * SUBLANES, ls] for v in range(nv)]
                for ii in range(nsub):
                    a8 = abc_ref[0, ii, h, :, ls]
                    c8 = abc_ref[1, ii, h, :, ls]
                    for v in range(nv):
                        w[ii][v] = w[ii][v] + jnp.where(rs[v] < c8, a8 * bs[v], 0.0)
            for ii in range(nsub):
                r0 = ii * PEER_KEYS + j0
                stage_ref[plane[2 + hf], r0:r0 + EXPERT_JCHUNK, ll * LANES:(ll + 1) * LANES] = (
                    jnp.concatenate(w[ii], axis=0))

    def stage_block(u_ref, blk):
        for ii in range(nsub):
            i = blk * nsub + ii
            for h in range(PEER_HEADS):
                abc_ref[0, ii, h] = jnp.broadcast_to(a_ref[h, pl.ds(i, 1), :], (SUBLANES, tt))
                abc_ref[1, ii, h] = jnp.broadcast_to(c_ref[h, pl.ds(i, 1), :], (SUBLANES, tt))
        for lt in range(2 * lt_per_half):
            gate_weights(lt)
        for hf in range(2):
            hs = slice(hf * half, (hf + 1) * half)
            stage_ref[plane[hf]] = jnp.dot(u_ref[...], h_ref[:, hs],
                                           preferred_element_type=F32)

    @pl.when(e == 0)
    def _():
        acc_ref[...] = jnp.zeros(acc_ref.shape, F32)
        stage_block(u0_ref, 0)

    for lt in range(2 * lt_per_half):
        hf, ll = divmod(lt, lt_per_half)
        x = stage_ref[plane[hf], :, ll * LANES:(ll + 1) * LANES]
        gelu = x * (1.0 + lax.erf(x * math.sqrt(0.5)))
        w = stage_ref[plane[2 + hf], :, ll * LANES:(ll + 1) * LANES]
        g_ref[:, lt * LANES:(lt + 1) * LANES] = (w * gelu).astype(BF16)
    for hf in range(2):
        hs = slice(hf * half, (hf + 1) * half)
        acc_ref[:, hs] += jnp.dot(vt_ref[...], g_ref[:, hs], preferred_element_type=F32)

    stage_block(un_ref, jnp.minimum(e + 1, last))

    @pl.when(e == last)
    def _():
        o_ref[...] = x1_ref[...] + acc_ref[...].T


def _peer_experts(h2t, a, c0, b, r1, u, vt, x1):
    t = h2t.shape[1]
    tt, eb = EXPERT_TT, EXPERT_EB
    nblk = PEER_EXPERTS // eb
    sel = pl.BlockSpec((PEER_HEADS, PEER_KEYS, tt), lambda i, e: (0, 0, i))
    row = pl.BlockSpec((tt, D_MODEL), lambda i, e: (i, 0))
    return pl.pallas_call(
        _experts_kernel,
        grid=(t // tt, nblk),
        in_specs=[pl.BlockSpec(memory_space=pltpu.SMEM),
                  pl.BlockSpec((D_MODEL, tt), lambda i, e: (0, i)), sel, sel, sel, sel,
                  pl.BlockSpec((eb, D_MODEL), lambda i, e: (0, 0)),
                  pl.BlockSpec((eb, D_MODEL), lambda i, e: (jnp.minimum(e + 1, nblk - 1), 0)),
                  pl.BlockSpec((D_MODEL, eb), lambda i, e: (0, e)),
                  row],
        out_specs=row,
        out_shape=jax.ShapeDtypeStruct((t, D_MODEL), F32),
        scratch_shapes=[pltpu.VMEM((D_MODEL, tt), F32), pltpu.VMEM((eb, tt), BF16),
                        pltpu.VMEM((2, eb // PEER_KEYS, PEER_HEADS, SUBLANES, tt), F32),
                        pltpu.VMEM((4, eb, tt // 2), F32)],
        compiler_params=_cparams(("parallel", "arbitrary")),
        name="peer_experts",
    )(jnp.zeros((4,), jnp.int32), h2t, a, c0, b, r1, u, u, vt, x1)


def _layer(x2, batch, seq, mix_gain, w_in, gate_bias, q_gain, k_gain, sinks,
           w_up_swa, w_up_sb, w_out, ffn_gain, w_q, sub_keys, u, v):
    qa, ka, va, qb, kb, vb, gl = _inproj(x2, mix_gain.reshape(1, -1), w_in.astype(BF16))
    ya = _swa(qa, ka, va, q_gain, k_gain, sinks, batch, seq)
    yb = _stickbreak(qb, kb, vb, batch, seq)
    x1, h2t, qt = _merge(ya, yb, gl, gate_bias.reshape(1, -1), x2,
                         w_up_swa.astype(BF16), w_up_sb.astype(BF16), w_out.astype(BF16),
                         ffn_gain.reshape(1, -1), w_q.T.astype(BF16))
    a, c0, b, r1 = _peer_select(qt, sub_keys.astype(BF16))
    return _peer_experts(h2t, a, c0, b, r1, u.astype(BF16), v.T.astype(BF16), x1)


def kernel(x, mix_norm_gain, w_in, gate_bias, swa_q_gain, swa_k_gain, swa_sinks, w_up_swa, w_up_sb,
           w_out, ffn_norm_gain, peer_w_q, peer_sub_keys, peer_u, peer_v):
    batch, seq, d = x.shape
    x2 = x.reshape(batch * seq, d)
    for layer in range(mix_norm_gain.shape[0]):
        x2 = _layer(x2, batch, seq, mix_norm_gain[layer], w_in[layer], gate_bias[layer],
                    swa_q_gain[layer], swa_k_gain[layer], swa_sinks[layer], w_up_swa[layer],
                    w_up_sb[layer], w_out[layer], ffn_norm_gain[layer], peer_w_q[layer],
                    peer_sub_keys[layer], peer_u[layer], peer_v[layer])
    return x2.reshape(batch, seq, d)
```

```python
import functools
import math

import jax
import jax.numpy as jnp
from jax import lax
from jax.experimental import pallas as pl
from jax.experimental.pallas import tpu as pltpu

F32 = jnp.float32
BF16 = jnp.bfloat16

D_MODEL = 1024
HEAD_DIM = 64
CHUNK = 64
RMS_EPS = 1e-6
NEG_INF = -1e30

SWA_Q_HEADS = 8
SWA_GROUP = 4
SWA_WIDTH = 512
SWA_KV_WIDTH = 128
SB_HEADS = 8
SB_WIDTH = 512
GATE_WIDTH = 2 * D_MODEL
IN_WIDTH = SWA_WIDTH + 2 * SWA_KV_WIDTH + 3 * SB_WIDTH + GATE_WIDTH

PEER_HEADS = 8
PEER_KEYS = 128
PEER_HALF = 128
PEER_TOPK = 16
PEER_EXPERTS = PEER_KEYS * PEER_KEYS

LANES = 128
SUBLANES = 8
BLK = 128

INPROJ_TM = 512
MERGE_TM = 256
SELECT_TT = 512
SELECT_LANES = 256
EXPERT_TT = 512
EXPERT_EB = 512
EXPERT_JCHUNK = 32
SB_QROWS = 512
VMEM_LIMIT = 56 * 1024 * 1024

_NT = (((1,), (1,)), ((), ()))


def _cparams(sem):
    return pltpu.CompilerParams(dimension_semantics=sem, vmem_limit_bytes=VMEM_LIMIT)


def _inproj_kernel(x_ref, g_ref, w_ref, qa_ref, ka_ref, va_ref, qb_ref, kb_ref, vb_ref, gl_ref):
    x = x_ref[...]
    ms = jnp.mean(x * x, axis=-1, keepdims=True)
    h = (x * lax.rsqrt(ms + RMS_EPS) * g_ref[...]).astype(BF16)

    def proj(lo, hi):
        return jnp.dot(h, w_ref[:, lo:hi], preferred_element_type=F32)

    o = 0
    qa_ref[...] = proj(o, o + SWA_WIDTH); o += SWA_WIDTH
    ka_ref[...] = proj(o, o + SWA_KV_WIDTH); o += SWA_KV_WIDTH
    va_ref[...] = proj(o, o + SWA_KV_WIDTH).astype(BF16); o += SWA_KV_WIDTH
    qb_ref[...] = (proj(o, o + SB_WIDTH) * (HEAD_DIM ** -0.5)).astype(BF16); o += SB_WIDTH
    kb_ref[...] = proj(o, o + SB_WIDTH).astype(BF16); o += SB_WIDTH
    vb_ref[...] = proj(o, o + SB_WIDTH).astype(BF16); o += SB_WIDTH
    gl_ref[...] = proj(o, o + GATE_WIDTH)


def _inproj(x2, gain, w_in):
    t = x2.shape[0]
    tm = INPROJ_TM
    row = lambda w: pl.BlockSpec((tm, w), lambda i: (i, 0))
    full = lambda a: pl.BlockSpec(a.shape, lambda i: (0,) * a.ndim)
    widths = (SWA_WIDTH, SWA_KV_WIDTH, SWA_KV_WIDTH, SB_WIDTH, SB_WIDTH, SB_WIDTH, GATE_WIDTH)
    dtypes = (F32, F32, BF16, BF16, BF16, BF16, F32)
    return pl.pallas_call(
        _inproj_kernel,
        grid=(t // tm,),
        in_specs=[row(D_MODEL), full(gain), full(w_in)],
        out_specs=[row(w) for w in widths],
        out_shape=[jax.ShapeDtypeStruct((t, w), dt) for w, dt in zip(widths, dtypes)],
        compiler_params=_cparams(("parallel",)),
        name="inproj",
    )(x2, gain, w_in)


def _half_rms(x, lo):
    sq = x * x
    s_lo = jnp.sum(jnp.where(lo, sq, 0.0), axis=-1, keepdims=True)
    s_hi = jnp.sum(jnp.where(lo, 0.0, sq), axis=-1, keepdims=True)
    inv = jnp.where(lo, lax.rsqrt(s_lo / HEAD_DIM + RMS_EPS), lax.rsqrt(s_hi / HEAD_DIM + RMS_EPS))
    return x * inv


def _swa_kernel(sink_ref, q_ref, k_ref, v_ref, qg_ref, kg_ref, o_ref,
                kl0, kh0, kl1, kh1, vl0, vh0, vl1, vh1):
    p = pl.program_id(1)
    lo = lax.broadcasted_iota(jnp.int32, (BLK, LANES), 1) < HEAD_DIM
    kv_scratch = (kl0, kh0, kl1, kh1, vl0, vh0, vl1, vh1)

    @pl.when(p == 0)
    def _():
        for r in kv_scratch:
            r[0:BLK, :] = jnp.zeros((BLK, LANES), BF16)

    row = pl.multiple_of((p + 1) * BLK, BLK)
    kn = _half_rms(k_ref[...], lo) * kg_ref[...]
    kr = pltpu.roll(kn, HEAD_DIM, 1)
    v = v_ref[...].astype(F32)
    vr = pltpu.roll(v, HEAD_DIM, 1)
    kl0[pl.ds(row, BLK), :] = jnp.where(lo, kn, 0.0).astype(BF16)
    kh0[pl.ds(row, BLK), :] = jnp.where(lo, 0.0, kr).astype(BF16)
    kl1[pl.ds(row, BLK), :] = jnp.where(lo, kr, 0.0).astype(BF16)
    kh1[pl.ds(row, BLK), :] = jnp.where(lo, 0.0, kn).astype(BF16)
    vl0[pl.ds(row, BLK), :] = jnp.where(lo, v, 0.0).astype(BF16)
    vh0[pl.ds(row, BLK), :] = jnp.where(lo, 0.0, vr).astype(BF16)
    vl1[pl.ds(row, BLK), :] = jnp.where(lo, vr, 0.0).astype(BF16)
    vh1[pl.ds(row, BLK), :] = jnp.where(lo, 0.0, v).astype(BF16)

    qi = lax.broadcasted_iota(jnp.int32, (BLK, 2 * BLK), 0)
    kj = lax.broadcasted_iota(jnp.int32, (BLK, 2 * BLK), 1)
    dist = jnp.abs(qi + BLK - kj).astype(F32)
    qc = qi // CHUNK
    kc = kj // CHUNK
    vis = (kc >= qc) & (kc <= qc + 2) & ((kj >= BLK) | (p > 0))
    win = pl.ds(pl.multiple_of(p * BLK, BLK), 2 * BLK)

    for m in range(SWA_Q_HEADS // 2):
        g = (2 * m) // SWA_GROUP
        kv = ((kl0, vl0), (kh0, vh0)) if g == 0 else ((kl1, vl1), (kh1, vh1))
        qn = _half_rms(q_ref[:, m * LANES:(m + 1) * LANES], lo) * qg_ref[...]
        qs = (qn * (HEAD_DIM ** -0.5)).astype(BF16)
        acc = jnp.zeros((BLK, LANES), F32)
        for which, (kref, vref) in enumerate(kv):
            h = 2 * m + which
            slope = 2.0 ** (-(h + 1))
            logits = lax.dot_general(qs, kref[win, :], _NT, preferred_element_type=F32)
            logits = jnp.where(vis, logits - slope * dist, NEG_INF)
            sink = sink_ref[h]
            mx = jnp.maximum(jnp.max(logits, axis=-1, keepdims=True), sink)
            pe = jnp.exp(logits - mx)
            den = jnp.sum(pe, axis=-1, keepdims=True) + jnp.exp(sink - mx)
            probs = (pe / den).astype(BF16)
            acc = acc + jnp.dot(probs, vref[win, :], preferred_element_type=F32)
        o_ref[:, m * LANES:(m + 1) * LANES] = acc.astype(BF16)


def _swa(qa, ka, va, qgain, kgain, sinks, batch, seq):
    nb = seq // BLK
    qg2 = jnp.concatenate([qgain, qgain]).reshape(1, LANES)
    kg2 = jnp.concatenate([kgain, kgain]).reshape(1, LANES)
    blk = lambda w: pl.BlockSpec((BLK, w), lambda b, p: (b * nb + p, 0))
    one = pl.BlockSpec((1, LANES), lambda b, p: (0, 0))
    return pl.pallas_call(
        _swa_kernel,
        grid=(batch, nb),
        in_specs=[pl.BlockSpec(memory_space=pltpu.SMEM), blk(SWA_WIDTH), blk(SWA_KV_WIDTH),
                  blk(SWA_KV_WIDTH), one, one],
        out_specs=blk(SWA_WIDTH),
        out_shape=jax.ShapeDtypeStruct((batch * seq, SWA_WIDTH), BF16),
        scratch_shapes=[pltpu.VMEM((seq + BLK, LANES), BF16)] * 8,
        compiler_params=_cparams(("parallel", "arbitrary")),
        name="swa",
    )(sinks, qa, ka, va, qg2, kg2)


def _sb_kernel(q_ref, k_ref, v_ref, tri_ref, o_ref, kk, vv, carry, acc):
    seq = q_ref.shape[0]
    nb = seq // BLK
    per = SB_QROWS // BLK
    lo = lax.broadcasted_iota(jnp.int32, (nb, BLK, LANES), 2) < HEAD_DIM
    k3 = k_ref[...].reshape(nb, BLK, LANES)
    v3 = v_ref[...].reshape(nb, BLK, LANES)
    zero = jnp.zeros_like(k3)
    kk[:, 0:BLK, :] = jnp.where(lo, k3, zero)
    kk[:, BLK:2 * BLK, :] = jnp.where(lo, zero, k3)
    vv[:, 0:BLK, :] = jnp.where(lo, v3, zero)
    vv[:, BLK:2 * BLK, :] = jnp.where(lo, zero, v3)

    def tile(qs, kj, d):
        r0 = 0 if d is None else d * BLK
        m = SB_QROWS - r0
        q = q_ref[pl.ds(pl.multiple_of(qs * SB_QROWS + r0, BLK), m), :]
        z2 = lax.dot_general(q, kk[kj], _NT, preferred_element_type=F32)
        if d is not None:
            before = (lax.broadcasted_iota(jnp.int32, (m, BLK), 1)
                      < lax.broadcasted_iota(jnp.int32, (m, BLK), 0))
        ws = []
        for hh in range(2):
            z = z2[:, hh * BLK:(hh + 1) * BLK]
            log_keep = -(jnp.maximum(z, 0.0) + jnp.log(1.0 + jnp.exp(-jnp.abs(z))))
            log_beta = log_keep + z
            if d is not None:
                log_keep = jnp.where(before, log_keep, 0.0)
            hi = log_keep.astype(BF16)
            lw = (log_keep - hi.astype(F32)).astype(BF16)
            sc = jnp.dot(jnp.concatenate([hi, lw], axis=1), tri_ref[...], preferred_element_type=F32)
            suffix = sc[:, :BLK] + carry[hh, r0:, :]
            carry[hh, r0:, :] = carry[hh, r0:, :] + sc[:, BLK:]
            w = jnp.exp(log_beta + suffix)
            if d is not None:
                w = jnp.where(before, w, 0.0)
            ws.append(w.astype(BF16))
        acc[r0:, :] = acc[r0:, :] + jnp.dot(jnp.concatenate(ws, axis=1), vv[kj],
                                           preferred_element_type=F32)

    def qblock(qs, c):
        carry[...] = jnp.zeros(carry.shape, F32)
        acc[...] = jnp.zeros(acc.shape, F32)
        for d in reversed(range(per)):
            tile(qs, qs * per + d, d)

        def inner(t, c2):
            tile(qs, qs * per - 1 - t, None)
            return c2

        lax.fori_loop(0, qs * per, inner, 0)
        o_ref[pl.ds(pl.multiple_of(qs * SB_QROWS, BLK), SB_QROWS), :] = acc[...].astype(BF16)
        return c

    lax.fori_loop(0, seq // SB_QROWS, qblock, 0)


def _suffix_matrix():
    j = jnp.arange(BLK)[:, None]
    s = jnp.arange(BLK)[None, :]
    strict = (j > s).astype(BF16)
    half = jnp.concatenate([strict, jnp.ones((BLK, BLK), BF16)], axis=1)
    return jnp.concatenate([half, half], axis=0)


def _stickbreak(qb, kb, vb, batch, seq):
    npair = SB_HEADS // 2
    blk = pl.BlockSpec((seq, LANES), lambda b, m: (b, m))
    tri = _suffix_matrix()
    return pl.pallas_call(
        _sb_kernel,
        grid=(batch, npair),
        in_specs=[blk, blk, blk, pl.BlockSpec(tri.shape, lambda b, m: (0, 0))],
        out_specs=blk,
        out_shape=jax.ShapeDtypeStruct((batch * seq, SB_WIDTH), BF16),
        scratch_shapes=[pltpu.VMEM((seq // BLK, 2 * BLK, LANES), BF16)] * 2
        + [pltpu.VMEM((2, SB_QROWS, BLK), F32), pltpu.VMEM((SB_QROWS, LANES), F32)],
        compiler_params=_cparams(("parallel", "parallel")),
        name="stickbreak",
    )(qb, kb, vb, tri)


def _merge_kernel(ya_ref, yb_ref, gl_ref, bias_ref, x_ref, wa_ref, wb_ref, wo_ref, g2_ref, wq_ref,
                  x1_ref, h2_ref, qt_ref):
    ua = jnp.dot(ya_ref[...], wa_ref[...], preferred_element_type=F32)
    ub = jnp.dot(yb_ref[...], wb_ref[...], preferred_element_type=F32)
    gates = jax.nn.sigmoid(gl_ref[...] + bias_ref[...])
    merged = gates[:, :D_MODEL] * ua + gates[:, D_MODEL:] * ub
    x1 = x_ref[...] + jnp.dot(merged.astype(BF16), wo_ref[...], preferred_element_type=F32)
    x1_ref[...] = x1
    ms = jnp.mean(x1 * x1, axis=-1, keepdims=True)
    h2t = (x1 * lax.rsqrt(ms + RMS_EPS) * g2_ref[...]).T.astype(BF16)
    h2_ref[...] = h2t
    qt_ref[...] = jnp.dot(wq_ref[...], h2t, preferred_element_type=F32).astype(BF16)


def _merge(ya, yb, gl, bias, x2, wa, wb, wo, g2, wq_t):
    t = x2.shape[0]
    tm = MERGE_TM
    row = lambda w: pl.BlockSpec((tm, w), lambda i: (i, 0))
    full = lambda a: pl.BlockSpec(a.shape, lambda i: (0,) * a.ndim)
    nq = wq_t.shape[0]
    per = EXPERT_TT // tm
    return pl.pallas_call(
        _merge_kernel,
        grid=(t // tm,),
        in_specs=[row(SWA_WIDTH), row(SB_WIDTH), row(GATE_WIDTH), full(bias), row(D_MODEL),
                  full(wa), full(wb), full(wo), full(g2), full(wq_t)],
        out_specs=[row(D_MODEL),
                   pl.BlockSpec((None, D_MODEL, tm), lambda i: (i // per, 0, i % per)),
                   pl.BlockSpec((nq, tm), lambda i: (0, i))],
        out_shape=[jax.ShapeDtypeStruct((t, D_MODEL), F32),
                   jax.ShapeDtypeStruct((t // EXPERT_TT, D_MODEL, EXPERT_TT), BF16),
                   jax.ShapeDtypeStruct((nq, t), BF16)],
        compiler_params=_cparams(("parallel",)),
        name="merge",
    )(ya, yb, gl, bias, x2, wa, wb, wo, g2, wq_t)


_CAND_GROUPS = ((0, 16), (1, 8), (2, 5), (3, 4), (4, 3), (5, 2), (6, 2), (7, 2))
_CAND_ROWS = 16 + 7 * SUBLANES + SUBLANES


def _top16(s, want_rank):
    n, tt = s.shape
    pos = lax.broadcasted_iota(jnp.int32, (n, tt), 0).astype(F32)
    rank = jnp.full((n, tt), float(PEER_TOPK), F32) if want_rank else None
    vals, idxs = [], []
    for r in range(PEER_TOPK):
        m = jnp.max(s, axis=0, keepdims=True)
        idx = jnp.min(jnp.where(s == m, pos, float(n)), axis=0, keepdims=True)
        hit = pos == idx
        if want_rank:
            rank = jnp.where(hit, float(r), rank)
        s = jnp.where(hit, -jnp.inf, s)
        vals.append(m)
        idxs.append(idx)
    return vals, idxs, rank


def _select_lanes(s0, s1):
    tt = s0.shape[1]
    v0, i0, _ = _top16(s0, False)
    v1, _, rank1 = _top16(s1, True)

    row8 = lax.broadcasted_iota(jnp.int32, (SUBLANES, tt), 0)
    stack8 = lambda rows: functools.reduce(
        lambda acc, kv: jnp.where(row8 == kv[0], kv[1], acc), enumerate(rows), jnp.zeros((SUBLANES, tt), F32))
    v1_lo, v1_hi = stack8(v1[:8]), stack8(v1[8:])
    v0_hi = stack8(v0[8:])

    pieces = []
    for r0, n in _CAND_GROUPS:
        lo_piece = v0[r0] + v1_lo
        pieces.append(lo_piece if n >= SUBLANES else jnp.where(row8 < n, lo_piece, -jnp.inf))
        if n > SUBLANES:
            pieces.append(v0[r0] + v1_hi)
    pieces.append(v0_hi + v1[0])
    cand = jnp.concatenate(pieces, axis=0)

    pos = lax.broadcasted_iota(jnp.int32, cand.shape, 0).astype(F32)
    chosen = jnp.zeros(cand.shape, F32)
    c = cand
    for _ in range(PEER_TOPK):
        m = jnp.max(c, axis=0, keepdims=True)
        idx = jnp.min(jnp.where(c == m, pos, float(_CAND_ROWS)), axis=0, keepdims=True)
        hit = pos == idx
        chosen = jnp.where(hit, 1.0, chosen)
        c = jnp.where(hit, -jnp.inf, c)

    best = v0[0] + v1[0]
    z = jnp.sum(jnp.where(chosen > 0.0, jnp.exp(cand - best), 0.0), axis=0, keepdims=True)

    counts = []
    off = 0
    for r0, n in _CAND_GROUPS:
        rows = SUBLANES * ((n + SUBLANES - 1) // SUBLANES)
        counts.append(jnp.sum(chosen[off:off + rows], axis=0, keepdims=True))
        off += rows
    for r in range(SUBLANES):
        counts.append(chosen[off + r:off + r + 1])

    key = lax.broadcasted_iota(jnp.int32, s0.shape, 0).astype(F32)
    c0 = jnp.zeros(s0.shape, F32)
    for r in range(PEER_TOPK):
        c0 = jnp.where(key == i0[r], counts[r], c0)

    return jnp.exp(s0 - v0[0]) * (0.5 / z), c0, jnp.exp(s1 - v1[0]), rank1


def _select_kernel(q_ref, sk_ref, a_ref, c_ref, b_ref, r_ref, s_ref):
    s_ref[0] = jnp.dot(sk_ref[0, 0], q_ref[0:PEER_HALF, :], preferred_element_type=F32)
    s_ref[1] = jnp.dot(sk_ref[0, 1], q_ref[PEER_HALF:2 * PEER_HALF, :], preferred_element_type=F32)

    def lane_tile(lt, carry):
        ls = pl.ds(pl.multiple_of(lt * SELECT_LANES, SELECT_LANES), SELECT_LANES)
        a, c0, b, rank1 = _select_lanes(s_ref[0, :, ls], s_ref[1, :, ls])
        a_ref[0, :, ls] = a
        c_ref[0, :, ls] = c0
        b_ref[0, :, ls] = b
        r_ref[0, :, ls] = rank1
        return carry

    lax.fori_loop(0, q_ref.shape[1] // SELECT_LANES, lane_tile, 0)


def _peer_select(qt, sub_keys):
    t = qt.shape[1]
    tt = SELECT_TT
    out = pl.BlockSpec((None, 1, PEER_KEYS, tt), lambda i, h: (i, h, 0, 0))
    shp = lambda dt: jax.ShapeDtypeStruct((t // tt, PEER_HEADS, PEER_KEYS, tt), dt)
    return pl.pallas_call(
        _select_kernel,
        grid=(t // tt, PEER_HEADS),
        in_specs=[pl.BlockSpec((2 * PEER_HALF, tt), lambda i, h: (h, i)),
                  pl.BlockSpec((1, 2, PEER_KEYS, PEER_HALF), lambda i, h: (h, 0, 0, 0))],
        out_specs=[out] * 4,
        out_shape=[shp(F32)] * 4,
        scratch_shapes=[pltpu.VMEM((2, PEER_KEYS, tt), F32)],
        compiler_params=_cparams(("parallel", "parallel")),
        name="peer_select",
    )(qt, sub_keys)


def _experts_kernel(zero_ref, h_ref, a_ref, c_ref, b_ref, r_ref, u0_ref, un_ref, vt_ref, x1_ref, o_ref,
                    acc_ref, g_ref, abc_ref, stage_ref):
    e = pl.program_id(1)
    last = pl.num_programs(1) - 1
    nsub = un_ref.shape[0] // PEER_KEYS
    tt = h_ref.shape[1]
    half = tt // 2
    lt_per_half = half // LANES
    nv = EXPERT_JCHUNK // SUBLANES
    plane = [zero_ref[k] + k for k in range(4)]

    def gate_weights(lt):
        hf, ll = divmod(lt, lt_per_half)
        ls = slice(lt * LANES, (lt + 1) * LANES)
        for jc in range(PEER_KEYS // EXPERT_JCHUNK):
            j0 = jc * EXPERT_JCHUNK
            w = [[jnp.zeros((SUBLANES, LANES), F32)] * nv for _ in range(nsub)]
            for h in range(PEER_HEADS):
                bs = [b_ref[h, j0 + v * SUBLANES:j0 + (v + 1) * SUBLANES, ls] for v in range(nv)]
                rs = [r_ref[h, j0 + v * SUBLANES:j0 + (v + 1) * SUBLANES, ls] for v in range(nv)]
                for ii in range(nsub):
                    a8 = abc_ref[0, ii, h, :, ls]
                    c8 = abc_ref[1, ii, h, :, ls]
                    for v in range(nv):
                        w[ii][v] = w[ii][v] + jnp.where(rs[v] < c8, a8 * bs[v], 0.0)
            for ii in range(nsub):
                r0 = ii * PEER_KEYS + j0
                stage_ref[plane[2 + hf], r0:r0 + EXPERT_JCHUNK, ll * LANES:(ll + 1) * LANES] = (
                    jnp.concatenate(w[ii], axis=0))

    def stage_block(u_ref, blk):
        for ii in range(nsub):
            i = blk * nsub + ii
            for h in range(PEER_HEADS):
                abc_ref[0, ii, h] = jnp.broadcast_to(a_ref[h, pl.ds(i, 1), :], (SUBLANES, tt))
                abc_ref[1, ii, h] = jnp.broadcast_to(c_ref[h, pl.ds(i, 1), :], (SUBLANES, tt))
        for lt in range(2 * lt_per_half):
            gate_weights(lt)
        for hf in range(2):
            hs = slice(hf * half, (hf + 1) * half)
            stage_ref[plane[hf]] = jnp.dot(u_ref[...], h_ref[:, hs],
                                           preferred_element_type=F32)

    @pl.when(e == 0)
    def _():
        acc_ref[...] = jnp.zeros(acc_ref.shape, F32)
        stage_block(u0_ref, 0)

    for lt in range(2 * lt_per_half):
        hf, ll = divmod(lt, lt_per_half)
        x = stage_ref[plane[hf], :, ll * LANES:(ll + 1) * LANES]
        gelu = x * (1.0 + lax.erf(x * math.sqrt(0.5)))
        w = stage_ref[plane[2 + hf], :, ll * LANES:(ll + 1) * LANES]
        g_ref[:, lt * LANES:(lt + 1) * LANES] = (w * gelu).astype(BF16)
    for hf in range(2):
        hs = slice(hf * half, (hf + 1) * half)
        acc_ref[:, hs] += jnp.dot(vt_ref[...], g_ref[:, hs], preferred_element_type=F32)

    stage_block(un_ref, jnp.minimum(e + 1, last))

    @pl.when(e == last)
    def _():
        o_ref[...] = x1_ref[...] + acc_ref[...].T


def _peer_experts(h2t, a, c0, b, r1, u, vt, x1):
    tt, eb = EXPERT_TT, EXPERT_EB
    assert SELECT_TT == tt and h2t.shape[2] == tt
    t = h2t.shape[0] * tt
    nblk = PEER_EXPERTS // eb
    sel = pl.BlockSpec((None, PEER_HEADS, PEER_KEYS, tt), lambda i, e: (i, 0, 0, 0))
    row = pl.BlockSpec((tt, D_MODEL), lambda i, e: (i, 0))
    return pl.pallas_call(
        _experts_kernel,
        grid=(t // tt, nblk),
        in_specs=[pl.BlockSpec(memory_space=pltpu.SMEM),
                  pl.BlockSpec((None, D_MODEL, tt), lambda i, e: (i, 0, 0)), sel, sel, sel, sel,
                  pl.BlockSpec((eb, D_MODEL), lambda i, e: (0, 0)),
                  pl.BlockSpec((eb, D_MODEL), lambda i, e: (jnp.minimum(e + 1, nblk - 1), 0)),
                  pl.BlockSpec((None, D_MODEL, eb), lambda i, e: (e, 0, 0)),
                  row],
        out_specs=row,
        out_shape=jax.ShapeDtypeStruct((t, D_MODEL), F32),
        scratch_shapes=[pltpu.VMEM((D_MODEL, tt), F32), pltpu.VMEM((eb, tt), BF16),
                        pltpu.VMEM((2, eb // PEER_KEYS, PEER_HEADS, SUBLANES, tt), F32),
                        pltpu.VMEM((4, eb, tt // 2), F32)],
        compiler_params=_cparams(("parallel", "arbitrary")),
        name="peer_experts",
    )(jnp.zeros((4,), jnp.int32), h2t, a, c0, b, r1, u, u, vt, x1)


def _layer(x2, batch, seq, mix_gain, w_in, gate_bias, q_gain, k_gain, sinks,
           w_up_swa, w_up_sb, w_out, ffn_gain, w_q, sub_keys, u, v):
    qa, ka, va, qb, kb, vb, gl = _inproj(x2, mix_gain.reshape(1, -1), w_in.astype(BF16))
    ya = _swa(qa, ka, va, q_gain, k_gain, sinks, batch, seq)
    yb = _stickbreak(qb, kb, vb, batch, seq)
    x1, h2t, qt = _merge(ya, yb, gl, gate_bias.reshape(1, -1), x2,
                         w_up_swa.astype(BF16), w_up_sb.astype(BF16), w_out.astype(BF16),
                         ffn_gain.reshape(1, -1), w_q.T.astype(BF16))
    a, c0, b, r1 = _peer_select(qt, sub_keys.astype(BF16))
    vt = v.reshape(PEER_EXPERTS // EXPERT_EB, EXPERT_EB, D_MODEL).transpose(0, 2, 1).astype(BF16)
    return _peer_experts(h2t, a, c0, b, r1, u.astype(BF16), vt, x1)


def kernel(x, mix_norm_gain, w_in, gate_bias, swa_q_gain, swa_k_gain, swa_sinks, w_up_swa, w_up_sb,
           w_out, ffn_norm_gain, peer_w_q, peer_sub_keys, peer_u, peer_v):
    batch, seq, d = x.shape
    x2 = x.reshape(batch * seq, d)
    for layer in range(mix_norm_gain.shape[0]):
        x2 = _layer(x2, batch, seq, mix_norm_gain[layer], w_in[layer], gate_bias[layer],
                    swa_q_gain[layer], swa_k_gain[layer], swa_sinks[layer], w_up_swa[layer],
                    w_up_sb[layer], w_out[layer], ffn_norm_gain[layer], peer_w_q[layer],
                    peer_sub_keys[layer], peer_u[layer], peer_v[layer])
    return x2.reshape(batch, seq, d)
```

```python
import functools
import math

import jax
import jax.numpy as jnp
from jax import lax
from jax.experimental import pallas as pl
from jax.experimental.pallas import tpu as pltpu

F32 = jnp.float32
BF16 = jnp.bfloat16

D_MODEL = 1024
HEAD_DIM = 64
CHUNK = 64
RMS_EPS = 1e-6
NEG_INF = -1e30

SWA_Q_HEADS = 8
SWA_GROUP = 4
SWA_WIDTH = 512
SWA_KV_WIDTH = 128
SB_HEADS = 8
SB_WIDTH = 512
GATE_WIDTH = 2 * D_MODEL
IN_WIDTH = SWA_WIDTH + 2 * SWA_KV_WIDTH + 3 * SB_WIDTH + GATE_WIDTH

PEER_HEADS = 8
PEER_KEYS = 128
PEER_HALF = 128
PEER_TOPK = 16
PEER_EXPERTS = PEER_KEYS * PEER_KEYS

LANES = 128
SUBLANES = 8
BLK = 128

INPROJ_TM = 512
MERGE_TM = 256
SELECT_TT = 512
SELECT_LANES = 256
EXPERT_TT = 512
EXPERT_EB = 512
EXPERT_JCHUNK = 32
SB_QROWS = 512
VMEM_LIMIT = 56 * 1024 * 1024

_NT = (((1,), (1,)), ((), ()))


def _cparams(sem):
    return pltpu.CompilerParams(dimension_semantics=sem, vmem_limit_bytes=VMEM_LIMIT)


def _inproj_kernel(x_ref, g_ref, w_ref, qa_ref, ka_ref, va_ref, qb_ref, kb_ref, vb_ref, gl_ref):
    x = x_ref[...]
    ms = jnp.mean(x * x, axis=-1, keepdims=True)
    h = (x * lax.rsqrt(ms + RMS_EPS) * g_ref[...]).astype(BF16)

    def proj(lo, hi):
        return jnp.dot(h, w_ref[:, lo:hi], preferred_element_type=F32)

    o = 0
    qa_ref[...] = proj(o, o + SWA_WIDTH); o += SWA_WIDTH
    ka_ref[...] = proj(o, o + SWA_KV_WIDTH); o += SWA_KV_WIDTH
    va_ref[...] = proj(o, o + SWA_KV_WIDTH).astype(BF16); o += SWA_KV_WIDTH
    qb_ref[...] = (proj(o, o + SB_WIDTH) * (HEAD_DIM ** -0.5)).astype(BF16); o += SB_WIDTH
    kb_ref[...] = proj(o, o + SB_WIDTH).astype(BF16); o += SB_WIDTH
    vb_ref[...] = proj(o, o + SB_WIDTH).astype(BF16); o += SB_WIDTH
    gl_ref[...] = proj(o, o + GATE_WIDTH)


def _inproj(x2, gain, w_in):
    t = x2.shape[0]
    tm = INPROJ_TM
    row = lambda w: pl.BlockSpec((tm, w), lambda i: (i, 0))
    full = lambda a: pl.BlockSpec(a.shape, lambda i: (0,) * a.ndim)
    widths = (SWA_WIDTH, SWA_KV_WIDTH, SWA_KV_WIDTH, SB_WIDTH, SB_WIDTH, SB_WIDTH, GATE_WIDTH)
    dtypes = (F32, F32, BF16, BF16, BF16, BF16, F32)
    return pl.pallas_call(
        _inproj_kernel,
        grid=(t // tm,),
        in_specs=[row(D_MODEL), full(gain), full(w_in)],
        out_specs=[row(w) for w in widths],
        out_shape=[jax.ShapeDtypeStruct((t, w), dt) for w, dt in zip(widths, dtypes)],
        compiler_params=_cparams(("parallel",)),
        name="inproj",
    )(x2, gain, w_in)


def _half_rms(x, lo):
    sq = x * x
    s_lo = jnp.sum(jnp.where(lo, sq, 0.0), axis=-1, keepdims=True)
    s_hi = jnp.sum(jnp.where(lo, 0.0, sq), axis=-1, keepdims=True)
    inv = jnp.where(lo, lax.rsqrt(s_lo / HEAD_DIM + RMS_EPS), lax.rsqrt(s_hi / HEAD_DIM + RMS_EPS))
    return x * inv


def _swa_kernel(sink_ref, q_ref, k_ref, v_ref, qg_ref, kg_ref, o_ref,
                kl0, kh0, kl1, kh1, vl0, vh0, vl1, vh1):
    p = pl.program_id(1)
    lo = lax.broadcasted_iota(jnp.int32, (BLK, LANES), 1) < HEAD_DIM
    kv_scratch = (kl0, kh0, kl1, kh1, vl0, vh0, vl1, vh1)

    @pl.when(p == 0)
    def _():
        for r in kv_scratch:
            r[0:BLK, :] = jnp.zeros((BLK, LANES), BF16)

    row = pl.multiple_of((p + 1) * BLK, BLK)
    kn = _half_rms(k_ref[...], lo) * kg_ref[...]
    kr = pltpu.roll(kn, HEAD_DIM, 1)
    v = v_ref[...].astype(F32)
    vr = pltpu.roll(v, HEAD_DIM, 1)
    kl0[pl.ds(row, BLK), :] = jnp.where(lo, kn, 0.0).astype(BF16)
    kh0[pl.ds(row, BLK), :] = jnp.where(lo, 0.0, kr).astype(BF16)
    kl1[pl.ds(row, BLK), :] = jnp.where(lo, kr, 0.0).astype(BF16)
    kh1[pl.ds(row, BLK), :] = jnp.where(lo, 0.0, kn).astype(BF16)
    vl0[pl.ds(row, BLK), :] = jnp.where(lo, v, 0.0).astype(BF16)
    vh0[pl.ds(row, BLK), :] = jnp.where(lo, 0.0, vr).astype(BF16)
    vl1[pl.ds(row, BLK), :] = jnp.where(lo, vr, 0.0).astype(BF16)
    vh1[pl.ds(row, BLK), :] = jnp.where(lo, 0.0, v).astype(BF16)

    qi = lax.broadcasted_iota(jnp.int32, (BLK, 2 * BLK), 0)
    kj = lax.broadcasted_iota(jnp.int32, (BLK, 2 * BLK), 1)
    dist = jnp.abs(qi + BLK - kj).astype(F32)
    qc = qi // CHUNK
    kc = kj // CHUNK
    vis = (kc >= qc) & (kc <= qc + 2) & ((kj >= BLK) | (p > 0))
    win = pl.ds(pl.multiple_of(p * BLK, BLK), 2 * BLK)

    for m in range(SWA_Q_HEADS // 2):
        g = (2 * m) // SWA_GROUP
        kv = ((kl0, vl0), (kh0, vh0)) if g == 0 else ((kl1, vl1), (kh1, vh1))
        qn = _half_rms(q_ref[:, m * LANES:(m + 1) * LANES], lo) * qg_ref[...]
        qs = (qn * (HEAD_DIM ** -0.5)).astype(BF16)
        acc = jnp.zeros((BLK, LANES), F32)
        for which, (kref, vref) in enumerate(kv):
            h = 2 * m + which
            slope = 2.0 ** (-(h + 1))
            logits = lax.dot_general(qs, kref[win, :], _NT, preferred_element_type=F32)
            logits = jnp.where(vis, logits - slope * dist, NEG_INF)
            sink = sink_ref[h]
            mx = jnp.maximum(jnp.max(logits, axis=-1, keepdims=True), sink)
            pe = jnp.exp(logits - mx)
            den = jnp.sum(pe, axis=-1, keepdims=True) + jnp.exp(sink - mx)
            probs = (pe / den).astype(BF16)
            acc = acc + jnp.dot(probs, vref[win, :], preferred_element_type=F32)
        o_ref[:, m * LANES:(m + 1) * LANES] = acc.astype(BF16)


def _swa(qa, ka, va, qgain, kgain, sinks, batch, seq):
    nb = seq // BLK
    qg2 = jnp.concatenate([qgain, qgain]).reshape(1, LANES)
    kg2 = jnp.concatenate([kgain, kgain]).reshape(1, LANES)
    blk = lambda w: pl.BlockSpec((BLK, w), lambda b, p: (b * nb + p, 0))
    one = pl.BlockSpec((1, LANES), lambda b, p: (0, 0))
    return pl.pallas_call(
        _swa_kernel,
        grid=(batch, nb),
        in_specs=[pl.BlockSpec(memory_space=pltpu.SMEM), blk(SWA_WIDTH), blk(SWA_KV_WIDTH),
                  blk(SWA_KV_WIDTH), one, one],
        out_specs=blk(SWA_WIDTH),
        out_shape=jax.ShapeDtypeStruct((batch * seq, SWA_WIDTH), BF16),
        scratch_shapes=[pltpu.VMEM((seq + BLK, LANES), BF16)] * 8,
        compiler_params=_cparams(("parallel", "arbitrary")),
        name="swa",
    )(sinks, qa, ka, va, qg2, kg2)


def _sb_kernel(q_ref, k_ref, v_ref, tri_ref, o_ref, kk, vv, carry, acc):
    seq = q_ref.shape[0]
    nb = seq // BLK
    per = SB_QROWS // BLK
    lo = lax.broadcasted_iota(jnp.int32, (nb, BLK, LANES), 2) < HEAD_DIM
    k3 = k_ref[...].reshape(nb, BLK, LANES)
    v3 = v_ref[...].reshape(nb, BLK, LANES)
    zero = jnp.zeros_like(k3)
    kk[:, 0:BLK, :] = jnp.where(lo, k3, zero)
    kk[:, BLK:2 * BLK, :] = jnp.where(lo, zero, k3)
    vv[:, 0:BLK, :] = jnp.where(lo, v3, zero)
    vv[:, BLK:2 * BLK, :] = jnp.where(lo, zero, v3)

    def tile(qs, kj, d):
        r0 = 0 if d is None else d * BLK
        m = SB_QROWS - r0
        q = q_ref[pl.ds(pl.multiple_of(qs * SB_QROWS + r0, BLK), m), :]
        z2 = lax.dot_general(q, kk[kj], _NT, preferred_element_type=F32)
        if d is not None:
            before = (lax.broadcasted_iota(jnp.int32, (m, BLK), 1)
                      < lax.broadcasted_iota(jnp.int32, (m, BLK), 0))
        ws = []
        for hh in range(2):
            z = z2[:, hh * BLK:(hh + 1) * BLK]
            log_keep = -(jnp.maximum(z, 0.0) + jnp.log(1.0 + jnp.exp(-jnp.abs(z))))
            log_beta = log_keep + z
            if d is not None:
                log_keep = jnp.where(before, log_keep, 0.0)
            hi = log_keep.astype(BF16)
            lw = (log_keep - hi.astype(F32)).astype(BF16)
            sc = jnp.dot(jnp.concatenate([hi, lw], axis=1), tri_ref[...], preferred_element_type=F32)
            suffix = sc[:, :BLK] + carry[hh, r0:, :]
            carry[hh, r0:, :] = carry[hh, r0:, :] + sc[:, BLK:]
            w = jnp.exp(log_beta + suffix)
            if d is not None:
                w = jnp.where(before, w, 0.0)
            ws.append(w.astype(BF16))
        acc[r0:, :] = acc[r0:, :] + jnp.dot(jnp.concatenate(ws, axis=1), vv[kj],
                                           preferred_element_type=F32)

    def qblock(qs, c):
        carry[...] = jnp.zeros(carry.shape, F32)
        acc[...] = jnp.zeros(acc.shape, F32)
        for d in reversed(range(per)):
            tile(qs, qs * per + d, d)

        def inner(t, c2):
            tile(qs, qs * per - 1 - t, None)
            return c2

        lax.fori_loop(0, qs * per, inner, 0)
        o_ref[pl.ds(pl.multiple_of(qs * SB_QROWS, BLK), SB_QROWS), :] = acc[...].astype(BF16)
        return c

    lax.fori_loop(0, seq // SB_QROWS, qblock, 0)


def _suffix_matrix():
    j = jnp.arange(BLK)[:, None]
    s = jnp.arange(BLK)[None, :]
    strict = (j > s).astype(BF16)
    half = jnp.concatenate([strict, jnp.ones((BLK, BLK), BF16)], axis=1)
    return jnp.concatenate([half, half], axis=0)


def _stickbreak(qb, kb, vb, batch, seq):
    npair = SB_HEADS // 2
    blk = pl.BlockSpec((seq, LANES), lambda b, m: (b, m))
    tri = _suffix_matrix()
    return pl.pallas_call(
        _sb_kernel,
        grid=(batch, npair),
        in_specs=[blk, blk, blk, pl.BlockSpec(tri.shape, lambda b, m: (0, 0))],
        out_specs=blk,
        out_shape=jax.ShapeDtypeStruct((batch * seq, SB_WIDTH), BF16),
        scratch_shapes=[pltpu.VMEM((seq // BLK, 2 * BLK, LANES), BF16)] * 2
        + [pltpu.VMEM((2, SB_QROWS, BLK), F32), pltpu.VMEM((SB_QROWS, LANES), F32)],
        compiler_params=_cparams(("parallel", "parallel")),
        name="stickbreak",
    )(qb, kb, vb, tri)


def _merge_kernel(ya_ref, yb_ref, gl_ref, bias_ref, x_ref, wa_ref, wb_ref, wo_ref, g2_ref, wq_ref,
                  x1_ref, h2_ref, qt_ref):
    ua = jnp.dot(ya_ref[...], wa_ref[...], preferred_element_type=F32)
    ub = jnp.dot(yb_ref[...], wb_ref[...], preferred_element_type=F32)
    gates = jax.nn.sigmoid(gl_ref[...] + bias_ref[...])
    merged = gates[:, :D_MODEL] * ua + gates[:, D_MODEL:] * ub
    x1 = x_ref[...] + jnp.dot(merged.astype(BF16), wo_ref[...], preferred_element_type=F32)
    x1_ref[...] = x1
    ms = jnp.mean(x1 * x1, axis=-1, keepdims=True)
    h2t = (x1 * lax.rsqrt(ms + RMS_EPS) * g2_ref[...]).T.astype(BF16)
    h2_ref[...] = h2t
    qt_ref[...] = jnp.dot(wq_ref[...], h2t, preferred_element_type=F32).astype(BF16)


def _merge(ya, yb, gl, bias, x2, wa, wb, wo, g2, wq_t):
    t = x2.shape[0]
    tm = MERGE_TM
    row = lambda w: pl.BlockSpec((tm, w), lambda i: (i, 0))
    full = lambda a: pl.BlockSpec(a.shape, lambda i: (0,) * a.ndim)
    nq = wq_t.shape[0]
    per = EXPERT_TT // tm
    return pl.pallas_call(
        _merge_kernel,
        grid=(t // tm,),
        in_specs=[row(SWA_WIDTH), row(SB_WIDTH), row(GATE_WIDTH), full(bias), row(D_MODEL),
                  full(wa), full(wb), full(wo), full(g2), full(wq_t)],
        out_specs=[row(D_MODEL),
                   pl.BlockSpec((None, D_MODEL, tm), lambda i: (i // per, 0, i % per)),
                   pl.BlockSpec((nq, tm), lambda i: (0, i))],
        out_shape=[jax.ShapeDtypeStruct((t, D_MODEL), F32),
                   jax.ShapeDtypeStruct((t // EXPERT_TT, D_MODEL, EXPERT_TT), BF16),
                   jax.ShapeDtypeStruct((nq, t), BF16)],
        compiler_params=_cparams(("parallel",)),
        name="merge",
    )(ya, yb, gl, bias, x2, wa, wb, wo, g2, wq_t)


_CAND_GROUPS = ((0, 16), (1, 8), (2, 5), (3, 4), (4, 3), (5, 2), (6, 2), (7, 2))
_CAND_ROWS = 16 + 7 * SUBLANES + SUBLANES


def _top16(s, exact_ties):
    n, tt = s.shape
    rank = jnp.full((n, tt), float(PEER_TOPK), F32)
    if exact_ties:
        pos = lax.broadcasted_iota(jnp.int32, (n, tt), 0).astype(F32)
    vals = []
    for r in range(PEER_TOPK):
        m = jnp.max(s, axis=0, keepdims=True)
        hit = s == m
        if exact_ties:
            hit = pos == jnp.min(jnp.where(hit, pos, float(n)), axis=0, keepdims=True)
        rank = jnp.where(hit, float(r), rank)
        s = jnp.where(hit, -jnp.inf, s)
        vals.append(m)
    return vals, rank


def _select_lanes(s0, s1, exact_ties):
    tt = s0.shape[1]
    v0, rank0 = _top16(s0, exact_ties)
    v1, rank1 = _top16(s1, exact_ties)

    row8 = lax.broadcasted_iota(jnp.int32, (SUBLANES, tt), 0)
    stack8 = lambda rows: functools.reduce(
        lambda acc, kv: jnp.where(row8 == kv[0], kv[1], acc), enumerate(rows), jnp.zeros((SUBLANES, tt), F32))
    v1_lo, v1_hi = stack8(v1[:8]), stack8(v1[8:])
    v0_hi = stack8(v0[8:])

    pieces = []
    for r0, n in _CAND_GROUPS:
        lo_piece = v0[r0] + v1_lo
        pieces.append(lo_piece if n >= SUBLANES else jnp.where(row8 < n, lo_piece, -jnp.inf))
        if n > SUBLANES:
            pieces.append(v0[r0] + v1_hi)
    pieces.append(v0_hi + v1[0])
    cand = jnp.concatenate(pieces, axis=0)

    if exact_ties:
        pos = lax.broadcasted_iota(jnp.int32, cand.shape, 0).astype(F32)
    chosen = jnp.zeros(cand.shape, F32)
    c = cand
    for _ in range(PEER_TOPK):
        m = jnp.max(c, axis=0, keepdims=True)
        hit = c == m
        if exact_ties:
            hit = pos == jnp.min(jnp.where(hit, pos, float(_CAND_ROWS)), axis=0, keepdims=True)
        chosen = jnp.where(hit, 1.0, chosen)
        c = jnp.where(hit, -jnp.inf, c)

    best = v0[0] + v1[0]
    z = jnp.sum(jnp.where(chosen > 0.0, jnp.exp(cand - best), 0.0), axis=0, keepdims=True)

    counts = []
    off = 0
    for r0, n in _CAND_GROUPS:
        rows = SUBLANES * ((n + SUBLANES - 1) // SUBLANES)
        counts.append(jnp.sum(chosen[off:off + rows], axis=0, keepdims=True))
        off += rows
    for r in range(SUBLANES):
        counts.append(chosen[off + r:off + r + 1])

    c0 = jnp.zeros(s0.shape, F32)
    for r in range(PEER_TOPK):
        c0 = jnp.where(rank0 == float(r), counts[r], c0)

    taken = lambda rk: jnp.sum(jnp.where(rk < float(PEER_TOPK), 1.0, 0.0), axis=0, keepdims=True)
    unique = ((taken(rank0) == float(PEER_TOPK)) & (taken(rank1) == float(PEER_TOPK))
              & (jnp.sum(chosen, axis=0, keepdims=True) == float(PEER_TOPK)))
    return (jnp.exp(s0 - v0[0]) * (0.5 / z), c0, jnp.exp(s1 - v1[0]), rank1), unique


def _select_kernel(q_ref, sk_ref, a_ref, c_ref, b_ref, r_ref, s_ref):
    s_ref[0] = jnp.dot(sk_ref[0, 0], q_ref[0:PEER_HALF, :], preferred_element_type=F32)
    s_ref[1] = jnp.dot(sk_ref[0, 1], q_ref[PEER_HALF:2 * PEER_HALF, :], preferred_element_type=F32)

    def lane_tile(lt, carry):
        ls = pl.ds(pl.multiple_of(lt * SELECT_LANES, SELECT_LANES), SELECT_LANES)

        def run(exact_ties):
            outs, unique = _select_lanes(s_ref[0, :, ls], s_ref[1, :, ls], exact_ties)
            for ref, val in zip((a_ref, c_ref, b_ref, r_ref), outs):
                ref[0, :, ls] = val
            return unique

        unique = run(False)

        @pl.when(jnp.min(jnp.where(unique, 1.0, 0.0)) < 0.5)
        def _():
            run(True)

        return carry

    lax.fori_loop(0, q_ref.shape[1] // SELECT_LANES, lane_tile, 0)


def _peer_select(qt, sub_keys):
    t = qt.shape[1]
    tt = SELECT_TT
    out = pl.BlockSpec((None, 1, PEER_KEYS, tt), lambda i, h: (i, h, 0, 0))
    shp = lambda dt: jax.ShapeDtypeStruct((t // tt, PEER_HEADS, PEER_KEYS, tt), dt)
    return pl.pallas_call(
        _select_kernel,
        grid=(t // tt, PEER_HEADS),
        in_specs=[pl.BlockSpec((2 * PEER_HALF, tt), lambda i, h: (h, i)),
                  pl.BlockSpec((1, 2, PEER_KEYS, PEER_HALF), lambda i, h: (h, 0, 0, 0))],
        out_specs=[out] * 4,
        out_shape=[shp(F32)] * 4,
        scratch_shapes=[pltpu.VMEM((2, PEER_KEYS, tt), F32)],
        compiler_params=_cparams(("parallel", "parallel")),
        name="peer_select",
    )(qt, sub_keys)


def _experts_kernel(zero_ref, h_ref, a_ref, c_ref, b_ref, r_ref, u0_ref, un_ref, vt_ref, x1_ref, o_ref,
                    acc_ref, g_ref, abc_ref, stage_ref):
    e = pl.program_id(1)
    last = pl.num_programs(1) - 1
    nsub = un_ref.shape[0] // PEER_KEYS
    tt = h_ref.shape[1]
    half = tt // 2
    lt_per_half = half // LANES
    nv = EXPERT_JCHUNK // SUBLANES
    plane = [zero_ref[k] + k for k in range(4)]

    def gate_weights(lt):
        hf, ll = divmod(lt, lt_per_half)
        ls = slice(lt * LANES, (lt + 1) * LANES)
        for jc in range(PEER_KEYS // EXPERT_JCHUNK):
            j0 = jc * EXPERT_JCHUNK
            w = [[jnp.zeros((SUBLANES, LANES), F32)] * nv for _ in range(nsub)]
            for h in range(PEER_HEADS):
                bs = [b_ref[h, j0 + v * SUBLANES:j0 + (v + 1) * SUBLANES, ls] for v in range(nv)]
                rs = [r_ref[h, j0 + v * SUBLANES:j0 + (v + 1) * SUBLANES, ls] for v in range(nv)]
                for ii in range(nsub):
                    a8 = abc_ref[0, ii, h, :, ls]
                    c8 = abc_ref[1, ii, h, :, ls]
                    for v in range(nv):
                        w[ii][v] = w[ii][v] + jnp.where(rs[v] < c8, a8 * bs[v], 0.0)
            for ii in range(nsub):
                r0 = ii * PEER_KEYS + j0
                stage_ref[plane[2 + hf], r0:r0 + EXPERT_JCHUNK, ll * LANES:(ll + 1) * LANES] = (
                    jnp.concatenate(w[ii], axis=0))

    def stage_block(u_ref, blk):
        for ii in range(nsub):
            i = blk * nsub + ii
            for h in range(PEER_HEADS):
                abc_ref[0, ii, h] = jnp.broadcast_to(a_ref[h, pl.ds(i, 1), :], (SUBLANES, tt))
                abc_ref[1, ii, h] = jnp.broadcast_to(c_ref[h, pl.ds(i, 1), :], (SUBLANES, tt))
        for lt in range(2 * lt_per_half):
            gate_weights(lt)
        for hf in range(2):
            hs = slice(hf * half, (hf + 1) * half)
            stage_ref[plane[hf]] = jnp.dot(u_ref[...], h_ref[:, hs],
                                           preferred_element_type=F32)

    @pl.when(e == 0)
    def _():
        acc_ref[...] = jnp.zeros(acc_ref.shape, F32)
        stage_block(u0_ref, 0)

    for lt in range(2 * lt_per_half):
        hf, ll = divmod(lt, lt_per_half)
        x = stage_ref[plane[hf], :, ll * LANES:(ll + 1) * LANES]
        gelu = x * (1.0 + lax.erf(x * math.sqrt(0.5)))
        w = stage_ref[plane[2 + hf], :, ll * LANES:(ll + 1) * LANES]
        g_ref[:, lt * LANES:(lt + 1) * LANES] = (w * gelu).astype(BF16)
    for hf in range(2):
        hs = slice(hf * half, (hf + 1) * half)
        acc_ref[:, hs] += jnp.dot(vt_ref[...], g_ref[:, hs], preferred_element_type=F32)

    stage_block(un_ref, jnp.minimum(e + 1, last))

    @pl.when(e == last)
    def _():
        o_ref[...] = x1_ref[...] + acc_ref[...].T


def _peer_experts(h2t, a, c0, b, r1, u, vt, x1):
    tt, eb = EXPERT_TT, EXPERT_EB
    assert SELECT_TT == tt and h2t.shape[2] == tt
    t = h2t.shape[0] * tt
    nblk = PEER_EXPERTS // eb
    sel = pl.BlockSpec((None, PEER_HEADS, PEER_KEYS, tt), lambda i, e: (i, 0, 0, 0))
    row = pl.BlockSpec((tt, D_MODEL), lambda i, e: (i, 0))
    return pl.pallas_call(
        _experts_kernel,
        grid=(t // tt, nblk),
        in_specs=[pl.BlockSpec(memory_space=pltpu.SMEM),
                  pl.BlockSpec((None, D_MODEL, tt), lambda i, e: (i, 0, 0)), sel, sel, sel, sel,
                  pl.BlockSpec((eb, D_MODEL), lambda i, e: (0, 0)),
                  pl.BlockSpec((eb, D_MODEL), lambda i, e: (jnp.minimum(e + 1, nblk - 1), 0)),
                  pl.BlockSpec((None, D_MODEL, eb), lambda i, e: (e, 0, 0)),
                  row],
        out_specs=row,
        out_shape=jax.ShapeDtypeStruct((t, D_MODEL), F32),
        scratch_shapes=[pltpu.VMEM((D_MODEL, tt), F32), pltpu.VMEM((eb, tt), BF16),
                        pltpu.VMEM((2, eb // PEER_KEYS, PEER_HEADS, SUBLANES, tt), F32),
                        pltpu.VMEM((4, eb, tt // 2), F32)],
        compiler_params=_cparams(("parallel", "arbitrary")),
        name="peer_experts",
    )(jnp.zeros((4,), jnp.int32), h2t, a, c0, b, r1, u, u, vt, x1)


def _layer(x2, batch, seq, mix_gain, w_in, gate_bias, q_gain, k_gain, sinks,
           w_up_swa, w_up_sb, w_out, ffn_gain, w_q, sub_keys, u, v):
    qa, ka, va, qb, kb, vb, gl = _inproj(x2, mix_gain.reshape(1, -1), w_in.astype(BF16))
    ya = _swa(qa, ka, va, q_gain, k_gain, sinks, batch, seq)
    yb = _stickbreak(qb, kb, vb, batch, seq)
    x1, h2t, qt = _merge(ya, yb, gl, gate_bias.reshape(1, -1), x2,
                         w_up_swa.astype(BF16), w_up_sb.astype(BF16), w_out.astype(BF16),
                         ffn_gain.reshape(1, -1), w_q.T.astype(BF16))
    a, c0, b, r1 = _peer_select(qt, sub_keys.astype(BF16))
    vt = v.reshape(PEER_EXPERTS // EXPERT_EB, EXPERT_EB, D_MODEL).transpose(0, 2, 1).astype(BF16)
    return _peer_experts(h2t, a, c0, b, r1, u.astype(BF16), vt, x1)


def kernel(x, mix_norm_gain, w_in, gate_bias, swa_q_gain, swa_k_gain, swa_sinks, w_up_swa, w_up_sb,
           w_out, ffn_norm_gain, peer_w_q, peer_sub_keys, peer_u, peer_v):
    batch, seq, d = x.shape
    x2 = x.reshape(batch * seq, d)
    for layer in range(mix_norm_gain.shape[0]):
        x2 = _layer(x2, batch, seq, mix_norm_gain[layer], w_in[layer], gate_bias[layer],
                    swa_q_gain[layer], swa_k_gain[layer], swa_sinks[layer], w_up_swa[layer],
                    w_up_sb[layer], w_out[layer], ffn_norm_gain[layer], peer_w_q[layer],
                    peer_sub_keys[layer], peer_u[layer], peer_v[layer])
    return x2.reshape(batch, seq, d)
```

```python
import functools
import math

import jax
import jax.numpy as jnp
from jax import lax
from jax.experimental import pallas as pl
from jax.experimental.pallas import tpu as pltpu

F32 = jnp.float32
BF16 = jnp.bfloat16

D_MODEL = 1024
HEAD_DIM = 64
CHUNK = 64
RMS_EPS = 1e-6
NEG_INF = -1e30

SWA_Q_HEADS = 8
SWA_GROUP = 4
SWA_WIDTH = 512
SWA_KV_WIDTH = 128
SB_HEADS = 8
SB_WIDTH = 512
GATE_WIDTH = 2 * D_MODEL
IN_WIDTH = SWA_WIDTH + 2 * SWA_KV_WIDTH + 3 * SB_WIDTH + GATE_WIDTH

PEER_HEADS = 8
PEER_KEYS = 128
PEER_HALF = 128
PEER_TOPK = 16
PEER_EXPERTS = PEER_KEYS * PEER_KEYS

LANES = 128
SUBLANES = 8
BLK = 128

INPROJ_TM = 512
MERGE_TM = 256
SELECT_TT = 512
SELECT_LANES = 256
EXPERT_TT = 512
EXPERT_EB = 512
EXPERT_JCHUNK = 32
SB_QROWS = 512
VMEM_LIMIT = 56 * 1024 * 1024

_NT = (((1,), (1,)), ((), ()))


def _cparams(sem):
    return pltpu.CompilerParams(dimension_semantics=sem, vmem_limit_bytes=VMEM_LIMIT)


def _inproj_kernel(x_ref, g_ref, w_ref, qa_ref, ka_ref, va_ref, qb_ref, kb_ref, vb_ref, gl_ref):
    x = x_ref[...]
    ms = jnp.mean(x * x, axis=-1, keepdims=True)
    h = (x * lax.rsqrt(ms + RMS_EPS) * g_ref[...]).astype(BF16)

    def proj(lo, hi):
        return jnp.dot(h, w_ref[:, lo:hi], preferred_element_type=F32)

    o = 0
    qa_ref[...] = proj(o, o + SWA_WIDTH); o += SWA_WIDTH
    ka_ref[...] = proj(o, o + SWA_KV_WIDTH); o += SWA_KV_WIDTH
    va_ref[...] = proj(o, o + SWA_KV_WIDTH).astype(BF16); o += SWA_KV_WIDTH
    qb_ref[...] = (proj(o, o + SB_WIDTH) * (HEAD_DIM ** -0.5)).astype(BF16); o += SB_WIDTH
    kb_ref[...] = proj(o, o + SB_WIDTH).astype(BF16); o += SB_WIDTH
    vb_ref[...] = proj(o, o + SB_WIDTH).astype(BF16); o += SB_WIDTH
    gl_ref[...] = proj(o, o + GATE_WIDTH)


def _inproj(x2, gain, w_in):
    t = x2.shape[0]
    tm = INPROJ_TM
    row = lambda w: pl.BlockSpec((tm, w), lambda i: (i, 0))
    full = lambda a: pl.BlockSpec(a.shape, lambda i: (0,) * a.ndim)
    widths = (SWA_WIDTH, SWA_KV_WIDTH, SWA_KV_WIDTH, SB_WIDTH, SB_WIDTH, SB_WIDTH, GATE_WIDTH)
    dtypes = (F32, F32, BF16, BF16, BF16, BF16, F32)
    return pl.pallas_call(
        _inproj_kernel,
        grid=(t // tm,),
        in_specs=[row(D_MODEL), full(gain), full(w_in)],
        out_specs=[row(w) for w in widths],
        out_shape=[jax.ShapeDtypeStruct((t, w), dt) for w, dt in zip(widths, dtypes)],
        compiler_params=_cparams(("parallel",)),
        name="inproj",
    )(x2, gain, w_in)


def _half_rms(x, lo):
    sq = x * x
    s_lo = jnp.sum(jnp.where(lo, sq, 0.0), axis=-1, keepdims=True)
    s_hi = jnp.sum(jnp.where(lo, 0.0, sq), axis=-1, keepdims=True)
    inv = jnp.where(lo, lax.rsqrt(s_lo / HEAD_DIM + RMS_EPS), lax.rsqrt(s_hi / HEAD_DIM + RMS_EPS))
    return x * inv


def _swa_kernel(sink_ref, q_ref, k_ref, v_ref, qg_ref, kg_ref, o_ref,
                kl0, kh0, kl1, kh1, vl0, vh0, vl1, vh1):
    p = pl.program_id(1)
    lo = lax.broadcasted_iota(jnp.int32, (BLK, LANES), 1) < HEAD_DIM
    kv_scratch = (kl0, kh0, kl1, kh1, vl0, vh0, vl1, vh1)

    @pl.when(p == 0)
    def _():
        for r in kv_scratch:
            r[0:BLK, :] = jnp.zeros((BLK, LANES), BF16)

    row = pl.multiple_of((p + 1) * BLK, BLK)
    kn = _half_rms(k_ref[...], lo) * kg_ref[...]
    kr = pltpu.roll(kn, HEAD_DIM, 1)
    v = v_ref[...].astype(F32)
    vr = pltpu.roll(v, HEAD_DIM, 1)
    kl0[pl.ds(row, BLK), :] = jnp.where(lo, kn, 0.0).astype(BF16)
    kh0[pl.ds(row, BLK), :] = jnp.where(lo, 0.0, kr).astype(BF16)
    kl1[pl.ds(row, BLK), :] = jnp.where(lo, kr, 0.0).astype(BF16)
    kh1[pl.ds(row, BLK), :] = jnp.where(lo, 0.0, kn).astype(BF16)
    vl0[pl.ds(row, BLK), :] = jnp.where(lo, v, 0.0).astype(BF16)
    vh0[pl.ds(row, BLK), :] = jnp.where(lo, 0.0, vr).astype(BF16)
    vl1[pl.ds(row, BLK), :] = jnp.where(lo, vr, 0.0).astype(BF16)
    vh1[pl.ds(row, BLK), :] = jnp.where(lo, 0.0, v).astype(BF16)

    qi = lax.broadcasted_iota(jnp.int32, (BLK, 2 * BLK), 0)
    kj = lax.broadcasted_iota(jnp.int32, (BLK, 2 * BLK), 1)
    dist = jnp.abs(qi + BLK - kj).astype(F32)
    qc = qi // CHUNK
    kc = kj // CHUNK
    vis = (kc >= qc) & (kc <= qc + 2) & ((kj >= BLK) | (p > 0))
    win = pl.ds(pl.multiple_of(p * BLK, BLK), 2 * BLK)

    for m in range(SWA_Q_HEADS // 2):
        g = (2 * m) // SWA_GROUP
        kv = ((kl0, vl0), (kh0, vh0)) if g == 0 else ((kl1, vl1), (kh1, vh1))
        qn = _half_rms(q_ref[:, m * LANES:(m + 1) * LANES], lo) * qg_ref[...]
        qs = (qn * (HEAD_DIM ** -0.5)).astype(BF16)
        acc = jnp.zeros((BLK, LANES), F32)
        for which, (kref, vref) in enumerate(kv):
            h = 2 * m + which
            slope = 2.0 ** (-(h + 1))
            logits = lax.dot_general(qs, kref[win, :], _NT, preferred_element_type=F32)
            logits = jnp.where(vis, logits - slope * dist, NEG_INF)
            sink = sink_ref[h]
            mx = jnp.maximum(jnp.max(logits, axis=-1, keepdims=True), sink)
            pe = jnp.exp(logits - mx)
            den = jnp.sum(pe, axis=-1, keepdims=True) + jnp.exp(sink - mx)
            probs = (pe / den).astype(BF16)
            acc = acc + jnp.dot(probs, vref[win, :], preferred_element_type=F32)
        o_ref[:, m * LANES:(m + 1) * LANES] = acc.astype(BF16)


def _swa(qa, ka, va, qgain, kgain, sinks, batch, seq):
    nb = seq // BLK
    qg2 = jnp.concatenate([qgain, qgain]).reshape(1, LANES)
    kg2 = jnp.concatenate([kgain, kgain]).reshape(1, LANES)
    blk = lambda w: pl.BlockSpec((BLK, w), lambda b, p: (b * nb + p, 0))
    one = pl.BlockSpec((1, LANES), lambda b, p: (0, 0))
    return pl.pallas_call(
        _swa_kernel,
        grid=(batch, nb),
        in_specs=[pl.BlockSpec(memory_space=pltpu.SMEM), blk(SWA_WIDTH), blk(SWA_KV_WIDTH),
                  blk(SWA_KV_WIDTH), one, one],
        out_specs=blk(SWA_WIDTH),
        out_shape=jax.ShapeDtypeStruct((batch * seq, SWA_WIDTH), BF16),
        scratch_shapes=[pltpu.VMEM((seq + BLK, LANES), BF16)] * 8,
        compiler_params=_cparams(("parallel", "arbitrary")),
        name="swa",
    )(sinks, qa, ka, va, qg2, kg2)


def _sb_kernel(q_ref, k_ref, v_ref, tri_ref, o_ref, kk, vv, carry, acc):
    seq = q_ref.shape[0]
    nb = seq // BLK
    per = SB_QROWS // BLK
    lo = lax.broadcasted_iota(jnp.int32, (nb, BLK, LANES), 2) < HEAD_DIM
    k3 = k_ref[...].reshape(nb, BLK, LANES)
    v3 = v_ref[...].reshape(nb, BLK, LANES)
    zero = jnp.zeros_like(k3)
    kk[:, 0:BLK, :] = jnp.where(lo, k3, zero)
    kk[:, BLK:2 * BLK, :] = jnp.where(lo, zero, k3)
    vv[:, 0:BLK, :] = jnp.where(lo, v3, zero)
    vv[:, BLK:2 * BLK, :] = jnp.where(lo, zero, v3)

    def tile(qs, kj, d):
        r0 = 0 if d is None else d * BLK
        m = SB_QROWS - r0
        q = q_ref[pl.ds(pl.multiple_of(qs * SB_QROWS + r0, BLK), m), :]
        z2 = lax.dot_general(q, kk[kj], _NT, preferred_element_type=F32)
        if d is not None:
            before = (lax.broadcasted_iota(jnp.int32, (m, BLK), 1)
                      < lax.broadcasted_iota(jnp.int32, (m, BLK), 0))
        ws = []
        for hh in range(2):
            z = z2[:, hh * BLK:(hh + 1) * BLK]
            log_keep = -(jnp.maximum(z, 0.0) + jnp.log(1.0 + jnp.exp(-jnp.abs(z))))
            log_beta = log_keep + z
            if d is not None:
                log_keep = jnp.where(before, log_keep, 0.0)
            hi = log_keep.astype(BF16)
            lw = (log_keep - hi.astype(F32)).astype(BF16)
            sc = jnp.dot(jnp.concatenate([hi, lw], axis=1), tri_ref[...], preferred_element_type=F32)
            suffix = sc[:, :BLK] + carry[hh, r0:, :]
            carry[hh, r0:, :] = carry[hh, r0:, :] + sc[:, BLK:]
            w = jnp.exp(log_beta + suffix)
            if d is not None:
                w = jnp.where(before, w, 0.0)
            ws.append(w.astype(BF16))
        acc[r0:, :] = acc[r0:, :] + jnp.dot(jnp.concatenate(ws, axis=1), vv[kj],
                                           preferred_element_type=F32)

    def qblock(qs, c):
        carry[...] = jnp.zeros(carry.shape, F32)
        acc[...] = jnp.zeros(acc.shape, F32)
        for d in reversed(range(per)):
            tile(qs, qs * per + d, d)

        def inner(t, c2):
            tile(qs, qs * per - 1 - t, None)
            return c2

        lax.fori_loop(0, qs * per, inner, 0)
        o_ref[pl.ds(pl.multiple_of(qs * SB_QROWS, BLK), SB_QROWS), :] = acc[...].astype(BF16)
        return c

    lax.fori_loop(0, seq // SB_QROWS, qblock, 0)


def _suffix_matrix():
    j = jnp.arange(BLK)[:, None]
    s = jnp.arange(BLK)[None, :]
    strict = (j > s).astype(BF16)
    half = jnp.concatenate([strict, jnp.ones((BLK, BLK), BF16)], axis=1)
    return jnp.concatenate([half, half], axis=0)


def _stickbreak(qb, kb, vb, batch, seq):
    npair = SB_HEADS // 2
    blk = pl.BlockSpec((seq, LANES), lambda b, m: (b, m))
    tri = _suffix_matrix()
    return pl.pallas_call(
        _sb_kernel,
        grid=(batch, npair),
        in_specs=[blk, blk, blk, pl.BlockSpec(tri.shape, lambda b, m: (0, 0))],
        out_specs=blk,
        out_shape=jax.ShapeDtypeStruct((batch * seq, SB_WIDTH), BF16),
        scratch_shapes=[pltpu.VMEM((seq // BLK, 2 * BLK, LANES), BF16)] * 2
        + [pltpu.VMEM((2, SB_QROWS, BLK), F32), pltpu.VMEM((SB_QROWS, LANES), F32)],
        compiler_params=_cparams(("parallel", "parallel")),
        name="stickbreak",
    )(qb, kb, vb, tri)


def _merge_kernel(ya_ref, yb_ref, gl_ref, bias_ref, x_ref, wa_ref, wb_ref, wo_ref, g2_ref, wq_ref,
                  x1_ref, h2_ref, qt_ref):
    ua = jnp.dot(ya_ref[...], wa_ref[...], preferred_element_type=F32)
    ub = jnp.dot(yb_ref[...], wb_ref[...], preferred_element_type=F32)
    gates = jax.nn.sigmoid(gl_ref[...] + bias_ref[...])
    merged = gates[:, :D_MODEL] * ua + gates[:, D_MODEL:] * ub
    x1 = x_ref[...] + jnp.dot(merged.astype(BF16), wo_ref[...], preferred_element_type=F32)
    x1_ref[...] = x1
    ms = jnp.mean(x1 * x1, axis=-1, keepdims=True)
    h2t = (x1 * lax.rsqrt(ms + RMS_EPS) * g2_ref[...]).T.astype(BF16)
    h2_ref[...] = h2t
    qt_ref[...] = jnp.dot(wq_ref[...], h2t, preferred_element_type=F32).astype(BF16)


def _merge(ya, yb, gl, bias, x2, wa, wb, wo, g2, wq_t):
    t = x2.shape[0]
    tm = MERGE_TM
    row = lambda w: pl.BlockSpec((tm, w), lambda i: (i, 0))
    full = lambda a: pl.BlockSpec(a.shape, lambda i: (0,) * a.ndim)
    nq = wq_t.shape[0]
    per = EXPERT_TT // tm
    return pl.pallas_call(
        _merge_kernel,
        grid=(t // tm,),
        in_specs=[row(SWA_WIDTH), row(SB_WIDTH), row(GATE_WIDTH), full(bias), row(D_MODEL),
                  full(wa), full(wb), full(wo), full(g2), full(wq_t)],
        out_specs=[row(D_MODEL),
                   pl.BlockSpec((None, D_MODEL, tm), lambda i: (i // per, 0, i % per)),
                   pl.BlockSpec((nq, tm), lambda i: (0, i))],
        out_shape=[jax.ShapeDtypeStruct((t, D_MODEL), F32),
                   jax.ShapeDtypeStruct((t // EXPERT_TT, D_MODEL, EXPERT_TT), BF16),
                   jax.ShapeDtypeStruct((nq, t), BF16)],
        compiler_params=_cparams(("parallel",)),
        name="merge",
    )(ya, yb, gl, bias, x2, wa, wb, wo, g2, wq_t)


_CAND_GROUPS = ((0, 16), (1, 8), (2, 5), (3, 4), (4, 3), (5, 2), (6, 2), (7, 2))
_CAND_ROWS = 16 + 7 * SUBLANES + SUBLANES


def _top16(s, exact_ties):
    n, tt = s.shape
    rank = jnp.full((n, tt), float(PEER_TOPK), F32)
    if exact_ties:
        pos = lax.broadcasted_iota(jnp.int32, (n, tt), 0).astype(F32)
    vals = []
    for r in range(PEER_TOPK):
        m = jnp.max(s, axis=0, keepdims=True)
        hit = s == m
        if exact_ties:
            hit = pos == jnp.min(jnp.where(hit, pos, float(n)), axis=0, keepdims=True)
        rank = jnp.where(hit, float(r), rank)
        s = jnp.where(hit, -jnp.inf, s)
        vals.append(m)
    return vals, rank


def _select_lanes(s0, s1, exact_ties):
    tt = s0.shape[1]
    v0, rank0 = _top16(s0, exact_ties)
    v1, rank1 = _top16(s1, exact_ties)

    row8 = lax.broadcasted_iota(jnp.int32, (SUBLANES, tt), 0)
    stack8 = lambda rows: functools.reduce(
        lambda acc, kv: jnp.where(row8 == kv[0], kv[1], acc), enumerate(rows), jnp.zeros((SUBLANES, tt), F32))
    v1_lo, v1_hi = stack8(v1[:8]), stack8(v1[8:])
    v0_hi = stack8(v0[8:])

    pieces = []
    for r0, n in _CAND_GROUPS:
        lo_piece = v0[r0] + v1_lo
        pieces.append(lo_piece if n >= SUBLANES else jnp.where(row8 < n, lo_piece, -jnp.inf))
        if n > SUBLANES:
            pieces.append(v0[r0] + v1_hi)
    pieces.append(v0_hi + v1[0])
    cand = jnp.concatenate(pieces, axis=0)

    if exact_ties:
        pos = lax.broadcasted_iota(jnp.int32, cand.shape, 0).astype(F32)
    chosen = jnp.zeros(cand.shape, F32)
    c = cand
    for _ in range(PEER_TOPK):
        m = jnp.max(c, axis=0, keepdims=True)
        hit = c == m
        if exact_ties:
            hit = pos == jnp.min(jnp.where(hit, pos, float(_CAND_ROWS)), axis=0, keepdims=True)
        chosen = jnp.where(hit, 1.0, chosen)
        c = jnp.where(hit, -jnp.inf, c)

    best = v0[0] + v1[0]
    z = jnp.sum(jnp.where(chosen > 0.0, jnp.exp(cand - best), 0.0), axis=0, keepdims=True)

    counts = []
    off = 0
    for r0, n in _CAND_GROUPS:
        rows = SUBLANES * ((n + SUBLANES - 1) // SUBLANES)
        counts.append(jnp.sum(chosen[off:off + rows], axis=0, keepdims=True))
        off += rows
    for r in range(SUBLANES):
        counts.append(chosen[off + r:off + r + 1])

    c0 = jnp.zeros(s0.shape, F32)
    for r in range(PEER_TOPK):
        c0 = jnp.where(rank0 == float(r), counts[r], c0)

    taken = lambda rk: jnp.sum(jnp.where(rk < float(PEER_TOPK), 1.0, 0.0), axis=0, keepdims=True)
    unique = ((taken(rank0) == float(PEER_TOPK)) & (taken(rank1) == float(PEER_TOPK))
              & (jnp.sum(chosen, axis=0, keepdims=True) == float(PEER_TOPK)))
    return (jnp.exp(s0 - v0[0]) * (0.5 / z), c0, jnp.exp(s1 - v1[0]), rank1), unique


def _select_kernel(q_ref, sk_ref, a_ref, c_ref, b_ref, r_ref, s_ref):
    s_ref[0] = jnp.dot(sk_ref[0, 0], q_ref[0:PEER_HALF, :], preferred_element_type=F32)
    s_ref[1] = jnp.dot(sk_ref[0, 1], q_ref[PEER_HALF:2 * PEER_HALF, :], preferred_element_type=F32)

    def lane_tile(lt, carry):
        ls = pl.ds(pl.multiple_of(lt * SELECT_LANES, SELECT_LANES), SELECT_LANES)

        def run(exact_ties):
            outs, unique = _select_lanes(s_ref[0, :, ls], s_ref[1, :, ls], exact_ties)
            for ref, val in zip((a_ref, c_ref, b_ref, r_ref), outs):
                ref[0, :, ls] = val
            return unique

        unique = run(False)

        @pl.when(jnp.min(jnp.where(unique, 1.0, 0.0)) < 0.5)
        def _():
            run(True)

        return carry

    lax.fori_loop(0, q_ref.shape[1] // SELECT_LANES, lane_tile, 0)


def _peer_select(qt, sub_keys):
    t = qt.shape[1]
    tt = SELECT_TT
    out = pl.BlockSpec((None, 1, PEER_KEYS, tt), lambda i, h: (i, h, 0, 0))
    shp = lambda dt: jax.ShapeDtypeStruct((t // tt, PEER_HEADS, PEER_KEYS, tt), dt)
    return pl.pallas_call(
        _select_kernel,
        grid=(t // tt, PEER_HEADS),
        in_specs=[pl.BlockSpec((2 * PEER_HALF, tt), lambda i, h: (h, i)),
                  pl.BlockSpec((1, 2, PEER_KEYS, PEER_HALF), lambda i, h: (h, 0, 0, 0))],
        out_specs=[out] * 4,
        out_shape=[shp(F32)] * 4,
        scratch_shapes=[pltpu.VMEM((2, PEER_KEYS, tt), F32)],
        compiler_params=_cparams(("parallel", "parallel")),
        name="peer_select",
    )(qt, sub_keys)


def _experts_kernel(zero_ref, h_ref, a_ref, c_ref, b_ref, r_ref, u0_ref, un_ref, vt_ref, x1_ref, o_ref,
                    acc_ref, g_ref, abc_ref, stage_ref, hs_ref):
    e = pl.program_id(1)
    last = pl.num_programs(1) - 1
    eb = un_ref.shape[0]
    nsub = eb // PEER_KEYS
    tt = h_ref.shape[1]
    half = tt // 2
    lt_per_half = half // LANES
    nlt = 2 * lt_per_half
    nv = EXPERT_JCHUNK // SUBLANES
    njc = PEER_KEYS // EXPERT_JCHUNK
    plane = [zero_ref[k] + k for k in range(4)]
    zero_bits = jnp.full((SUBLANES, LANES), zero_ref[0], jnp.int32)

    def zero_bits_after(x):
        return pltpu.bitcast(x, jnp.int32) & zero_bits

    def gate_weights(lt, last_chunk_init=None):
        hf, ll = divmod(lt, lt_per_half)
        ls = slice(lt * LANES, (lt + 1) * LANES)
        for jc in range(njc):
            j0 = jc * EXPERT_JCHUNK
            init = jnp.zeros((SUBLANES, LANES), F32)
            if last_chunk_init is not None and jc == njc - 1:
                init = last_chunk_init
            w = [[init] * nv for _ in range(nsub)]
            for h in range(PEER_HEADS):
                bs = [b_ref[h, j0 + v * SUBLANES:j0 + (v + 1) * SUBLANES, ls] for v in range(nv)]
                rs = [r_ref[h, j0 + v * SUBLANES:j0 + (v + 1) * SUBLANES, ls] for v in range(nv)]
                for ii in range(nsub):
                    a8 = abc_ref[0, ii, h, :, ls]
                    c8 = abc_ref[1, ii, h, :, ls]
                    for v in range(nv):
                        w[ii][v] = w[ii][v] + jnp.where(rs[v] < c8, a8 * bs[v], 0.0)
            for ii in range(nsub):
                r0 = ii * PEER_KEYS + j0
                stage_ref[plane[2 + hf], r0:r0 + EXPERT_JCHUNK, ll * LANES:(ll + 1) * LANES] = (
                    jnp.concatenate(w[ii], axis=0))
        return w[0][0]

    def stage_block(u_ref, blk):
        for ii in range(nsub):
            i = blk * nsub + ii
            for h in range(PEER_HEADS):
                abc_ref[0, ii, h] = jnp.broadcast_to(a_ref[h, pl.ds(i, 1), :], (SUBLANES, tt))
                abc_ref[1, ii, h] = jnp.broadcast_to(c_ref[h, pl.ds(i, 1), :], (SUBLANES, tt))
        first = gate_weights(0)
        head = (slice(0, 2 * SUBLANES), slice(0, LANES))
        hs_ref[head] = pltpu.bitcast(
            pltpu.bitcast(hs_ref[head], jnp.int32) | zero_bits_after(first), BF16)
        for hf in range(2):
            hs = slice(hf * half, (hf + 1) * half)
            stage_ref[plane[hf]] = jnp.dot(u_ref[...], hs_ref[:, hs],
                                           preferred_element_type=F32)
        for lt in range(1, nlt - 1):
            gate_weights(lt)
        tail = stage_ref[plane[1], eb - SUBLANES:eb, half - LANES:half]
        gate_weights(nlt - 1, pltpu.bitcast(zero_bits_after(tail), F32))

    @pl.when(e == 0)
    def _():
        acc_ref[...] = jnp.zeros(acc_ref.shape, F32)
        hs_ref[...] = h_ref[...]
        stage_block(u0_ref, 0)

    for lt in range(nlt):
        hf, ll = divmod(lt, lt_per_half)
        x = stage_ref[plane[hf], :, ll * LANES:(ll + 1) * LANES]
        gelu = x * (1.0 + lax.erf(x * math.sqrt(0.5)))
        w = stage_ref[plane[2 + hf], :, ll * LANES:(ll + 1) * LANES]
        g_ref[:, lt * LANES:(lt + 1) * LANES] = (w * gelu).astype(BF16)
    for hf in range(2):
        hs = slice(hf * half, (hf + 1) * half)
        acc_ref[:, hs] += jnp.dot(vt_ref[...], g_ref[:, hs], preferred_element_type=F32)

    stage_block(un_ref, jnp.minimum(e + 1, last))

    @pl.when(e == last)
    def _():
        o_ref[...] = x1_ref[...] + acc_ref[...].T


def _peer_experts(h2t, a, c0, b, r1, u, vt, x1):
    tt, eb = EXPERT_TT, EXPERT_EB
    assert SELECT_TT == tt and h2t.shape[2] == tt
    t = h2t.shape[0] * tt
    nblk = PEER_EXPERTS // eb
    sel = pl.BlockSpec((None, PEER_HEADS, PEER_KEYS, tt), lambda i, e: (i, 0, 0, 0))
    row = pl.BlockSpec((tt, D_MODEL), lambda i, e: (i, 0))
    return pl.pallas_call(
        _experts_kernel,
        grid=(t // tt, nblk),
        in_specs=[pl.BlockSpec(memory_space=pltpu.SMEM),
                  pl.BlockSpec((None, D_MODEL, tt), lambda i, e: (i, 0, 0)), sel, sel, sel, sel,
                  pl.BlockSpec((eb, D_MODEL), lambda i, e: (0, 0)),
                  pl.BlockSpec((eb, D_MODEL), lambda i, e: (jnp.minimum(e + 1, nblk - 1), 0)),
                  pl.BlockSpec((None, D_MODEL, eb), lambda i, e: (e, 0, 0)),
                  row],
        out_specs=row,
        out_shape=jax.ShapeDtypeStruct((t, D_MODEL), F32),
        scratch_shapes=[pltpu.VMEM((D_MODEL, tt), F32), pltpu.VMEM((eb, tt), BF16),
                        pltpu.VMEM((2, eb // PEER_KEYS, PEER_HEADS, SUBLANES, tt), F32),
                        pltpu.VMEM((4, eb, tt // 2), F32), pltpu.VMEM((D_MODEL, tt), BF16)],
        compiler_params=_cparams(("parallel", "arbitrary")),
        name="peer_experts",
    )(jnp.zeros((4,), jnp.int32), h2t, a, c0, b, r1, u, u, vt, x1)


def _layer(x2, batch, seq, mix_gain, w_in, gate_bias, q_gain, k_gain, sinks,
           w_up_swa, w_up_sb, w_out, ffn_gain, w_q, sub_keys, u, v):
    qa, ka, va, qb, kb, vb, gl = _inproj(x2, mix_gain.reshape(1, -1), w_in.astype(BF16))
    ya = _swa(qa, ka, va, q_gain, k_gain, sinks, batch, seq)
    yb = _stickbreak(qb, kb, vb, batch, seq)
    x1, h2t, qt = _merge(ya, yb, gl, gate_bias.reshape(1, -1), x2,
                         w_up_swa.astype(BF16), w_up_sb.astype(BF16), w_out.astype(BF16),
                         ffn_gain.reshape(1, -1), w_q.T.astype(BF16))
    a, c0, b, r1 = _peer_select(qt, sub_keys.astype(BF16))
    vt = v.reshape(PEER_EXPERTS // EXPERT_EB, EXPERT_EB, D_MODEL).transpose(0, 2, 1).astype(BF16)
    return _peer_experts(h2t, a, c0, b, r1, u.astype(BF16), vt, x1)


def kernel(x, mix_norm_gain, w_in, gate_bias, swa_q_gain, swa_k_gain, swa_sinks, w_up_swa, w_up_sb,
           w_out, ffn_norm_gain, peer_w_q, peer_sub_keys, peer_u, peer_v):
    batch, seq, d = x.shape
    x2 = x.reshape(batch * seq, d)
    for layer in range(mix_norm_gain.shape[0]):
        x2 = _layer(x2, batch, seq, mix_norm_gain[layer], w_in[layer], gate_bias[layer],
                    swa_q_gain[layer], swa_k_gain[layer], swa_sinks[layer], w_up_swa[layer],
                    w_up_sb[layer], w_out[layer], ffn_norm_gain[layer], peer_w_q[layer],
                    peer_sub_keys[layer], peer_u[layer], peer_v[layer])
    return x2.reshape(batch, seq, d)
```

```python
import functools
import math

import jax
import jax.numpy as jnp
from jax import lax
from jax.experimental import pallas as pl
from jax.experimental.pallas import tpu as pltpu

F32 = jnp.float32
BF16 = jnp.bfloat16

D_MODEL = 1024
HEAD_DIM = 64
CHUNK = 64
RMS_EPS = 1e-6
NEG_INF = -1e30

SWA_Q_HEADS = 8
SWA_GROUP = 4
SWA_WIDTH = 512
SWA_KV_WIDTH = 128
SB_HEADS = 8
SB_WIDTH = 512
GATE_WIDTH = 2 * D_MODEL
IN_WIDTH = SWA_WIDTH + 2 * SWA_KV_WIDTH + 3 * SB_WIDTH + GATE_WIDTH

PEER_HEADS = 8
PEER_KEYS = 128
PEER_HALF = 128
PEER_TOPK = 16
PEER_EXPERTS = PEER_KEYS * PEER_KEYS

LANES = 128
SUBLANES = 8
BLK = 128

INPROJ_TM = 512
MERGE_TM = 256
SELECT_TT = 512
SELECT_LANES = 256
EXPERT_TT = 512
EXPERT_EB = 512
EXPERT_JCHUNK = 32
SB_QROWS = 1024
VMEM_LIMIT = 56 * 1024 * 1024

_NT = (((1,), (1,)), ((), ()))


def _cparams(sem):
    return pltpu.CompilerParams(dimension_semantics=sem, vmem_limit_bytes=VMEM_LIMIT)


def _inproj_kernel(x_ref, g_ref, w_ref, qa_ref, ka_ref, va_ref, qb_ref, kb_ref, vb_ref, gl_ref):
    x = x_ref[...]
    ms = jnp.mean(x * x, axis=-1, keepdims=True)
    h = (x * lax.rsqrt(ms + RMS_EPS) * g_ref[...]).astype(BF16)

    def proj(lo, hi):
        return jnp.dot(h, w_ref[:, lo:hi], preferred_element_type=F32)

    o = 0
    qa_ref[...] = proj(o, o + SWA_WIDTH); o += SWA_WIDTH
    ka_ref[...] = proj(o, o + SWA_KV_WIDTH); o += SWA_KV_WIDTH
    va_ref[...] = proj(o, o + SWA_KV_WIDTH).astype(BF16); o += SWA_KV_WIDTH
    qb_ref[...] = (proj(o, o + SB_WIDTH) * (HEAD_DIM ** -0.5)).astype(BF16); o += SB_WIDTH
    kb_ref[...] = proj(o, o + SB_WIDTH).astype(BF16); o += SB_WIDTH
    vb_ref[...] = proj(o, o + SB_WIDTH).astype(BF16); o += SB_WIDTH
    gl_ref[...] = proj(o, o + GATE_WIDTH)


def _inproj(x2, gain, w_in):
    t = x2.shape[0]
    tm = INPROJ_TM
    row = lambda w: pl.BlockSpec((tm, w), lambda i: (i, 0))
    full = lambda a: pl.BlockSpec(a.shape, lambda i: (0,) * a.ndim)
    widths = (SWA_WIDTH, SWA_KV_WIDTH, SWA_KV_WIDTH, SB_WIDTH, SB_WIDTH, SB_WIDTH, GATE_WIDTH)
    dtypes = (F32, F32, BF16, BF16, BF16, BF16, F32)
    return pl.pallas_call(
        _inproj_kernel,
        grid=(t // tm,),
        in_specs=[row(D_MODEL), full(gain), full(w_in)],
        out_specs=[row(w) for w in widths],
        out_shape=[jax.ShapeDtypeStruct((t, w), dt) for w, dt in zip(widths, dtypes)],
        compiler_params=_cparams(("parallel",)),
        name="inproj",
    )(x2, gain, w_in)


def _half_rms(x, lo):
    sq = x * x
    s_lo = jnp.sum(jnp.where(lo, sq, 0.0), axis=-1, keepdims=True)
    s_hi = jnp.sum(jnp.where(lo, 0.0, sq), axis=-1, keepdims=True)
    inv = jnp.where(lo, lax.rsqrt(s_lo / HEAD_DIM + RMS_EPS), lax.rsqrt(s_hi / HEAD_DIM + RMS_EPS))
    return x * inv


def _swa_kernel(sink_ref, q_ref, k_ref, v_ref, qg_ref, kg_ref, o_ref,
                kl0, kh0, kl1, kh1, vl0, vh0, vl1, vh1):
    p = pl.program_id(1)
    lo = lax.broadcasted_iota(jnp.int32, (BLK, LANES), 1) < HEAD_DIM
    kv_scratch = (kl0, kh0, kl1, kh1, vl0, vh0, vl1, vh1)

    @pl.when(p == 0)
    def _():
        for r in kv_scratch:
            r[0:BLK, :] = jnp.zeros((BLK, LANES), BF16)

    row = pl.multiple_of((p + 1) * BLK, BLK)
    kn = _half_rms(k_ref[...], lo) * kg_ref[...]
    kr = pltpu.roll(kn, HEAD_DIM, 1)
    v = v_ref[...].astype(F32)
    vr = pltpu.roll(v, HEAD_DIM, 1)
    kl0[pl.ds(row, BLK), :] = jnp.where(lo, kn, 0.0).astype(BF16)
    kh0[pl.ds(row, BLK), :] = jnp.where(lo, 0.0, kr).astype(BF16)
    kl1[pl.ds(row, BLK), :] = jnp.where(lo, kr, 0.0).astype(BF16)
    kh1[pl.ds(row, BLK), :] = jnp.where(lo, 0.0, kn).astype(BF16)
    vl0[pl.ds(row, BLK), :] = jnp.where(lo, v, 0.0).astype(BF16)
    vh0[pl.ds(row, BLK), :] = jnp.where(lo, 0.0, vr).astype(BF16)
    vl1[pl.ds(row, BLK), :] = jnp.where(lo, vr, 0.0).astype(BF16)
    vh1[pl.ds(row, BLK), :] = jnp.where(lo, 0.0, v).astype(BF16)

    qi = lax.broadcasted_iota(jnp.int32, (BLK, 2 * BLK), 0)
    kj = lax.broadcasted_iota(jnp.int32, (BLK, 2 * BLK), 1)
    dist = jnp.abs(qi + BLK - kj).astype(F32)
    qc = qi // CHUNK
    kc = kj // CHUNK
    vis = (kc >= qc) & (kc <= qc + 2) & ((kj >= BLK) | (p > 0))
    win = pl.ds(pl.multiple_of(p * BLK, BLK), 2 * BLK)

    for m in range(SWA_Q_HEADS // 2):
        g = (2 * m) // SWA_GROUP
        kv = ((kl0, vl0), (kh0, vh0)) if g == 0 else ((kl1, vl1), (kh1, vh1))
        qn = _half_rms(q_ref[:, m * LANES:(m + 1) * LANES], lo) * qg_ref[...]
        qs = (qn * (HEAD_DIM ** -0.5)).astype(BF16)
        acc = jnp.zeros((BLK, LANES), F32)
        for which, (kref, vref) in enumerate(kv):
            h = 2 * m + which
            slope = 2.0 ** (-(h + 1))
            logits = lax.dot_general(qs, kref[win, :], _NT, preferred_element_type=F32)
            logits = jnp.where(vis, logits - slope * dist, NEG_INF)
            sink = sink_ref[h]
            mx = jnp.maximum(jnp.max(logits, axis=-1, keepdims=True), sink)
            pe = jnp.exp(logits - mx)
            den = jnp.sum(pe, axis=-1, keepdims=True) + jnp.exp(sink - mx)
            probs = (pe / den).astype(BF16)
            acc = acc + jnp.dot(probs, vref[win, :], preferred_element_type=F32)
        o_ref[:, m * LANES:(m + 1) * LANES] = acc.astype(BF16)


def _swa(qa, ka, va, qgain, kgain, sinks, batch, seq):
    nb = seq // BLK
    qg2 = jnp.concatenate([qgain, qgain]).reshape(1, LANES)
    kg2 = jnp.concatenate([kgain, kgain]).reshape(1, LANES)
    blk = lambda w: pl.BlockSpec((BLK, w), lambda b, p: (b * nb + p, 0))
    one = pl.BlockSpec((1, LANES), lambda b, p: (0, 0))
    return pl.pallas_call(
        _swa_kernel,
        grid=(batch, nb),
        in_specs=[pl.BlockSpec(memory_space=pltpu.SMEM), blk(SWA_WIDTH), blk(SWA_KV_WIDTH),
                  blk(SWA_KV_WIDTH), one, one],
        out_specs=blk(SWA_WIDTH),
        out_shape=jax.ShapeDtypeStruct((batch * seq, SWA_WIDTH), BF16),
        scratch_shapes=[pltpu.VMEM((seq + BLK, LANES), BF16)] * 8,
        compiler_params=_cparams(("parallel", "arbitrary")),
        name="swa",
    )(sinks, qa, ka, va, qg2, kg2)


def _sb_kernel(q_ref, k_ref, v_ref, tri_ref, o_ref, kk, vv, carry, acc):
    seq = q_ref.shape[0]
    nb = seq // BLK
    per = SB_QROWS // BLK
    lo = lax.broadcasted_iota(jnp.int32, (nb, BLK, LANES), 2) < HEAD_DIM
    k3 = k_ref[...].reshape(nb, BLK, LANES)
    v3 = v_ref[...].reshape(nb, BLK, LANES)
    zero = jnp.zeros_like(k3)
    kk[:, 0:BLK, :] = jnp.where(lo, k3, zero)
    kk[:, BLK:2 * BLK, :] = jnp.where(lo, zero, k3)
    vv[:, 0:BLK, :] = jnp.where(lo, v3, zero)
    vv[:, BLK:2 * BLK, :] = jnp.where(lo, zero, v3)

    def tile(qs, kj, d):
        r0 = 0 if d is None else d * BLK
        m = SB_QROWS - r0
        q = q_ref[pl.ds(pl.multiple_of(qs * SB_QROWS + r0, BLK), m), :]
        z2 = lax.dot_general(q, kk[kj], _NT, preferred_element_type=F32)
        if d is not None:
            before = (lax.broadcasted_iota(jnp.int32, (m, BLK), 1)
                      < lax.broadcasted_iota(jnp.int32, (m, BLK), 0))
        ws = []
        for hh in range(2):
            z = z2[:, hh * BLK:(hh + 1) * BLK]
            log_keep = -(jnp.maximum(z, 0.0) + jnp.log(1.0 + jnp.exp(-jnp.abs(z))))
            log_beta = log_keep + z
            if d is not None:
                log_keep = jnp.where(before, log_keep, 0.0)
            hi = log_keep.astype(BF16)
            lw = (log_keep - hi.astype(F32)).astype(BF16)
            sc = jnp.dot(jnp.concatenate([hi, lw], axis=1), tri_ref[...], preferred_element_type=F32)
            suffix = sc[:, :BLK] + carry[hh, r0:, :]
            carry[hh, r0:, :] = carry[hh, r0:, :] + sc[:, BLK:]
            w = jnp.exp(log_beta + suffix)
            if d is not None:
                w = jnp.where(before, w, 0.0)
            ws.append(w.astype(BF16))
        acc[r0:, :] = acc[r0:, :] + jnp.dot(jnp.concatenate(ws, axis=1), vv[kj],
                                           preferred_element_type=F32)

    def qblock(qs, c):
        carry[...] = jnp.zeros(carry.shape, F32)
        acc[...] = jnp.zeros(acc.shape, F32)
        for d in reversed(range(per)):
            tile(qs, qs * per + d, d)

        def inner(t, c2):
            tile(qs, qs * per - 1 - t, None)
            return c2

        lax.fori_loop(0, qs * per, inner, 0)
        o_ref[pl.ds(pl.multiple_of(qs * SB_QROWS, BLK), SB_QROWS), :] = acc[...].astype(BF16)
        return c

    lax.fori_loop(0, seq // SB_QROWS, qblock, 0)


def _suffix_matrix():
    j = jnp.arange(BLK)[:, None]
    s = jnp.arange(BLK)[None, :]
    strict = (j > s).astype(BF16)
    half = jnp.concatenate([strict, jnp.ones((BLK, BLK), BF16)], axis=1)
    return jnp.concatenate([half, half], axis=0)


def _stickbreak(qb, kb, vb, batch, seq):
    npair = SB_HEADS // 2
    blk = pl.BlockSpec((seq, LANES), lambda b, m: (b, m))
    tri = _suffix_matrix()
    return pl.pallas_call(
        _sb_kernel,
        grid=(batch, npair),
        in_specs=[blk, blk, blk, pl.BlockSpec(tri.shape, lambda b, m: (0, 0))],
        out_specs=blk,
        out_shape=jax.ShapeDtypeStruct((batch * seq, SB_WIDTH), BF16),
        scratch_shapes=[pltpu.VMEM((seq // BLK, 2 * BLK, LANES), BF16)] * 2
        + [pltpu.VMEM((2, SB_QROWS, BLK), F32), pltpu.VMEM((SB_QROWS, LANES), F32)],
        compiler_params=_cparams(("parallel", "parallel")),
        name="stickbreak",
    )(qb, kb, vb, tri)


def _merge_kernel(ya_ref, yb_ref, gl_ref, bias_ref, x_ref, wa_ref, wb_ref, wo_ref, g2_ref, wq_ref,
                  x1_ref, h2_ref, qt_ref):
    ua = jnp.dot(ya_ref[...], wa_ref[...], preferred_element_type=F32)
    ub = jnp.dot(yb_ref[...], wb_ref[...], preferred_element_type=F32)
    gates = jax.nn.sigmoid(gl_ref[...] + bias_ref[...])
    merged = gates[:, :D_MODEL] * ua + gates[:, D_MODEL:] * ub
    x1 = x_ref[...] + jnp.dot(merged.astype(BF16), wo_ref[...], preferred_element_type=F32)
    x1_ref[...] = x1
    ms = jnp.mean(x1 * x1, axis=-1, keepdims=True)
    h2t = (x1 * lax.rsqrt(ms + RMS_EPS) * g2_ref[...]).T.astype(BF16)
    h2_ref[...] = h2t
    qt_ref[...] = jnp.dot(wq_ref[...], h2t, preferred_element_type=F32).astype(BF16)


def _merge(ya, yb, gl, bias, x2, wa, wb, wo, g2, wq_t):
    t = x2.shape[0]
    tm = MERGE_TM
    row = lambda w: pl.BlockSpec((tm, w), lambda i: (i, 0))
    full = lambda a: pl.BlockSpec(a.shape, lambda i: (0,) * a.ndim)
    nq = wq_t.shape[0]
    per = EXPERT_TT // tm
    return pl.pallas_call(
        _merge_kernel,
        grid=(t // tm,),
        in_specs=[row(SWA_WIDTH), row(SB_WIDTH), row(GATE_WIDTH), full(bias), row(D_MODEL),
                  full(wa), full(wb), full(wo), full(g2), full(wq_t)],
        out_specs=[row(D_MODEL),
                   pl.BlockSpec((None, D_MODEL, tm), lambda i: (i // per, 0, i % per)),
                   pl.BlockSpec((nq, tm), lambda i: (0, i))],
        out_shape=[jax.ShapeDtypeStruct((t, D_MODEL), F32),
                   jax.ShapeDtypeStruct((t // EXPERT_TT, D_MODEL, EXPERT_TT), BF16),
                   jax.ShapeDtypeStruct((nq, t), BF16)],
        compiler_params=_cparams(("parallel",)),
        name="merge",
    )(ya, yb, gl, bias, x2, wa, wb, wo, g2, wq_t)


_CAND_GROUPS = ((0, 16), (1, 8), (2, 5), (3, 4), (4, 3), (5, 2), (6, 2), (7, 2))
_CAND_ROWS = 16 + 7 * SUBLANES + SUBLANES


def _top16(s, exact_ties):
    n, tt = s.shape
    rank = jnp.full((n, tt), float(PEER_TOPK), F32)
    if exact_ties:
        pos = lax.broadcasted_iota(jnp.int32, (n, tt), 0).astype(F32)
    vals = []
    for r in range(PEER_TOPK):
        m = jnp.max(s, axis=0, keepdims=True)
        hit = s == m
        if exact_ties:
            hit = pos == jnp.min(jnp.where(hit, pos, float(n)), axis=0, keepdims=True)
        rank = jnp.where(hit, float(r), rank)
        s = jnp.where(hit, -jnp.inf, s)
        vals.append(m)
    return vals, rank


def _select_lanes(s0, s1, exact_ties):
    tt = s0.shape[1]
    v0, rank0 = _top16(s0, exact_ties)
    v1, rank1 = _top16(s1, exact_ties)

    row8 = lax.broadcasted_iota(jnp.int32, (SUBLANES, tt), 0)
    stack8 = lambda rows: functools.reduce(
        lambda acc, kv: jnp.where(row8 == kv[0], kv[1], acc), enumerate(rows), jnp.zeros((SUBLANES, tt), F32))
    v1_lo, v1_hi = stack8(v1[:8]), stack8(v1[8:])
    v0_hi = stack8(v0[8:])

    pieces = []
    for r0, n in _CAND_GROUPS:
        lo_piece = v0[r0] + v1_lo
        pieces.append(lo_piece if n >= SUBLANES else jnp.where(row8 < n, lo_piece, -jnp.inf))
        if n > SUBLANES:
            pieces.append(v0[r0] + v1_hi)
    pieces.append(v0_hi + v1[0])
    cand = jnp.concatenate(pieces, axis=0)

    if exact_ties:
        pos = lax.broadcasted_iota(jnp.int32, cand.shape, 0).astype(F32)
    chosen = jnp.zeros(cand.shape, F32)
    c = cand
    for _ in range(PEER_TOPK):
        m = jnp.max(c, axis=0, keepdims=True)
        hit = c == m
        if exact_ties:
            hit = pos == jnp.min(jnp.where(hit, pos, float(_CAND_ROWS)), axis=0, keepdims=True)
        chosen = jnp.where(hit, 1.0, chosen)
        c = jnp.where(hit, -jnp.inf, c)

    best = v0[0] + v1[0]
    z = jnp.sum(jnp.where(chosen > 0.0, jnp.exp(cand - best), 0.0), axis=0, keepdims=True)

    counts = []
    off = 0
    for r0, n in _CAND_GROUPS:
        rows = SUBLANES * ((n + SUBLANES - 1) // SUBLANES)
        counts.append(jnp.sum(chosen[off:off + rows], axis=0, keepdims=True))
        off += rows
    for r in range(SUBLANES):
        counts.append(chosen[off + r:off + r + 1])

    c0 = jnp.zeros(s0.shape, F32)
    for r in range(PEER_TOPK):
        c0 = jnp.where(rank0 == float(r), counts[r], c0)

    taken = lambda rk: jnp.sum(jnp.where(rk < float(PEER_TOPK), 1.0, 0.0), axis=0, keepdims=True)
    unique = ((taken(rank0) == float(PEER_TOPK)) & (taken(rank1) == float(PEER_TOPK))
              & (jnp.sum(chosen, axis=0, keepdims=True) == float(PEER_TOPK)))
    return (jnp.exp(s0 - v0[0]) * (0.5 / z), c0, jnp.exp(s1 - v1[0]), rank1), unique


def _select_kernel(q_ref, sk_ref, a_ref, c_ref, b_ref, r_ref, s_ref):
    s_ref[0] = jnp.dot(sk_ref[0, 0], q_ref[0:PEER_HALF, :], preferred_element_type=F32)
    s_ref[1] = jnp.dot(sk_ref[0, 1], q_ref[PEER_HALF:2 * PEER_HALF, :], preferred_element_type=F32)

    def lane_tile(lt, carry):
        ls = pl.ds(pl.multiple_of(lt * SELECT_LANES, SELECT_LANES), SELECT_LANES)

        def run(exact_ties):
            outs, unique = _select_lanes(s_ref[0, :, ls], s_ref[1, :, ls], exact_ties)
            for ref, val in zip((a_ref, c_ref, b_ref, r_ref), outs):
                ref[0, :, ls] = val
            return unique

        unique = run(False)

        @pl.when(jnp.min(jnp.where(unique, 1.0, 0.0)) < 0.5)
        def _():
            run(True)

        return carry

    lax.fori_loop(0, q_ref.shape[1] // SELECT_LANES, lane_tile, 0)


def _peer_select(qt, sub_keys):
    t = qt.shape[1]
    tt = SELECT_TT
    out = pl.BlockSpec((None, 1, PEER_KEYS, tt), lambda i, h: (i, h, 0, 0))
    shp = lambda dt: jax.ShapeDtypeStruct((t // tt, PEER_HEADS, PEER_KEYS, tt), dt)
    return pl.pallas_call(
        _select_kernel,
        grid=(t // tt, PEER_HEADS),
        in_specs=[pl.BlockSpec((2 * PEER_HALF, tt), lambda i, h: (h, i)),
                  pl.BlockSpec((1, 2, PEER_KEYS, PEER_HALF), lambda i, h: (h, 0, 0, 0))],
        out_specs=[out] * 4,
        out_shape=[shp(F32)] * 4,
        scratch_shapes=[pltpu.VMEM((2, PEER_KEYS, tt), F32)],
        compiler_params=_cparams(("parallel", "parallel")),
        name="peer_select",
    )(qt, sub_keys)


def _experts_kernel(zero_ref, h_ref, a_ref, c_ref, b_ref, r_ref, u0_ref, un_ref, vt_ref, x1_ref, o_ref,
                    acc_ref, g_ref, abc_ref, stage_ref, hs_ref):
    e = pl.program_id(1)
    last = pl.num_programs(1) - 1
    eb = un_ref.shape[0]
    nsub = eb // PEER_KEYS
    tt = h_ref.shape[1]
    half = tt // 2
    lt_per_half = half // LANES
    nlt = 2 * lt_per_half
    nv = EXPERT_JCHUNK // SUBLANES
    njc = PEER_KEYS // EXPERT_JCHUNK
    plane = [zero_ref[k] + k for k in range(4)]
    zero_bits = jnp.full((SUBLANES, LANES), zero_ref[0], jnp.int32)

    def zero_bits_after(x):
        return pltpu.bitcast(x, jnp.int32) & zero_bits

    def gate_weights(lt, last_chunk_init=None):
        hf, ll = divmod(lt, lt_per_half)
        ls = slice(lt * LANES, (lt + 1) * LANES)
        for jc in range(njc):
            j0 = jc * EXPERT_JCHUNK
            init = jnp.zeros((SUBLANES, LANES), F32)
            if last_chunk_init is not None and jc == njc - 1:
                init = last_chunk_init
            w = [[init] * nv for _ in range(nsub)]
            for h in range(PEER_HEADS):
                bs = [b_ref[h, j0 + v * SUBLANES:j0 + (v + 1) * SUBLANES, ls] for v in range(nv)]
                rs = [r_ref[h, j0 + v * SUBLANES:j0 + (v + 1) * SUBLANES, ls] for v in range(nv)]
                for ii in range(nsub):
                    a8 = abc_ref[0, ii, h, :, ls]
                    c8 = abc_ref[1, ii, h, :, ls]
                    for v in range(nv):
                        w[ii][v] = w[ii][v] + jnp.where(rs[v] < c8, a8 * bs[v], 0.0)
            for ii in range(nsub):
                r0 = ii * PEER_KEYS + j0
                stage_ref[plane[2 + hf], r0:r0 + EXPERT_JCHUNK, ll * LANES:(ll + 1) * LANES] = (
                    jnp.concatenate(w[ii], axis=0))
        return w[0][0]

    def stage_block(u_ref, blk):
        for ii in range(nsub):
            i = blk * nsub + ii
            for h in range(PEER_HEADS):
                abc_ref[0, ii, h] = jnp.broadcast_to(a_ref[h, pl.ds(i, 1), :], (SUBLANES, tt))
                abc_ref[1, ii, h] = jnp.broadcast_to(c_ref[h, pl.ds(i, 1), :], (SUBLANES, tt))
        first = gate_weights(0)
        head = (slice(0, 2 * SUBLANES), slice(0, LANES))
        hs_ref[head] = pltpu.bitcast(
            pltpu.bitcast(hs_ref[head], jnp.int32) | zero_bits_after(first), BF16)
        for hf in range(2):
            hs = slice(hf * half, (hf + 1) * half)
            stage_ref[plane[hf]] = jnp.dot(u_ref[...], hs_ref[:, hs],
                                           preferred_element_type=F32)
        for lt in range(1, nlt - 1):
            gate_weights(lt)
        tail = stage_ref[plane[1], eb - SUBLANES:eb, half - LANES:half]
        gate_weights(nlt - 1, pltpu.bitcast(zero_bits_after(tail), F32))

    @pl.when(e == 0)
    def _():
        acc_ref[...] = jnp.zeros(acc_ref.shape, F32)
        hs_ref[...] = h_ref[...]
        stage_block(u0_ref, 0)

    for lt in range(nlt):
        hf, ll = divmod(lt, lt_per_half)
        x = stage_ref[plane[hf], :, ll * LANES:(ll + 1) * LANES]
        gelu = x * (1.0 + lax.erf(x * math.sqrt(0.5)))
        w = stage_ref[plane[2 + hf], :, ll * LANES:(ll + 1) * LANES]
        g_ref[:, lt * LANES:(lt + 1) * LANES] = (w * gelu).astype(BF16)
    for hf in range(2):
        hs = slice(hf * half, (hf + 1) * half)
        acc_ref[:, hs] += jnp.dot(vt_ref[...], g_ref[:, hs], preferred_element_type=F32)

    stage_block(un_ref, jnp.minimum(e + 1, last))

    @pl.when(e == last)
    def _():
        o_ref[...] = x1_ref[...] + acc_ref[...].T


def _peer_experts(h2t, a, c0, b, r1, u, vt, x1):
    tt, eb = EXPERT_TT, EXPERT_EB
    assert SELECT_TT == tt and h2t.shape[2] == tt
    t = h2t.shape[0] * tt
    nblk = PEER_EXPERTS // eb
    sel = pl.BlockSpec((None, PEER_HEADS, PEER_KEYS, tt), lambda i, e: (i, 0, 0, 0))
    row = pl.BlockSpec((tt, D_MODEL), lambda i, e: (i, 0))
    return pl.pallas_call(
        _experts_kernel,
        grid=(t // tt, nblk),
        in_specs=[pl.BlockSpec(memory_space=pltpu.SMEM),
                  pl.BlockSpec((None, D_MODEL, tt), lambda i, e: (i, 0, 0)), sel, sel, sel, sel,
                  pl.BlockSpec((eb, D_MODEL), lambda i, e: (0, 0)),
                  pl.BlockSpec((eb, D_MODEL), lambda i, e: (jnp.minimum(e + 1, nblk - 1), 0)),
                  pl.BlockSpec((None, D_MODEL, eb), lambda i, e: (e, 0, 0)),
                  row],
        out_specs=row,
        out_shape=jax.ShapeDtypeStruct((t, D_MODEL), F32),
        scratch_shapes=[pltpu.VMEM((D_MODEL, tt), F32), pltpu.VMEM((eb, tt), BF16),
                        pltpu.VMEM((2, eb // PEER_KEYS, PEER_HEADS, SUBLANES, tt), F32),
                        pltpu.VMEM((4, eb, tt // 2), F32), pltpu.VMEM((D_MODEL, tt), BF16)],
        compiler_params=_cparams(("parallel", "arbitrary")),
        name="peer_experts",
    )(jnp.zeros((4,), jnp.int32), h2t, a, c0, b, r1, u, u, vt, x1)


def _layer(x2, batch, seq, mix_gain, w_in, gate_bias, q_gain, k_gain, sinks,
           w_up_swa, w_up_sb, w_out, ffn_gain, w_q, sub_keys, u, v):
    qa, ka, va, qb, kb, vb, gl = _inproj(x2, mix_gain.reshape(1, -1), w_in.astype(BF16))
    ya = _swa(qa, ka, va, q_gain, k_gain, sinks, batch, seq)
    yb = _stickbreak(qb, kb, vb, batch, seq)
    x1, h2t, qt = _merge(ya, yb, gl, gate_bias.reshape(1, -1), x2,
                         w_up_swa.astype(BF16), w_up_sb.astype(BF16), w_out.astype(BF16),
                         ffn_gain.reshape(1, -1), w_q.T.astype(BF16))
    a, c0, b, r1 = _peer_select(qt, sub_keys.astype(BF16))
    vt = v.reshape(PEER_EXPERTS // EXPERT_EB, EXPERT_EB, D_MODEL).transpose(0, 2, 1).astype(BF16)
    return _peer_experts(h2t, a, c0, b, r1, u.astype(BF16), vt, x1)


def kernel(x, mix_norm_gain, w_in, gate_bias, swa_q_gain, swa_k_gain, swa_sinks, w_up_swa, w_up_sb,
           w_out, ffn_norm_gain, peer_w_q, peer_sub_keys, peer_u, peer_v):
    batch, seq, d = x.shape
    x2 = x.reshape(batch * seq, d)
    for layer in range(mix_norm_gain.shape[0]):
        x2 = _layer(x2, batch, seq, mix_norm_gain[layer], w_in[layer], gate_bias[layer],
                    swa_q_gain[layer], swa_k_gain[layer], swa_sinks[layer], w_up_swa[layer],
                    w_up_sb[layer], w_out[layer], ffn_norm_gain[layer], peer_w_q[layer],
                    peer_sub_keys[layer], peer_u[layer], peer_v[layer])
    return x2.reshape(batch, seq, d)
```

```python
import functools
import math

import jax
import jax.numpy as jnp
from jax import lax
from jax.experimental import pallas as pl
from jax.experimental.pallas import tpu as pltpu

F32 = jnp.float32
BF16 = jnp.bfloat16

D_MODEL = 1024
HEAD_DIM = 64
CHUNK = 64
RMS_EPS = 1e-6
NEG_INF = -1e30

SWA_Q_HEADS = 8
SWA_GROUP = 4
SWA_WIDTH = 512
SWA_KV_WIDTH = 128
SB_HEADS = 8
SB_WIDTH = 512
GATE_WIDTH = 2 * D_MODEL
IN_WIDTH = SWA_WIDTH + 2 * SWA_KV_WIDTH + 3 * SB_WIDTH + GATE_WIDTH

PEER_HEADS = 8
PEER_KEYS = 128
PEER_HALF = 128
PEER_TOPK = 16
PEER_EXPERTS = PEER_KEYS * PEER_KEYS

LANES = 128
SUBLANES = 8
BLK = 128

INPROJ_TM = 512
MERGE_TM = 256
SELECT_TT = 512
SELECT_LANES = 256
EXPERT_TT = 512
EXPERT_EB = 1024
EXPERT_JCHUNK = 32
SB_QROWS = 2048
VMEM_LIMIT = 56 * 1024 * 1024

_NT = (((1,), (1,)), ((), ()))


def _cparams(sem):
    return pltpu.CompilerParams(dimension_semantics=sem, vmem_limit_bytes=VMEM_LIMIT)


def _inproj_kernel(x_ref, g_ref, w_ref, qa_ref, ka_ref, va_ref, qb_ref, kb_ref, vb_ref, gl_ref):
    x = x_ref[...]
    ms = jnp.mean(x * x, axis=-1, keepdims=True)
    h = (x * lax.rsqrt(ms + RMS_EPS) * g_ref[...]).astype(BF16)

    def proj(lo, hi):
        return jnp.dot(h, w_ref[:, lo:hi], preferred_element_type=F32)

    o = 0
    qa_ref[...] = proj(o, o + SWA_WIDTH); o += SWA_WIDTH
    ka_ref[...] = proj(o, o + SWA_KV_WIDTH); o += SWA_KV_WIDTH
    va_ref[...] = proj(o, o + SWA_KV_WIDTH).astype(BF16); o += SWA_KV_WIDTH
    qb_ref[...] = (proj(o, o + SB_WIDTH) * (HEAD_DIM ** -0.5)).astype(BF16); o += SB_WIDTH
    kb_ref[...] = proj(o, o + SB_WIDTH).astype(BF16); o += SB_WIDTH
    vb_ref[...] = proj(o, o + SB_WIDTH).astype(BF16); o += SB_WIDTH
    gl_ref[...] = proj(o, o + GATE_WIDTH)


def _inproj(x2, gain, w_in):
    t = x2.shape[0]
    tm = INPROJ_TM
    row = lambda w: pl.BlockSpec((tm, w), lambda i: (i, 0))
    full = lambda a: pl.BlockSpec(a.shape, lambda i: (0,) * a.ndim)
    widths = (SWA_WIDTH, SWA_KV_WIDTH, SWA_KV_WIDTH, SB_WIDTH, SB_WIDTH, SB_WIDTH, GATE_WIDTH)
    dtypes = (F32, F32, BF16, BF16, BF16, BF16, F32)
    return pl.pallas_call(
        _inproj_kernel,
        grid=(t // tm,),
        in_specs=[row(D_MODEL), full(gain), full(w_in)],
        out_specs=[row(w) for w in widths],
        out_shape=[jax.ShapeDtypeStruct((t, w), dt) for w, dt in zip(widths, dtypes)],
        compiler_params=_cparams(("parallel",)),
        name="inproj",
    )(x2, gain, w_in)


def _half_rms(x, lo):
    sq = x * x
    s_lo = jnp.sum(jnp.where(lo, sq, 0.0), axis=-1, keepdims=True)
    s_hi = jnp.sum(jnp.where(lo, 0.0, sq), axis=-1, keepdims=True)
    inv = jnp.where(lo, lax.rsqrt(s_lo / HEAD_DIM + RMS_EPS), lax.rsqrt(s_hi / HEAD_DIM + RMS_EPS))
    return x * inv


def _swa_kernel(sink_ref, q_ref, k_ref, v_ref, qg_ref, kg_ref, o_ref,
                kl0, kh0, kl1, kh1, vl0, vh0, vl1, vh1):
    p = pl.program_id(1)
    lo = lax.broadcasted_iota(jnp.int32, (BLK, LANES), 1) < HEAD_DIM
    kv_scratch = (kl0, kh0, kl1, kh1, vl0, vh0, vl1, vh1)

    @pl.when(p == 0)
    def _():
        for r in kv_scratch:
            r[0:BLK, :] = jnp.zeros((BLK, LANES), BF16)

    row = pl.multiple_of((p + 1) * BLK, BLK)
    kn = _half_rms(k_ref[...], lo) * kg_ref[...]
    kr = pltpu.roll(kn, HEAD_DIM, 1)
    v = v_ref[...].astype(F32)
    vr = pltpu.roll(v, HEAD_DIM, 1)
    kl0[pl.ds(row, BLK), :] = jnp.where(lo, kn, 0.0).astype(BF16)
    kh0[pl.ds(row, BLK), :] = jnp.where(lo, 0.0, kr).astype(BF16)
    kl1[pl.ds(row, BLK), :] = jnp.where(lo, kr, 0.0).astype(BF16)
    kh1[pl.ds(row, BLK), :] = jnp.where(lo, 0.0, kn).astype(BF16)
    vl0[pl.ds(row, BLK), :] = jnp.where(lo, v, 0.0).astype(BF16)
    vh0[pl.ds(row, BLK), :] = jnp.where(lo, 0.0, vr).astype(BF16)
    vl1[pl.ds(row, BLK), :] = jnp.where(lo, vr, 0.0).astype(BF16)
    vh1[pl.ds(row, BLK), :] = jnp.where(lo, 0.0, v).astype(BF16)

    qi = lax.broadcasted_iota(jnp.int32, (BLK, 2 * BLK), 0)
    kj = lax.broadcasted_iota(jnp.int32, (BLK, 2 * BLK), 1)
    dist = jnp.abs(qi + BLK - kj).astype(F32)
    qc = qi // CHUNK
    kc = kj // CHUNK
    vis = (kc >= qc) & (kc <= qc + 2) & ((kj >= BLK) | (p > 0))
    win = pl.ds(pl.multiple_of(p * BLK, BLK), 2 * BLK)

    for m in range(SWA_Q_HEADS // 2):
        g = (2 * m) // SWA_GROUP
        kv = ((kl0, vl0), (kh0, vh0)) if g == 0 else ((kl1, vl1), (kh1, vh1))
        qn = _half_rms(q_ref[:, m * LANES:(m + 1) * LANES], lo) * qg_ref[...]
        qs = (qn * (HEAD_DIM ** -0.5)).astype(BF16)
        acc = jnp.zeros((BLK, LANES), F32)
        for which, (kref, vref) in enumerate(kv):
            h = 2 * m + which
            slope = 2.0 ** (-(h + 1))
            logits = lax.dot_general(qs, kref[win, :], _NT, preferred_element_type=F32)
            logits = jnp.where(vis, logits - slope * dist, NEG_INF)
            sink = sink_ref[h]
            mx = jnp.maximum(jnp.max(logits, axis=-1, keepdims=True), sink)
            pe = jnp.exp(logits - mx)
            den = jnp.sum(pe, axis=-1, keepdims=True) + jnp.exp(sink - mx)
            probs = (pe / den).astype(BF16)
            acc = acc + jnp.dot(probs, vref[win, :], preferred_element_type=F32)
        o_ref[:, m * LANES:(m + 1) * LANES] = acc.astype(BF16)


def _swa(qa, ka, va, qgain, kgain, sinks, batch, seq):
    nb = seq // BLK
    qg2 = jnp.concatenate([qgain, qgain]).reshape(1, LANES)
    kg2 = jnp.concatenate([kgain, kgain]).reshape(1, LANES)
    blk = lambda w: pl.BlockSpec((BLK, w), lambda b, p: (b * nb + p, 0))
    one = pl.BlockSpec((1, LANES), lambda b, p: (0, 0))
    return pl.pallas_call(
        _swa_kernel,
        grid=(batch, nb),
        in_specs=[pl.BlockSpec(memory_space=pltpu.SMEM), blk(SWA_WIDTH), blk(SWA_KV_WIDTH),
                  blk(SWA_KV_WIDTH), one, one],
        out_specs=blk(SWA_WIDTH),
        out_shape=jax.ShapeDtypeStruct((batch * seq, SWA_WIDTH), BF16),
        scratch_shapes=[pltpu.VMEM((seq + BLK, LANES), BF16)] * 8,
        compiler_params=_cparams(("parallel", "arbitrary")),
        name="swa",
    )(sinks, qa, ka, va, qg2, kg2)


def _sb_kernel(q_ref, k_ref, v_ref, tri_ref, o_ref, kk, vv, carry, acc):
    seq = q_ref.shape[0]
    nb = seq // BLK
    per = SB_QROWS // BLK
    lo = lax.broadcasted_iota(jnp.int32, (nb, BLK, LANES), 2) < HEAD_DIM
    k3 = k_ref[...].reshape(nb, BLK, LANES)
    v3 = v_ref[...].reshape(nb, BLK, LANES)
    zero = jnp.zeros_like(k3)
    kk[:, 0:BLK, :] = jnp.where(lo, k3, zero)
    kk[:, BLK:2 * BLK, :] = jnp.where(lo, zero, k3)
    vv[:, 0:BLK, :] = jnp.where(lo, v3, zero)
    vv[:, BLK:2 * BLK, :] = jnp.where(lo, zero, v3)

    def tile(qs, kj, d):
        r0 = 0 if d is None else d * BLK
        m = SB_QROWS - r0
        q = q_ref[pl.ds(pl.multiple_of(qs * SB_QROWS + r0, BLK), m), :]
        z2 = lax.dot_general(q, kk[kj], _NT, preferred_element_type=F32)
        if d is not None:
            before = (lax.broadcasted_iota(jnp.int32, (m, BLK), 1)
                      < lax.broadcasted_iota(jnp.int32, (m, BLK), 0))
        ws = []
        for hh in range(2):
            z = z2[:, hh * BLK:(hh + 1) * BLK]
            log_keep = -(jnp.maximum(z, 0.0) + jnp.log(1.0 + jnp.exp(-jnp.abs(z))))
            log_beta = log_keep + z
            if d is not None:
                log_keep = jnp.where(before, log_keep, 0.0)
            hi = log_keep.astype(BF16)
            lw = (log_keep - hi.astype(F32)).astype(BF16)
            sc = jnp.dot(jnp.concatenate([hi, lw], axis=1), tri_ref[...], preferred_element_type=F32)
            suffix = sc[:, :BLK] + carry[hh, r0:, :]
            carry[hh, r0:, :] = carry[hh, r0:, :] + sc[:, BLK:]
            w = jnp.exp(log_beta + suffix)
            if d is not None:
                w = jnp.where(before, w, 0.0)
            ws.append(w.astype(BF16))
        acc[r0:, :] = acc[r0:, :] + jnp.dot(jnp.concatenate(ws, axis=1), vv[kj],
                                           preferred_element_type=F32)

    def qblock(qs, c):
        carry[...] = jnp.zeros(carry.shape, F32)
        acc[...] = jnp.zeros(acc.shape, F32)
        for d in reversed(range(per)):
            tile(qs, qs * per + d, d)

        def inner(t, c2):
            tile(qs, qs * per - 1 - t, None)
            return c2

        lax.fori_loop(0, qs * per, inner, 0)
        o_ref[pl.ds(pl.multiple_of(qs * SB_QROWS, BLK), SB_QROWS), :] = acc[...].astype(BF16)
        return c

    lax.fori_loop(0, seq // SB_QROWS, qblock, 0)


def _suffix_matrix():
    j = jnp.arange(BLK)[:, None]
    s = jnp.arange(BLK)[None, :]
    strict = (j > s).astype(BF16)
    half = jnp.concatenate([strict, jnp.ones((BLK, BLK), BF16)], axis=1)
    return jnp.concatenate([half, half], axis=0)


def _stickbreak(qb, kb, vb, batch, seq):
    npair = SB_HEADS // 2
    blk = pl.BlockSpec((seq, LANES), lambda b, m: (b, m))
    tri = _suffix_matrix()
    return pl.pallas_call(
        _sb_kernel,
        grid=(batch, npair),
        in_specs=[blk, blk, blk, pl.BlockSpec(tri.shape, lambda b, m: (0, 0))],
        out_specs=blk,
        out_shape=jax.ShapeDtypeStruct((batch * seq, SB_WIDTH), BF16),
        scratch_shapes=[pltpu.VMEM((seq // BLK, 2 * BLK, LANES), BF16)] * 2
        + [pltpu.VMEM((2, SB_QROWS, BLK), F32), pltpu.VMEM((SB_QROWS, LANES), F32)],
        compiler_params=_cparams(("parallel", "parallel")),
        name="stickbreak",
    )(qb, kb, vb, tri)


def _merge_kernel(ya_ref, yb_ref, gl_ref, bias_ref, x_ref, wa_ref, wb_ref, wo_ref, g2_ref, wq_ref,
                  x1_ref, h2_ref, qt_ref):
    ua = jnp.dot(ya_ref[...], wa_ref[...], preferred_element_type=F32)
    ub = jnp.dot(yb_ref[...], wb_ref[...], preferred_element_type=F32)
    gates = jax.nn.sigmoid(gl_ref[...] + bias_ref[...])
    merged = gates[:, :D_MODEL] * ua + gates[:, D_MODEL:] * ub
    x1 = x_ref[...] + jnp.dot(merged.astype(BF16), wo_ref[...], preferred_element_type=F32)
    x1_ref[...] = x1
    ms = jnp.mean(x1 * x1, axis=-1, keepdims=True)
    h2t = (x1 * lax.rsqrt(ms + RMS_EPS) * g2_ref[...]).T.astype(BF16)
    h2_ref[...] = h2t
    qt_ref[...] = jnp.dot(wq_ref[...], h2t, preferred_element_type=F32).astype(BF16)


def _merge(ya, yb, gl, bias, x2, wa, wb, wo, g2, wq_t):
    t = x2.shape[0]
    tm = MERGE_TM
    row = lambda w: pl.BlockSpec((tm, w), lambda i: (i, 0))
    full = lambda a: pl.BlockSpec(a.shape, lambda i: (0,) * a.ndim)
    nq = wq_t.shape[0]
    per = EXPERT_TT // tm
    return pl.pallas_call(
        _merge_kernel,
        grid=(t // tm,),
        in_specs=[row(SWA_WIDTH), row(SB_WIDTH), row(GATE_WIDTH), full(bias), row(D_MODEL),
                  full(wa), full(wb), full(wo), full(g2), full(wq_t)],
        out_specs=[row(D_MODEL),
                   pl.BlockSpec((None, D_MODEL, tm), lambda i: (i // per, 0, i % per)),
                   pl.BlockSpec((nq, tm), lambda i: (0, i))],
        out_shape=[jax.ShapeDtypeStruct((t, D_MODEL), F32),
                   jax.ShapeDtypeStruct((t // EXPERT_TT, D_MODEL, EXPERT_TT), BF16),
                   jax.ShapeDtypeStruct((nq, t), BF16)],
        compiler_params=_cparams(("parallel",)),
        name="merge",
    )(ya, yb, gl, bias, x2, wa, wb, wo, g2, wq_t)


_CAND_GROUPS = ((0, 16), (1, 8), (2, 5), (3, 4), (4, 3), (5, 2), (6, 2), (7, 2))
_CAND_ROWS = 16 + 7 * SUBLANES + SUBLANES


def _top16(s, exact_ties):
    n, tt = s.shape
    rank = jnp.full((n, tt), float(PEER_TOPK), F32)
    if exact_ties:
        pos = lax.broadcasted_iota(jnp.int32, (n, tt), 0).astype(F32)
    vals = []
    for r in range(PEER_TOPK):
        m = jnp.max(s, axis=0, keepdims=True)
        hit = s == m
        if exact_ties:
            hit = pos == jnp.min(jnp.where(hit, pos, float(n)), axis=0, keepdims=True)
        rank = jnp.where(hit, float(r), rank)
        s = jnp.where(hit, -jnp.inf, s)
        vals.append(m)
    return vals, rank


def _select_lanes(s0, s1, exact_ties):
    tt = s0.shape[1]
    v0, rank0 = _top16(s0, exact_ties)
    v1, rank1 = _top16(s1, exact_ties)

    row8 = lax.broadcasted_iota(jnp.int32, (SUBLANES, tt), 0)
    stack8 = lambda rows: functools.reduce(
        lambda acc, kv: jnp.where(row8 == kv[0], kv[1], acc), enumerate(rows), jnp.zeros((SUBLANES, tt), F32))
    v1_lo, v1_hi = stack8(v1[:8]), stack8(v1[8:])
    v0_hi = stack8(v0[8:])

    pieces = []
    for r0, n in _CAND_GROUPS:
        lo_piece = v0[r0] + v1_lo
        pieces.append(lo_piece if n >= SUBLANES else jnp.where(row8 < n, lo_piece, -jnp.inf))
        if n > SUBLANES:
            pieces.append(v0[r0] + v1_hi)
    pieces.append(v0_hi + v1[0])
    cand = jnp.concatenate(pieces, axis=0)

    if exact_ties:
        pos = lax.broadcasted_iota(jnp.int32, cand.shape, 0).astype(F32)
    chosen = jnp.zeros(cand.shape, F32)
    c = cand
    for _ in range(PEER_TOPK):
        m = jnp.max(c, axis=0, keepdims=True)
        hit = c == m
        if exact_ties:
            hit = pos == jnp.min(jnp.where(hit, pos, float(_CAND_ROWS)), axis=0, keepdims=True)
        chosen = jnp.where(hit, 1.0, chosen)
        c = jnp.where(hit, -jnp.inf, c)

    best = v0[0] + v1[0]
    z = jnp.sum(jnp.where(chosen > 0.0, jnp.exp(cand - best), 0.0), axis=0, keepdims=True)

    counts = []
    off = 0
    for r0, n in _CAND_GROUPS:
        rows = SUBLANES * ((n + SUBLANES - 1) // SUBLANES)
        counts.append(jnp.sum(chosen[off:off + rows], axis=0, keepdims=True))
        off += rows
    for r in range(SUBLANES):
        counts.append(chosen[off + r:off + r + 1])

    c0 = jnp.zeros(s0.shape, F32)
    for r in range(PEER_TOPK):
        c0 = jnp.where(rank0 == float(r), counts[r], c0)

    taken = lambda rk: jnp.sum(jnp.where(rk < float(PEER_TOPK), 1.0, 0.0), axis=0, keepdims=True)
    unique = ((taken(rank0) == float(PEER_TOPK)) & (taken(rank1) == float(PEER_TOPK))
              & (jnp.sum(chosen, axis=0, keepdims=True) == float(PEER_TOPK)))
    return (jnp.exp(s0 - v0[0]) * (0.5 / z), c0, jnp.exp(s1 - v1[0]), rank1), unique


def _select_kernel(q_ref, sk_ref, a_ref, c_ref, b_ref, r_ref, s_ref):
    s_ref[0] = jnp.dot(sk_ref[0, 0], q_ref[0:PEER_HALF, :], preferred_element_type=F32)
    s_ref[1] = jnp.dot(sk_ref[0, 1], q_ref[PEER_HALF:2 * PEER_HALF, :], preferred_element_type=F32)

    def lane_tile(lt, carry):
        ls = pl.ds(pl.multiple_of(lt * SELECT_LANES, SELECT_LANES), SELECT_LANES)

        def run(exact_ties):
            outs, unique = _select_lanes(s_ref[0, :, ls], s_ref[1, :, ls], exact_ties)
            for ref, val in zip((a_ref, c_ref, b_ref, r_ref), outs):
                ref[0, :, ls] = val
            return unique

        unique = run(False)

        @pl.when(jnp.min(jnp.where(unique, 1.0, 0.0)) < 0.5)
        def _():
            run(True)

        return carry

    lax.fori_loop(0, q_ref.shape[1] // SELECT_LANES, lane_tile, 0)


def _peer_select(qt, sub_keys):
    t = qt.shape[1]
    tt = SELECT_TT
    out = pl.BlockSpec((None, 1, PEER_KEYS, tt), lambda i, h: (i, h, 0, 0))
    shp = lambda dt: jax.ShapeDtypeStruct((t // tt, PEER_HEADS, PEER_KEYS, tt), dt)
    return pl.pallas_call(
        _select_kernel,
        grid=(t // tt, PEER_HEADS),
        in_specs=[pl.BlockSpec((2 * PEER_HALF, tt), lambda i, h: (h, i)),
                  pl.BlockSpec((1, 2, PEER_KEYS, PEER_HALF), lambda i, h: (h, 0, 0, 0))],
        out_specs=[out] * 4,
        out_shape=[shp(F32)] * 4,
        scratch_shapes=[pltpu.VMEM((2, PEER_KEYS, tt), F32)],
        compiler_params=_cparams(("parallel", "parallel")),
        name="peer_select",
    )(qt, sub_keys)


def _experts_kernel(zero_ref, h_ref, a_ref, c_ref, b_ref, r_ref, u0_ref, un_ref, vt_ref, x1_ref, o_ref,
                    acc_ref, g_ref, abc_ref, stage_ref, hs_ref):
    e = pl.program_id(1)
    last = pl.num_programs(1) - 1
    eb = un_ref.shape[0]
    nsub = eb // PEER_KEYS
    tt = h_ref.shape[1]
    half = tt // 2
    lt_per_half = half // LANES
    nlt = 2 * lt_per_half
    nv = EXPERT_JCHUNK // SUBLANES
    njc = PEER_KEYS // EXPERT_JCHUNK
    plane = [zero_ref[k] + k for k in range(4)]
    zero_bits = jnp.full((SUBLANES, LANES), zero_ref[0], jnp.int32)

    def zero_bits_after(x):
        return pltpu.bitcast(x, jnp.int32) & zero_bits

    def gate_weights(lt, last_chunk_init=None):
        hf, ll = divmod(lt, lt_per_half)
        ls = slice(lt * LANES, (lt + 1) * LANES)
        for jc in range(njc):
            j0 = jc * EXPERT_JCHUNK
            init = jnp.zeros((SUBLANES, LANES), F32)
            if last_chunk_init is not None and jc == njc - 1:
                init = last_chunk_init
            w = [[init] * nv for _ in range(nsub)]
            for h in range(PEER_HEADS):
                bs = [b_ref[h, j0 + v * SUBLANES:j0 + (v + 1) * SUBLANES, ls] for v in range(nv)]
                rs = [r_ref[h, j0 + v * SUBLANES:j0 + (v + 1) * SUBLANES, ls] for v in range(nv)]
                for ii in range(nsub):
                    a8 = abc_ref[0, ii, h, :, ls]
                    c8 = abc_ref[1, ii, h, :, ls]
                    for v in range(nv):
                        w[ii][v] = w[ii][v] + jnp.where(rs[v] < c8, a8 * bs[v], 0.0)
            for ii in range(nsub):
                r0 = ii * PEER_KEYS + j0
                stage_ref[plane[2 + hf], r0:r0 + EXPERT_JCHUNK, ll * LANES:(ll + 1) * LANES] = (
                    jnp.concatenate(w[ii], axis=0))
        return w[0][0]

    def stage_block(u_ref, blk):
        for ii in range(nsub):
            i = blk * nsub + ii
            for h in range(PEER_HEADS):
                abc_ref[0, ii, h] = jnp.broadcast_to(a_ref[h, pl.ds(i, 1), :], (SUBLANES, tt))
                abc_ref[1, ii, h] = jnp.broadcast_to(c_ref[h, pl.ds(i, 1), :], (SUBLANES, tt))
        first = gate_weights(0)
        head = (slice(0, 2 * SUBLANES), slice(0, LANES))
        hs_ref[head] = pltpu.bitcast(
            pltpu.bitcast(hs_ref[head], jnp.int32) | zero_bits_after(first), BF16)
        for hf in range(2):
            hs = slice(hf * half, (hf + 1) * half)
            stage_ref[plane[hf]] = jnp.dot(u_ref[...], hs_ref[:, hs],
                                           preferred_element_type=F32)
        for lt in range(1, nlt - 1):
            gate_weights(lt)
        tail = stage_ref[plane[1], eb - SUBLANES:eb, half - LANES:half]
        gate_weights(nlt - 1, pltpu.bitcast(zero_bits_after(tail), F32))

    @pl.when(e == 0)
    def _():
        acc_ref[...] = jnp.zeros(acc_ref.shape, F32)
        hs_ref[...] = h_ref[...]
        stage_block(u0_ref, 0)

    for lt in range(nlt):
        hf, ll = divmod(lt, lt_per_half)
        x = stage_ref[plane[hf], :, ll * LANES:(ll + 1) * LANES]
        gelu = x * (1.0 + lax.erf(x * math.sqrt(0.5)))
        w = stage_ref[plane[2 + hf], :, ll * LANES:(ll + 1) * LANES]
        g_ref[:, lt * LANES:(lt + 1) * LANES] = (w * gelu).astype(BF16)
    for hf in range(2):
        hs = slice(hf * half, (hf + 1) * half)
        acc_ref[:, hs] += jnp.dot(vt_ref[...], g_ref[:, hs], preferred_element_type=F32)

    stage_block(un_ref, jnp.minimum(e + 1, last))

    @pl.when(e == last)
    def _():
        o_ref[...] = x1_ref[...] + acc_ref[...].T


def _peer_experts(h2t, a, c0, b, r1, u, vt, x1):
    tt, eb = EXPERT_TT, EXPERT_EB
    assert SELECT_TT == tt and h2t.shape[2] == tt
    t = h2t.shape[0] * tt
    nblk = PEER_EXPERTS // eb
    sel = pl.BlockSpec((None, PEER_HEADS, PEER_KEYS, tt), lambda i, e: (i, 0, 0, 0))
    row = pl.BlockSpec((tt, D_MODEL), lambda i, e: (i, 0))
    return pl.pallas_call(
        _experts_kernel,
        grid=(t // tt, nblk),
        in_specs=[pl.BlockSpec(memory_space=pltpu.SMEM),
                  pl.BlockSpec((None, D_MODEL, tt), lambda i, e: (i, 0, 0)), sel, sel, sel, sel,
                  pl.BlockSpec((eb, D_MODEL), lambda i, e: (0, 0)),
                  pl.BlockSpec((eb, D_MODEL), lambda i, e: (jnp.minimum(e + 1, nblk - 1), 0)),
                  pl.BlockSpec((None, D_MODEL, eb), lambda i, e: (e, 0, 0)),
                  row],
        out_specs=row,
        out_shape=jax.ShapeDtypeStruct((t, D_MODEL), F32),
        scratch_shapes=[pltpu.VMEM((D_MODEL, tt), F32), pltpu.VMEM((eb, tt), BF16),
                        pltpu.VMEM((2, eb // PEER_KEYS, PEER_HEADS, SUBLANES, tt), F32),
                        pltpu.VMEM((4, eb, tt // 2), F32), pltpu.VMEM((D_MODEL, tt), BF16)],
        compiler_params=_cparams(("parallel", "arbitrary")),
        name="peer_experts",
    )(jnp.zeros((4,), jnp.int32), h2t, a, c0, b, r1, u, u, vt, x1)


def _layer(x2, batch, seq, mix_gain, w_in, gate_bias, q_gain, k_gain, sinks,
           w_up_swa, w_up_sb, w_out, ffn_gain, w_q, sub_keys, u, v):
    qa, ka, va, qb, kb, vb, gl = _inproj(x2, mix_gain.reshape(1, -1), w_in.astype(BF16))
    ya = _swa(qa, ka, va, q_gain, k_gain, sinks, batch, seq)
    yb = _stickbreak(qb, kb, vb, batch, seq)
    x1, h2t, qt = _merge(ya, yb, gl, gate_bias.reshape(1, -1), x2,
                         w_up_swa.astype(BF16), w_up_sb.astype(BF16), w_out.astype(BF16),
                         ffn_gain.reshape(1, -1), w_q.T.astype(BF16))
    a, c0, b, r1 = _peer_select(qt, sub_keys.astype(BF16))
    vt = v.reshape(PEER_EXPERTS // EXPERT_EB, EXPERT_EB, D_MODEL).transpose(0, 2, 1).astype(BF16)
    return _peer_experts(h2t, a, c0, b, r1, u.astype(BF16), vt, x1)


def kernel(x, mix_norm_gain, w_in, gate_bias, swa_q_gain, swa_k_gain, swa_sinks, w_up_swa, w_up_sb,
           w_out, ffn_norm_gain, peer_w_q, peer_sub_keys, peer_u, peer_v):
    batch, seq, d = x.shape
    x2 = x.reshape(batch * seq, d)
    for layer in range(mix_norm_gain.shape[0]):
        x2 = _layer(x2, batch, seq, mix_norm_gain[layer], w_in[layer], gate_bias[layer],
                    swa_q_gain[layer], swa_k_gain[layer], swa_sinks[layer], w_up_swa[layer],
                    w_up_sb[layer], w_out[layer], ffn_norm_gain[layer], peer_w_q[layer],
                    peer_sub_keys[layer], peer_u[layer], peer_v[layer])
    return x2.reshape(batch, seq, d)
```

```python
import functools
import math

import jax
import jax.numpy as jnp
from jax import lax
from jax.experimental import pallas as pl
from jax.experimental.pallas import tpu as pltpu

F32 = jnp.float32
BF16 = jnp.bfloat16

D_MODEL = 1024
HEAD_DIM = 64
CHUNK = 64
RMS_EPS = 1e-6
NEG_INF = -1e30

SWA_Q_HEADS = 8
SWA_GROUP = 4
SWA_WIDTH = 512
SWA_KV_WIDTH = 128
SB_HEADS = 8
SB_WIDTH = 512
GATE_WIDTH = 2 * D_MODEL
IN_WIDTH = SWA_WIDTH + 2 * SWA_KV_WIDTH + 3 * SB_WIDTH + GATE_WIDTH

PEER_HEADS = 8
PEER_KEYS = 128
PEER_HALF = 128
PEER_TOPK = 16
PEER_EXPERTS = PEER_KEYS * PEER_KEYS

LANES = 128
SUBLANES = 8
BLK = 128

INPROJ_TM = 512
MERGE_TM = 256
SELECT_TT = 512
SELECT_LANES = 256
EXPERT_TT = 512
EXPERT_EB = 1024
EXPERT_JCHUNK = 32
SB_QROWS = 2048
VMEM_LIMIT = 56 * 1024 * 1024

_NT = (((1,), (1,)), ((), ()))


def _cparams(sem):
    return pltpu.CompilerParams(dimension_semantics=sem, vmem_limit_bytes=VMEM_LIMIT)


def _inproj_kernel(x_ref, g_ref, w_ref, qa_ref, ka_ref, va_ref, qb_ref, kb_ref, vb_ref, gl_ref):
    x = x_ref[...]
    ms = jnp.mean(x * x, axis=-1, keepdims=True)
    h = (x * lax.rsqrt(ms + RMS_EPS) * g_ref[...]).astype(BF16)

    def proj(lo, hi):
        return jnp.dot(h, w_ref[:, lo:hi], preferred_element_type=F32)

    o = 0
    qa_ref[...] = proj(o, o + SWA_WIDTH); o += SWA_WIDTH
    ka_ref[...] = proj(o, o + SWA_KV_WIDTH); o += SWA_KV_WIDTH
    va_ref[...] = proj(o, o + SWA_KV_WIDTH).astype(BF16); o += SWA_KV_WIDTH
    qb_ref[...] = (proj(o, o + SB_WIDTH) * (HEAD_DIM ** -0.5)).astype(BF16); o += SB_WIDTH
    kb_ref[...] = proj(o, o + SB_WIDTH).astype(BF16); o += SB_WIDTH
    vb_ref[...] = proj(o, o + SB_WIDTH).astype(BF16); o += SB_WIDTH
    gl_ref[...] = proj(o, o + GATE_WIDTH)


def _inproj(x2, gain, w_in):
    t = x2.shape[0]
    tm = INPROJ_TM
    row = lambda w: pl.BlockSpec((tm, w), lambda i: (i, 0))
    full = lambda a: pl.BlockSpec(a.shape, lambda i: (0,) * a.ndim)
    widths = (SWA_WIDTH, SWA_KV_WIDTH, SWA_KV_WIDTH, SB_WIDTH, SB_WIDTH, SB_WIDTH, GATE_WIDTH)
    dtypes = (F32, F32, BF16, BF16, BF16, BF16, F32)
    return pl.pallas_call(
        _inproj_kernel,
        grid=(t // tm,),
        in_specs=[row(D_MODEL), full(gain), full(w_in)],
        out_specs=[row(w) for w in widths],
        out_shape=[jax.ShapeDtypeStruct((t, w), dt) for w, dt in zip(widths, dtypes)],
        compiler_params=_cparams(("parallel",)),
        name="inproj",
    )(x2, gain, w_in)


def _half_rms(x, lo):
    sq = x * x
    s_lo = jnp.sum(jnp.where(lo, sq, 0.0), axis=-1, keepdims=True)
    s_hi = jnp.sum(jnp.where(lo, 0.0, sq), axis=-1, keepdims=True)
    inv = jnp.where(lo, lax.rsqrt(s_lo / HEAD_DIM + RMS_EPS), lax.rsqrt(s_hi / HEAD_DIM + RMS_EPS))
    return x * inv


def _swa_kernel(sink_ref, q_ref, k_ref, v_ref, qg_ref, kg_ref, o_ref,
                kl0, kh0, kl1, kh1, vl0, vh0, vl1, vh1):
    lo = lax.broadcasted_iota(jnp.int32, (BLK, LANES), 1) < HEAD_DIM
    for r in (kl0, kh0, kl1, kh1, vl0, vh0, vl1, vh1):
        r[0:BLK, :] = jnp.zeros((BLK, LANES), BF16)

    qi = lax.broadcasted_iota(jnp.int32, (BLK, 2 * BLK), 0)
    kj = lax.broadcasted_iota(jnp.int32, (BLK, 2 * BLK), 1)
    dist = jnp.abs(qi + BLK - kj).astype(F32)
    qc = qi // CHUNK
    kc = kj // CHUNK
    band = (kc >= qc) & (kc <= qc + 2)

    def block(p, carry):
        cur = pl.ds(pl.multiple_of(p * BLK, BLK), BLK)
        row = pl.ds(pl.multiple_of((p + 1) * BLK, BLK), BLK)
        kn = _half_rms(k_ref[cur, :], lo) * kg_ref[...]
        kr = pltpu.roll(kn, HEAD_DIM, 1)
        v = v_ref[cur, :].astype(F32)
        vr = pltpu.roll(v, HEAD_DIM, 1)
        kl0[row, :] = jnp.where(lo, kn, 0.0).astype(BF16)
        kh0[row, :] = jnp.where(lo, 0.0, kr).astype(BF16)
        kl1[row, :] = jnp.where(lo, kr, 0.0).astype(BF16)
        kh1[row, :] = jnp.where(lo, 0.0, kn).astype(BF16)
        vl0[row, :] = jnp.where(lo, v, 0.0).astype(BF16)
        vh0[row, :] = jnp.where(lo, 0.0, vr).astype(BF16)
        vl1[row, :] = jnp.where(lo, vr, 0.0).astype(BF16)
        vh1[row, :] = jnp.where(lo, 0.0, v).astype(BF16)

        vis = band & ((kj >= BLK) | (p > 0))
        win = pl.ds(pl.multiple_of(p * BLK, BLK), 2 * BLK)
        for m in range(SWA_Q_HEADS // 2):
            g = (2 * m) // SWA_GROUP
            kv = ((kl0, vl0), (kh0, vh0)) if g == 0 else ((kl1, vl1), (kh1, vh1))
            qn = _half_rms(q_ref[cur, m * LANES:(m + 1) * LANES], lo) * qg_ref[...]
            qs = (qn * (HEAD_DIM ** -0.5)).astype(BF16)
            acc = jnp.zeros((BLK, LANES), F32)
            for which, (kref, vref) in enumerate(kv):
                h = 2 * m + which
                slope = 2.0 ** (-(h + 1))
                logits = lax.dot_general(qs, kref[win, :], _NT, preferred_element_type=F32)
                logits = jnp.where(vis, logits - slope * dist, NEG_INF)
                sink = sink_ref[h]
                mx = jnp.maximum(jnp.max(logits, axis=-1, keepdims=True), sink)
                pe = jnp.exp(logits - mx)
                den = jnp.sum(pe, axis=-1, keepdims=True) + jnp.exp(sink - mx)
                probs = (pe / den).astype(BF16)
                acc = acc + jnp.dot(probs, vref[win, :], preferred_element_type=F32)
            o_ref[cur, m * LANES:(m + 1) * LANES] = acc.astype(BF16)
        return carry

    lax.fori_loop(0, q_ref.shape[0] // BLK, block, 0)


def _swa(qa, ka, va, qgain, kgain, sinks, batch, seq):
    qg2 = jnp.concatenate([qgain, qgain]).reshape(1, LANES)
    kg2 = jnp.concatenate([kgain, kgain]).reshape(1, LANES)
    blk = lambda w: pl.BlockSpec((seq, w), lambda b: (b, 0))
    one = pl.BlockSpec((1, LANES), lambda b: (0, 0))
    return pl.pallas_call(
        _swa_kernel,
        grid=(batch,),
        in_specs=[pl.BlockSpec(memory_space=pltpu.SMEM), blk(SWA_WIDTH), blk(SWA_KV_WIDTH),
                  blk(SWA_KV_WIDTH), one, one],
        out_specs=blk(SWA_WIDTH),
        out_shape=jax.ShapeDtypeStruct((batch * seq, SWA_WIDTH), BF16),
        scratch_shapes=[pltpu.VMEM((seq + BLK, LANES), BF16)] * 8,
        compiler_params=_cparams(("parallel",)),
        name="swa",
    )(sinks, qa, ka, va, qg2, kg2)


def _sb_kernel(q_ref, k_ref, v_ref, tri_ref, o_ref, kk, vv, carry, acc):
    seq = q_ref.shape[0]
    nb = seq // BLK
    per = SB_QROWS // BLK
    lo = lax.broadcasted_iota(jnp.int32, (nb, BLK, LANES), 2) < HEAD_DIM
    k3 = k_ref[...].reshape(nb, BLK, LANES)
    v3 = v_ref[...].reshape(nb, BLK, LANES)
    zero = jnp.zeros_like(k3)
    kk[:, 0:BLK, :] = jnp.where(lo, k3, zero)
    kk[:, BLK:2 * BLK, :] = jnp.where(lo, zero, k3)
    vv[:, 0:BLK, :] = jnp.where(lo, v3, zero)
    vv[:, BLK:2 * BLK, :] = jnp.where(lo, zero, v3)

    def tile(qs, kj, d):
        r0 = 0 if d is None else d * BLK
        m = SB_QROWS - r0
        q = q_ref[pl.ds(pl.multiple_of(qs * SB_QROWS + r0, BLK), m), :]
        z2 = lax.dot_general(q, kk[kj], _NT, preferred_element_type=F32)
        if d is not None:
            before = (lax.broadcasted_iota(jnp.int32, (m, BLK), 1)
                      < lax.broadcasted_iota(jnp.int32, (m, BLK), 0))
        ws = []
        for hh in range(2):
            z = z2[:, hh * BLK:(hh + 1) * BLK]
            log_keep = -(jnp.maximum(z, 0.0) + jnp.log(1.0 + jnp.exp(-jnp.abs(z))))
            log_beta = log_keep + z
            if d is not None:
                log_keep = jnp.where(before, log_keep, 0.0)
            hi = log_keep.astype(BF16)
            lw = (log_keep - hi.astype(F32)).astype(BF16)
            sc = jnp.dot(jnp.concatenate([hi, lw], axis=1), tri_ref[...], preferred_element_type=F32)
            suffix = sc[:, :BLK] + carry[hh, r0:, :]
            carry[hh, r0:, :] = carry[hh, r0:, :] + sc[:, BLK:]
            w = jnp.exp(log_beta + suffix)
            if d is not None:
                w = jnp.where(before, w, 0.0)
            ws.append(w.astype(BF16))
        acc[r0:, :] = acc[r0:, :] + jnp.dot(jnp.concatenate(ws, axis=1), vv[kj],
                                           preferred_element_type=F32)

    def qblock(qs, c):
        carry[...] = jnp.zeros(carry.shape, F32)
        acc[...] = jnp.zeros(acc.shape, F32)
        for d in reversed(range(per)):
            tile(qs, qs * per + d, d)

        def inner(t, c2):
            tile(qs, qs * per - 1 - t, None)
            return c2

        lax.fori_loop(0, qs * per, inner, 0)
        o_ref[pl.ds(pl.multiple_of(qs * SB_QROWS, BLK), SB_QROWS), :] = acc[...].astype(BF16)
        return c

    lax.fori_loop(0, seq // SB_QROWS, qblock, 0)


def _suffix_matrix():
    j = jnp.arange(BLK)[:, None]
    s = jnp.arange(BLK)[None, :]
    strict = (j > s).astype(BF16)
    half = jnp.concatenate([strict, jnp.ones((BLK, BLK), BF16)], axis=1)
    return jnp.concatenate([half, half], axis=0)


def _stickbreak(qb, kb, vb, batch, seq):
    npair = SB_HEADS // 2
    blk = pl.BlockSpec((seq, LANES), lambda b, m: (b, m))
    tri = _suffix_matrix()
    return pl.pallas_call(
        _sb_kernel,
        grid=(batch, npair),
        in_specs=[blk, blk, blk, pl.BlockSpec(tri.shape, lambda b, m: (0, 0))],
        out_specs=blk,
        out_shape=jax.ShapeDtypeStruct((batch * seq, SB_WIDTH), BF16),
        scratch_shapes=[pltpu.VMEM((seq // BLK, 2 * BLK, LANES), BF16)] * 2
        + [pltpu.VMEM((2, SB_QROWS, BLK), F32), pltpu.VMEM((SB_QROWS, LANES), F32)],
        compiler_params=_cparams(("parallel", "parallel")),
        name="stickbreak",
    )(qb, kb, vb, tri)


def _merge_kernel(ya_ref, yb_ref, gl_ref, bias_ref, x_ref, wa_ref, wb_ref, wo_ref, g2_ref, wq_ref,
                  x1_ref, h2_ref, qt_ref):
    ua = jnp.dot(ya_ref[...], wa_ref[...], preferred_element_type=F32)
    ub = jnp.dot(yb_ref[...], wb_ref[...], preferred_element_type=F32)
    gates = jax.nn.sigmoid(gl_ref[...] + bias_ref[...])
    merged = gates[:, :D_MODEL] * ua + gates[:, D_MODEL:] * ub
    x1 = x_ref[...] + jnp.dot(merged.astype(BF16), wo_ref[...], preferred_element_type=F32)
    x1_ref[...] = x1
    ms = jnp.mean(x1 * x1, axis=-1, keepdims=True)
    h2t = (x1 * lax.rsqrt(ms + RMS_EPS) * g2_ref[...]).T.astype(BF16)
    h2_ref[...] = h2t
    qt_ref[...] = jnp.dot(wq_ref[...], h2t, preferred_element_type=F32).astype(BF16)


def _merge(ya, yb, gl, bias, x2, wa, wb, wo, g2, wq_t):
    t = x2.shape[0]
    tm = MERGE_TM
    row = lambda w: pl.BlockSpec((tm, w), lambda i: (i, 0))
    full = lambda a: pl.BlockSpec(a.shape, lambda i: (0,) * a.ndim)
    nq = wq_t.shape[0]
    per = EXPERT_TT // tm
    return pl.pallas_call(
        _merge_kernel,
        grid=(t // tm,),
        in_specs=[row(SWA_WIDTH), row(SB_WIDTH), row(GATE_WIDTH), full(bias), row(D_MODEL),
                  full(wa), full(wb), full(wo), full(g2), full(wq_t)],
        out_specs=[row(D_MODEL),
                   pl.BlockSpec((None, D_MODEL, tm), lambda i: (i // per, 0, i % per)),
                   pl.BlockSpec((nq, tm), lambda i: (0, i))],
        out_shape=[jax.ShapeDtypeStruct((t, D_MODEL), F32),
                   jax.ShapeDtypeStruct((t // EXPERT_TT, D_MODEL, EXPERT_TT), BF16),
                   jax.ShapeDtypeStruct((nq, t), BF16)],
        compiler_params=_cparams(("parallel",)),
        name="merge",
    )(ya, yb, gl, bias, x2, wa, wb, wo, g2, wq_t)


_CAND_GROUPS = ((0, 16), (1, 8), (2, 5), (3, 4), (4, 3), (5, 2), (6, 2), (7, 2))
_CAND_ROWS = 16 + 7 * SUBLANES + SUBLANES


def _top16(s, exact_ties):
    n, tt = s.shape
    rank = jnp.full((n, tt), float(PEER_TOPK), F32)
    if exact_ties:
        pos = lax.broadcasted_iota(jnp.int32, (n, tt), 0).astype(F32)
    vals = []
    for r in range(PEER_TOPK):
        m = jnp.max(s, axis=0, keepdims=True)
        hit = s == m
        if exact_ties:
            hit = pos == jnp.min(jnp.where(hit, pos, float(n)), axis=0, keepdims=True)
        rank = jnp.where(hit, float(r), rank)
        s = jnp.where(hit, -jnp.inf, s)
        vals.append(m)
    return vals, rank


def _select_lanes(s0, s1, exact_ties):
    tt = s0.shape[1]
    v0, rank0 = _top16(s0, exact_ties)
    v1, rank1 = _top16(s1, exact_ties)

    row8 = lax.broadcasted_iota(jnp.int32, (SUBLANES, tt), 0)
    stack8 = lambda rows: functools.reduce(
        lambda acc, kv: jnp.where(row8 == kv[0], kv[1], acc), enumerate(rows), jnp.zeros((SUBLANES, tt), F32))
    v1_lo, v1_hi = stack8(v1[:8]), stack8(v1[8:])
    v0_hi = stack8(v0[8:])

    pieces = []
    for r0, n in _CAND_GROUPS:
        lo_piece = v0[r0] + v1_lo
        pieces.append(lo_piece if n >= SUBLANES else jnp.where(row8 < n, lo_piece, -jnp.inf))
        if n > SUBLANES:
            pieces.append(v0[r0] + v1_hi)
    pieces.append(v0_hi + v1[0])
    cand = jnp.concatenate(pieces, axis=0)

    if exact_ties:
        pos = lax.broadcasted_iota(jnp.int32, cand.shape, 0).astype(F32)
    chosen = jnp.zeros(cand.shape, F32)
    c = cand
    for _ in range(PEER_TOPK):
        m = jnp.max(c, axis=0, keepdims=True)
        hit = c == m
        if exact_ties:
            hit = pos == jnp.min(jnp.where(hit, pos, float(_CAND_ROWS)), axis=0, keepdims=True)
        chosen = jnp.where(hit, 1.0, chosen)
        c = jnp.where(hit, -jnp.inf, c)

    best = v0[0] + v1[0]
    z = jnp.sum(jnp.where(chosen > 0.0, jnp.exp(cand - best), 0.0), axis=0, keepdims=True)

    counts = []
    off = 0
    for r0, n in _CAND_GROUPS:
        rows = SUBLANES * ((n + SUBLANES - 1) // SUBLANES)
        counts.append(jnp.sum(chosen[off:off + rows], axis=0, keepdims=True))
        off += rows
    for r in range(SUBLANES):
        counts.append(chosen[off + r:off + r + 1])

    c0 = jnp.zeros(s0.shape, F32)
    for r in range(PEER_TOPK):
        c0 = jnp.where(rank0 == float(r), counts[r], c0)

    taken = lambda rk: jnp.sum(jnp.where(rk < float(PEER_TOPK), 1.0, 0.0), axis=0, keepdims=True)
    unique = ((taken(rank0) == float(PEER_TOPK)) & (taken(rank1) == float(PEER_TOPK))
              & (jnp.sum(chosen, axis=0, keepdims=True) == float(PEER_TOPK)))
    return (jnp.exp(s0 - v0[0]) * (0.5 / z), c0, jnp.exp(s1 - v1[0]), rank1), unique


def _select_kernel(q_ref, sk_ref, a_ref, c_ref, b_ref, r_ref, s_ref):
    passes = q_ref.shape[1] // SELECT_LANES

    def head_pass(n, carry):
        h = n // passes
        ls = pl.ds(pl.multiple_of((n % passes) * SELECT_LANES, SELECT_LANES), SELECT_LANES)

        @pl.when(n % passes == 0)
        def _():
            for p in range(2):
                rows = pl.ds(pl.multiple_of((2 * h + p) * PEER_HALF, PEER_HALF), PEER_HALF)
                s_ref[p] = jnp.dot(sk_ref[h, p], q_ref[rows, :], preferred_element_type=F32)

        def run(exact_ties):
            outs, unique = _select_lanes(s_ref[0, :, ls], s_ref[1, :, ls], exact_ties)
            for ref, val in zip((a_ref, c_ref, b_ref, r_ref), outs):
                ref[h, :, ls] = val
            return unique

        unique = run(False)

        @pl.when(jnp.min(jnp.where(unique, 1.0, 0.0)) < 0.5)
        def _():
            run(True)

        return carry

    lax.fori_loop(0, PEER_HEADS * passes, head_pass, 0)


def _peer_select(qt, sub_keys):
    t = qt.shape[1]
    tt = SELECT_TT
    out = pl.BlockSpec((None, PEER_HEADS, PEER_KEYS, tt), lambda i: (i, 0, 0, 0))
    shp = lambda dt: jax.ShapeDtypeStruct((t // tt, PEER_HEADS, PEER_KEYS, tt), dt)
    return pl.pallas_call(
        _select_kernel,
        grid=(t // tt,),
        in_specs=[pl.BlockSpec((PEER_HEADS * 2 * PEER_HALF, tt), lambda i: (0, i)),
                  pl.BlockSpec(sub_keys.shape, lambda i: (0, 0, 0, 0))],
        out_specs=[out] * 4,
        out_shape=[shp(F32)] * 4,
        scratch_shapes=[pltpu.VMEM((2, PEER_KEYS, tt), F32)],
        compiler_params=_cparams(("parallel",)),
        name="peer_select",
    )(qt, sub_keys)


def _experts_kernel(zero_ref, h_ref, a_ref, c_ref, b_ref, r_ref, u0_ref, un_ref, vt_ref, x1_ref, o_ref,
                    acc_ref, g_ref, abc_ref, stage_ref, hs_ref):
    e = pl.program_id(1)
    last = pl.num_programs(1) - 1
    eb = un_ref.shape[0]
    nsub = eb // PEER_KEYS
    tt = h_ref.shape[1]
    half = tt // 2
    lt_per_half = half // LANES
    nlt = 2 * lt_per_half
    nv = EXPERT_JCHUNK // SUBLANES
    njc = PEER_KEYS // EXPERT_JCHUNK
    plane = [zero_ref[k] + k for k in range(4)]
    zero_bits = jnp.full((SUBLANES, LANES), zero_ref[0], jnp.int32)

    def zero_bits_after(x):
        return pltpu.bitcast(x, jnp.int32) & zero_bits

    def gate_weights(lt, last_chunk_init=None):
        hf, ll = divmod(lt, lt_per_half)
        ls = slice(lt * LANES, (lt + 1) * LANES)
        for jc in range(njc):
            j0 = jc * EXPERT_JCHUNK
            init = jnp.zeros((SUBLANES, LANES), F32)
            if last_chunk_init is not None and jc == njc - 1:
                init = last_chunk_init
            w = [[init] * nv for _ in range(nsub)]
            for h in range(PEER_HEADS):
                bs = [b_ref[h, j0 + v * SUBLANES:j0 + (v + 1) * SUBLANES, ls] for v in range(nv)]
                rs = [r_ref[h, j0 + v * SUBLANES:j0 + (v + 1) * SUBLANES, ls] for v in range(nv)]
                for ii in range(nsub):
                    a8 = abc_ref[0, ii, h, :, ls]
                    c8 = abc_ref[1, ii, h, :, ls]
                    for v in range(nv):
                        w[ii][v] = w[ii][v] + jnp.where(rs[v] < c8, a8 * bs[v], 0.0)
            for ii in range(nsub):
                r0 = ii * PEER_KEYS + j0
                stage_ref[plane[2 + hf], r0:r0 + EXPERT_JCHUNK, ll * LANES:(ll + 1) * LANES] = (
                    jnp.concatenate(w[ii], axis=0))
        return w[0][0]

    def stage_block(u_ref, blk):
        for ii in range(nsub):
            i = blk * nsub + ii
            for h in range(PEER_HEADS):
                abc_ref[0, ii, h] = jnp.broadcast_to(a_ref[h, pl.ds(i, 1), :], (SUBLANES, tt))
                abc_ref[1, ii, h] = jnp.broadcast_to(c_ref[h, pl.ds(i, 1), :], (SUBLANES, tt))
        first = gate_weights(0)
        head = (slice(0, 2 * SUBLANES), slice(0, LANES))
        hs_ref[head] = pltpu.bitcast(
            pltpu.bitcast(hs_ref[head], jnp.int32) | zero_bits_after(first), BF16)
        for hf in range(2):
            hs = slice(hf * half, (hf + 1) * half)
            stage_ref[plane[hf]] = jnp.dot(u_ref[...], hs_ref[:, hs],
                                           preferred_element_type=F32)
        for lt in range(1, nlt - 1):
            gate_weights(lt)
        tail = stage_ref[plane[1], eb - SUBLANES:eb, half - LANES:half]
        gate_weights(nlt - 1, pltpu.bitcast(zero_bits_after(tail), F32))

    @pl.when(e == 0)
    def _():
        acc_ref[...] = jnp.zeros(acc_ref.shape, F32)
        hs_ref[...] = h_ref[...]
        stage_block(u0_ref, 0)

    for lt in range(nlt):
        hf, ll = divmod(lt, lt_per_half)
        x = stage_ref[plane[hf], :, ll * LANES:(ll + 1) * LANES]
        gelu = x * (1.0 + lax.erf(x * math.sqrt(0.5)))
        w = stage_ref[plane[2 + hf], :, ll * LANES:(ll + 1) * LANES]
        g_ref[:, lt * LANES:(lt + 1) * LANES] = (w * gelu).astype(BF16)
    for hf in range(2):
        hs = slice(hf * half, (hf + 1) * half)
        acc_ref[:, hs] += jnp.dot(vt_ref[...], g_ref[:, hs], preferred_element_type=F32)

    stage_block(un_ref, jnp.minimum(e + 1, last))

    @pl.when(e == last)
    def _():
        o_ref[...] = x1_ref[...] + acc_ref[...].T


def _peer_experts(h2t, a, c0, b, r1, u, vt, x1):
    tt, eb = EXPERT_TT, EXPERT_EB
    assert SELECT_TT == tt and h2t.shape[2] == tt
    t = h2t.shape[0] * tt
    nblk = PEER_EXPERTS // eb
    sel = pl.BlockSpec((None, PEER_HEADS, PEER_KEYS, tt), lambda i, e: (i, 0, 0, 0))
    row = pl.BlockSpec((tt, D_MODEL), lambda i, e: (i, 0))
    return pl.pallas_call(
        _experts_kernel,
        grid=(t // tt, nblk),
        in_specs=[pl.BlockSpec(memory_space=pltpu.SMEM),
                  pl.BlockSpec((None, D_MODEL, tt), lambda i, e: (i, 0, 0)), sel, sel, sel, sel,
                  pl.BlockSpec((eb, D_MODEL), lambda i, e: (0, 0)),
                  pl.BlockSpec((eb, D_MODEL), lambda i, e: (jnp.minimum(e + 1, nblk - 1), 0)),
                  pl.BlockSpec((None, D_MODEL, eb), lambda i, e: (e, 0, 0)),
                  row],
        out_specs=row,
        out_shape=jax.ShapeDtypeStruct((t, D_MODEL), F32),
        scratch_shapes=[pltpu.VMEM((D_MODEL, tt), F32), pltpu.VMEM((eb, tt), BF16),
                        pltpu.VMEM((2, eb // PEER_KEYS, PEER_HEADS, SUBLANES, tt), F32),
                        pltpu.VMEM((4, eb, tt // 2), F32), pltpu.VMEM((D_MODEL, tt), BF16)],
        compiler_params=_cparams(("parallel", "arbitrary")),
        name="peer_experts",
    )(jnp.zeros((4,), jnp.int32), h2t, a, c0, b, r1, u, u, vt, x1)


def _layer(x2, batch, seq, mix_gain, w_in, gate_bias, q_gain, k_gain, sinks,
           w_up_swa, w_up_sb, w_out, ffn_gain, w_q, sub_keys, u, v):
    qa, ka, va, qb, kb, vb, gl = _inproj(x2, mix_gain.reshape(1, -1), w_in.astype(BF16))
    ya = _swa(qa, ka, va, q_gain, k_gain, sinks, batch, seq)
    yb = _stickbreak(qb, kb, vb, batch, seq)
    x1, h2t, qt = _merge(ya, yb, gl, gate_bias.reshape(1, -1), x2,
                         w_up_swa.astype(BF16), w_up_sb.astype(BF16), w_out.astype(BF16),
                         ffn_gain.reshape(1, -1), w_q.T.astype(BF16))
    a, c0, b, r1 = _peer_select(qt, sub_keys.astype(BF16))
    vt = v.reshape(PEER_EXPERTS // EXPERT_EB, EXPERT_EB, D_MODEL).transpose(0, 2, 1).astype(BF16)
    return _peer_experts(h2t, a, c0, b, r1, u.astype(BF16), vt, x1)


def kernel(x, mix_norm_gain, w_in, gate_bias, swa_q_gain, swa_k_gain, swa_sinks, w_up_swa, w_up_sb,
           w_out, ffn_norm_gain, peer_w_q, peer_sub_keys, peer_u, peer_v):
    batch, seq, d = x.shape
    x2 = x.reshape(batch * seq, d)
    for layer in range(mix_norm_gain.shape[0]):
        x2 = _layer(x2, batch, seq, mix_norm_gain[layer], w_in[layer], gate_bias[layer],
                    swa_q_gain[layer], swa_k_gain[layer], swa_sinks[layer], w_up_swa[layer],
                    w_up_sb[layer], w_out[layer], ffn_norm_gain[layer], peer_w_q[layer],
                    peer_sub_keys[layer], peer_u[layer], peer_v[layer])
    return x2.reshape(batch, seq, d)
```

```python
import functools
import math

import jax
import jax.numpy as jnp
from jax import lax
from jax.experimental import pallas as pl
from jax.experimental.pallas import tpu as pltpu

F32 = jnp.float32
BF16 = jnp.bfloat16

D_MODEL = 1024
HEAD_DIM = 64
CHUNK = 64
RMS_EPS = 1e-6
NEG_INF = -1e30

SWA_Q_HEADS = 8
SWA_GROUP = 4
SWA_WIDTH = 512
SWA_KV_WIDTH = 128
SB_HEADS = 8
SB_WIDTH = 512
GATE_WIDTH = 2 * D_MODEL
IN_WIDTH = SWA_WIDTH + 2 * SWA_KV_WIDTH + 3 * SB_WIDTH + GATE_WIDTH

PEER_HEADS = 8
PEER_KEYS = 128
PEER_HALF = 128
PEER_TOPK = 16
PEER_EXPERTS = PEER_KEYS * PEER_KEYS

LANES = 128
SUBLANES = 8
BLK = 128

INPROJ_TM = 512
MERGE_TM = 256
SELECT_TT = 512
SELECT_LANES = 256
EXPERT_TT = 512
EXPERT_EB = 1024
EXPERT_JCHUNK = 32
SB_QROWS = 2048
VMEM_LIMIT = 56 * 1024 * 1024

_NT = (((1,), (1,)), ((), ()))


def _cparams(sem):
    return pltpu.CompilerParams(dimension_semantics=sem, vmem_limit_bytes=VMEM_LIMIT)


def _inproj_kernel(x_ref, g_ref, w_ref, qa_ref, ka_ref, va_ref, qb_ref, kb_ref, vb_ref, gl_ref):
    x = x_ref[...]
    ms = jnp.mean(x * x, axis=-1, keepdims=True)
    h = (x * lax.rsqrt(ms + RMS_EPS) * g_ref[...]).astype(BF16)

    def proj(lo, hi):
        return jnp.dot(h, w_ref[:, lo:hi], preferred_element_type=F32)

    o = 0
    qa_ref[...] = proj(o, o + SWA_WIDTH); o += SWA_WIDTH
    ka_ref[...] = proj(o, o + SWA_KV_WIDTH); o += SWA_KV_WIDTH
    va_ref[...] = proj(o, o + SWA_KV_WIDTH).astype(BF16); o += SWA_KV_WIDTH
    qb_ref[...] = (proj(o, o + SB_WIDTH) * (HEAD_DIM ** -0.5)).astype(BF16); o += SB_WIDTH
    kb_ref[...] = proj(o, o + SB_WIDTH).astype(BF16); o += SB_WIDTH
    vb_ref[...] = proj(o, o + SB_WIDTH).astype(BF16); o += SB_WIDTH
    gl_ref[...] = proj(o, o + GATE_WIDTH)


def _inproj(x2, gain, w_in):
    t = x2.shape[0]
    tm = INPROJ_TM
    row = lambda w: pl.BlockSpec((tm, w), lambda i: (i, 0))
    full = lambda a: pl.BlockSpec(a.shape, lambda i: (0,) * a.ndim)
    widths = (SWA_WIDTH, SWA_KV_WIDTH, SWA_KV_WIDTH, SB_WIDTH, SB_WIDTH, SB_WIDTH, GATE_WIDTH)
    dtypes = (F32, F32, BF16, BF16, BF16, BF16, F32)
    return pl.pallas_call(
        _inproj_kernel,
        grid=(t // tm,),
        in_specs=[row(D_MODEL), full(gain), full(w_in)],
        out_specs=[row(w) for w in widths],
        out_shape=[jax.ShapeDtypeStruct((t, w), dt) for w, dt in zip(widths, dtypes)],
        compiler_params=_cparams(("parallel",)),
        name="inproj",
    )(x2, gain, w_in)


def _half_rms(x, lo):
    sq = x * x
    s_lo = jnp.sum(jnp.where(lo, sq, 0.0), axis=-1, keepdims=True)
    s_hi = jnp.sum(jnp.where(lo, 0.0, sq), axis=-1, keepdims=True)
    inv = jnp.where(lo, lax.rsqrt(s_lo / HEAD_DIM + RMS_EPS), lax.rsqrt(s_hi / HEAD_DIM + RMS_EPS))
    return x * inv


def _swa_kernel(sink_ref, q_ref, k_ref, v_ref, qg_ref, kg_ref, o_ref,
                kl0, kh0, kl1, kh1, vl0, vh0, vl1, vh1):
    lo = lax.broadcasted_iota(jnp.int32, (BLK, LANES), 1) < HEAD_DIM
    for r in (kl0, kh0, kl1, kh1, vl0, vh0, vl1, vh1):
        r[0:BLK, :] = jnp.zeros((BLK, LANES), BF16)

    qi = lax.broadcasted_iota(jnp.int32, (BLK, 2 * BLK), 0)
    kj = lax.broadcasted_iota(jnp.int32, (BLK, 2 * BLK), 1)
    dist = jnp.abs(qi + BLK - kj).astype(F32)
    qc = qi // CHUNK
    kc = kj // CHUNK
    band = (kc >= qc) & (kc <= qc + 2)

    def block(p, carry):
        cur = pl.ds(pl.multiple_of(p * BLK, BLK), BLK)
        row = pl.ds(pl.multiple_of((p + 1) * BLK, BLK), BLK)
        kn = _half_rms(k_ref[cur, :], lo) * kg_ref[...]
        kr = pltpu.roll(kn, HEAD_DIM, 1)
        v = v_ref[cur, :].astype(F32)
        vr = pltpu.roll(v, HEAD_DIM, 1)
        kl0[row, :] = jnp.where(lo, kn, 0.0).astype(BF16)
        kh0[row, :] = jnp.where(lo, 0.0, kr).astype(BF16)
        kl1[row, :] = jnp.where(lo, kr, 0.0).astype(BF16)
        kh1[row, :] = jnp.where(lo, 0.0, kn).astype(BF16)
        vl0[row, :] = jnp.where(lo, v, 0.0).astype(BF16)
        vh0[row, :] = jnp.where(lo, 0.0, vr).astype(BF16)
        vl1[row, :] = jnp.where(lo, vr, 0.0).astype(BF16)
        vh1[row, :] = jnp.where(lo, 0.0, v).astype(BF16)

        vis = band & ((kj >= BLK) | (p > 0))
        win = pl.ds(pl.multiple_of(p * BLK, BLK), 2 * BLK)
        for m in range(SWA_Q_HEADS // 2):
            g = (2 * m) // SWA_GROUP
            kv = ((kl0, vl0), (kh0, vh0)) if g == 0 else ((kl1, vl1), (kh1, vh1))
            qn = _half_rms(q_ref[cur, m * LANES:(m + 1) * LANES], lo) * qg_ref[...]
            qs = (qn * (HEAD_DIM ** -0.5)).astype(BF16)
            acc = jnp.zeros((BLK, LANES), F32)
            for which, (kref, vref) in enumerate(kv):
                h = 2 * m + which
                slope = 2.0 ** (-(h + 1))
                logits = lax.dot_general(qs, kref[win, :], _NT, preferred_element_type=F32)
                logits = jnp.where(vis, logits - slope * dist, NEG_INF)
                sink = sink_ref[h]
                mx = jnp.maximum(jnp.max(logits, axis=-1, keepdims=True), sink)
                pe = jnp.exp(logits - mx)
                den = jnp.sum(pe, axis=-1, keepdims=True) + jnp.exp(sink - mx)
                probs = (pe / den).astype(BF16)
                acc = acc + jnp.dot(probs, vref[win, :], preferred_element_type=F32)
            o_ref[cur, m * LANES:(m + 1) * LANES] = acc.astype(BF16)
        return carry

    lax.fori_loop(0, q_ref.shape[0] // BLK, block, 0)


def _swa(qa, ka, va, qgain, kgain, sinks, batch, seq):
    qg2 = jnp.concatenate([qgain, qgain]).reshape(1, LANES)
    kg2 = jnp.concatenate([kgain, kgain]).reshape(1, LANES)
    blk = lambda w: pl.BlockSpec((seq, w), lambda b: (b, 0))
    one = pl.BlockSpec((1, LANES), lambda b: (0, 0))
    return pl.pallas_call(
        _swa_kernel,
        grid=(batch,),
        in_specs=[pl.BlockSpec(memory_space=pltpu.SMEM), blk(SWA_WIDTH), blk(SWA_KV_WIDTH),
                  blk(SWA_KV_WIDTH), one, one],
        out_specs=blk(SWA_WIDTH),
        out_shape=jax.ShapeDtypeStruct((batch * seq, SWA_WIDTH), BF16),
        scratch_shapes=[pltpu.VMEM((seq + BLK, LANES), BF16)] * 8,
        compiler_params=_cparams(("parallel",)),
        name="swa",
    )(sinks, qa, ka, va, qg2, kg2)


def _sb_kernel(q_ref, k_ref, v_ref, tri_ref, o_ref, kk, vv, carry, acc):
    seq = q_ref.shape[0]
    nb = seq // BLK
    per = SB_QROWS // BLK
    lo = lax.broadcasted_iota(jnp.int32, (nb, BLK, LANES), 2) < HEAD_DIM
    k3 = k_ref[...].reshape(nb, BLK, LANES)
    v3 = v_ref[...].reshape(nb, BLK, LANES)
    zero = jnp.zeros_like(k3)
    kk[:, 0:BLK, :] = jnp.where(lo, k3, zero)
    kk[:, BLK:2 * BLK, :] = jnp.where(lo, zero, k3)
    vv[:, 0:BLK, :] = jnp.where(lo, v3, zero)
    vv[:, BLK:2 * BLK, :] = jnp.where(lo, zero, v3)

    def tile(qs, kj, d):
        r0 = 0 if d is None else d * BLK
        m = SB_QROWS - r0
        q = q_ref[pl.ds(pl.multiple_of(qs * SB_QROWS + r0, BLK), m), :]
        z2 = lax.dot_general(q, kk[kj], _NT, preferred_element_type=F32)
        if d is not None:
            before = (lax.broadcasted_iota(jnp.int32, (m, BLK), 1)
                      < lax.broadcasted_iota(jnp.int32, (m, BLK), 0))
        ws = []
        for hh in range(2):
            z = z2[:, hh * BLK:(hh + 1) * BLK]
            log_keep = -(jnp.maximum(z, 0.0) + jnp.log(1.0 + jnp.exp(-jnp.abs(z))))
            log_beta = log_keep + z
            if d is not None:
                log_keep = jnp.where(before, log_keep, 0.0)
            hi = log_keep.astype(BF16)
            lw = (log_keep - hi.astype(F32)).astype(BF16)
            sc = jnp.dot(jnp.concatenate([hi, lw], axis=1), tri_ref[...], preferred_element_type=F32)
            suffix = sc[:, :BLK] + carry[hh, r0:, :]
            carry[hh, r0:, :] = carry[hh, r0:, :] + sc[:, BLK:]
            w = jnp.exp(log_beta + suffix)
            if d is not None:
                w = jnp.where(before, w, 0.0)
            ws.append(w.astype(BF16))
        acc[r0:, :] = acc[r0:, :] + jnp.dot(jnp.concatenate(ws, axis=1), vv[kj],
                                           preferred_element_type=F32)

    def qblock(qs, c):
        carry[...] = jnp.zeros(carry.shape, F32)
        acc[...] = jnp.zeros(acc.shape, F32)
        for d in reversed(range(per)):
            tile(qs, qs * per + d, d)

        def inner(t, c2):
            tile(qs, qs * per - 1 - t, None)
            return c2

        lax.fori_loop(0, qs * per, inner, 0)
        o_ref[pl.ds(pl.multiple_of(qs * SB_QROWS, BLK), SB_QROWS), :] = acc[...].astype(BF16)
        return c

    lax.fori_loop(0, seq // SB_QROWS, qblock, 0)


def _suffix_matrix():
    j = jnp.arange(BLK)[:, None]
    s = jnp.arange(BLK)[None, :]
    strict = (j > s).astype(BF16)
    half = jnp.concatenate([strict, jnp.ones((BLK, BLK), BF16)], axis=1)
    return jnp.concatenate([half, half], axis=0)


def _stickbreak(qb, kb, vb, batch, seq):
    npair = SB_HEADS // 2
    blk = pl.BlockSpec((seq, LANES), lambda b, m: (b, m))
    tri = _suffix_matrix()
    return pl.pallas_call(
        _sb_kernel,
        grid=(batch, npair),
        in_specs=[blk, blk, blk, pl.BlockSpec(tri.shape, lambda b, m: (0, 0))],
        out_specs=blk,
        out_shape=jax.ShapeDtypeStruct((batch * seq, SB_WIDTH), BF16),
        scratch_shapes=[pltpu.VMEM((seq // BLK, 2 * BLK, LANES), BF16)] * 2
        + [pltpu.VMEM((2, SB_QROWS, BLK), F32), pltpu.VMEM((SB_QROWS, LANES), F32)],
        compiler_params=_cparams(("parallel", "parallel")),
        name="stickbreak",
    )(qb, kb, vb, tri)


def _merge_kernel(ya_ref, yb_ref, gl_ref, bias_ref, x_ref, wa_ref, wb_ref, wo_ref, g2_ref, wq_ref,
                  x1_ref, h2_ref, qt_ref):
    ua = jnp.dot(ya_ref[...], wa_ref[...], preferred_element_type=F32)
    ub = jnp.dot(yb_ref[...], wb_ref[...], preferred_element_type=F32)
    gates = jax.nn.sigmoid(gl_ref[...] + bias_ref[...])
    merged = gates[:, :D_MODEL] * ua + gates[:, D_MODEL:] * ub
    x1 = x_ref[...] + jnp.dot(merged.astype(BF16), wo_ref[...], preferred_element_type=F32)
    x1_ref[...] = x1
    ms = jnp.mean(x1 * x1, axis=-1, keepdims=True)
    h2t = (x1 * lax.rsqrt(ms + RMS_EPS) * g2_ref[...]).T.astype(BF16)
    h2_ref[...] = h2t
    qt_ref[...] = jnp.dot(wq_ref[...], h2t, preferred_element_type=F32).astype(BF16)


def _merge(ya, yb, gl, bias, x2, wa, wb, wo, g2, wq_t):
    t = x2.shape[0]
    tm = MERGE_TM
    row = lambda w: pl.BlockSpec((tm, w), lambda i: (i, 0))
    full = lambda a: pl.BlockSpec(a.shape, lambda i: (0,) * a.ndim)
    nq = wq_t.shape[0]
    per = EXPERT_TT // tm
    return pl.pallas_call(
        _merge_kernel,
        grid=(t // tm,),
        in_specs=[row(SWA_WIDTH), row(SB_WIDTH), row(GATE_WIDTH), full(bias), row(D_MODEL),
                  full(wa), full(wb), full(wo), full(g2), full(wq_t)],
        out_specs=[row(D_MODEL),
                   pl.BlockSpec((None, D_MODEL, tm), lambda i: (i // per, 0, i % per)),
                   pl.BlockSpec((nq, tm), lambda i: (0, i))],
        out_shape=[jax.ShapeDtypeStruct((t, D_MODEL), F32),
                   jax.ShapeDtypeStruct((t // EXPERT_TT, D_MODEL, EXPERT_TT), BF16),
                   jax.ShapeDtypeStruct((nq, t), BF16)],
        compiler_params=_cparams(("parallel",)),
        name="merge",
    )(ya, yb, gl, bias, x2, wa, wb, wo, g2, wq_t)


_CAND_GROUPS = ((0, 16), (1, 8), (2, 5), (3, 4), (4, 3), (5, 2), (6, 2), (7, 2))
_CAND_ROWS = 16 + 7 * SUBLANES + SUBLANES


def _top16(s, exact_ties):
    n, tt = s.shape
    rank = jnp.full((n, tt), float(PEER_TOPK), F32)
    if exact_ties:
        pos = lax.broadcasted_iota(jnp.int32, (n, tt), 0).astype(F32)
    vals = []
    for r in range(PEER_TOPK):
        m = jnp.max(s, axis=0, keepdims=True)
        hit = s == m
        if exact_ties:
            hit = pos == jnp.min(jnp.where(hit, pos, float(n)), axis=0, keepdims=True)
        rank = jnp.where(hit, float(r), rank)
        s = jnp.where(hit, -jnp.inf, s)
        vals.append(m)
    return vals, rank


def _select_lanes(s0, s1, exact_ties):
    tt = s0.shape[1]
    v0, rank0 = _top16(s0, exact_ties)
    v1, rank1 = _top16(s1, exact_ties)

    row8 = lax.broadcasted_iota(jnp.int32, (SUBLANES, tt), 0)
    stack8 = lambda rows: functools.reduce(
        lambda acc, kv: jnp.where(row8 == kv[0], kv[1], acc), enumerate(rows), jnp.zeros((SUBLANES, tt), F32))
    v1_lo, v1_hi = stack8(v1[:8]), stack8(v1[8:])
    v0_hi = stack8(v0[8:])

    pieces = []
    for r0, n in _CAND_GROUPS:
        lo_piece = v0[r0] + v1_lo
        pieces.append(lo_piece if n >= SUBLANES else jnp.where(row8 < n, lo_piece, -jnp.inf))
        if n > SUBLANES:
            pieces.append(v0[r0] + v1_hi)
    pieces.append(v0_hi + v1[0])
    cand = jnp.concatenate(pieces, axis=0)

    if exact_ties:
        pos = lax.broadcasted_iota(jnp.int32, cand.shape, 0).astype(F32)
    chosen = jnp.zeros(cand.shape, F32)
    c = cand
    for _ in range(PEER_TOPK):
        m = jnp.max(c, axis=0, keepdims=True)
        hit = c == m
        if exact_ties:
            hit = pos == jnp.min(jnp.where(hit, pos, float(_CAND_ROWS)), axis=0, keepdims=True)
        chosen = jnp.where(hit, 1.0, chosen)
        c = jnp.where(hit, -jnp.inf, c)

    best = v0[0] + v1[0]
    z = jnp.sum(jnp.where(chosen > 0.0, jnp.exp(cand - best), 0.0), axis=0, keepdims=True)

    counts = []
    off = 0
    for r0, n in _CAND_GROUPS:
        rows = SUBLANES * ((n + SUBLANES - 1) // SUBLANES)
        counts.append(jnp.sum(chosen[off:off + rows], axis=0, keepdims=True))
        off += rows
    for r in range(SUBLANES):
        counts.append(chosen[off + r:off + r + 1])

    c0 = jnp.zeros(s0.shape, F32)
    for r in range(PEER_TOPK):
        c0 = jnp.where(rank0 == float(r), counts[r], c0)

    taken = lambda rk: jnp.sum(jnp.where(rk < float(PEER_TOPK), 1.0, 0.0), axis=0, keepdims=True)
    unique = ((taken(rank0) == float(PEER_TOPK)) & (taken(rank1) == float(PEER_TOPK))
              & (jnp.sum(chosen, axis=0, keepdims=True) == float(PEER_TOPK)))
    return (jnp.exp(s0 - v0[0]) * (0.5 / z), c0, jnp.exp(s1 - v1[0]), rank1), unique


def _select_kernel(q_ref, sk_ref, a_ref, c_ref, b_ref, r_ref, s_ref):
    passes = q_ref.shape[1] // SELECT_LANES

    def head_pass(n, carry):
        h = n // passes
        ls = pl.ds(pl.multiple_of((n % passes) * SELECT_LANES, SELECT_LANES), SELECT_LANES)

        @pl.when(n % passes == 0)
        def _():
            for p in range(2):
                rows = pl.ds(pl.multiple_of((2 * h + p) * PEER_HALF, PEER_HALF), PEER_HALF)
                s_ref[p] = jnp.dot(sk_ref[h, p], q_ref[rows, :], preferred_element_type=F32)

        def run(exact_ties):
            outs, unique = _select_lanes(s_ref[0, :, ls], s_ref[1, :, ls], exact_ties)
            for ref, val in zip((a_ref, c_ref, b_ref, r_ref), outs):
                ref[h, :, ls] = val
            return unique

        unique = run(False)

        @pl.when(jnp.min(jnp.where(unique, 1.0, 0.0)) < 0.5)
        def _():
            run(True)

        return carry

    lax.fori_loop(0, PEER_HEADS * passes, head_pass, 0)


def _peer_select(qt, sub_keys):
    t = qt.shape[1]
    tt = SELECT_TT
    out = pl.BlockSpec((None, PEER_HEADS, PEER_KEYS, tt), lambda i: (i, 0, 0, 0))
    shp = lambda dt: jax.ShapeDtypeStruct((t // tt, PEER_HEADS, PEER_KEYS, tt), dt)
    return pl.pallas_call(
        _select_kernel,
        grid=(t // tt,),
        in_specs=[pl.BlockSpec((PEER_HEADS * 2 * PEER_HALF, tt), lambda i: (0, i)),
                  pl.BlockSpec(sub_keys.shape, lambda i: (0, 0, 0, 0))],
        out_specs=[out] * 4,
        out_shape=[shp(F32)] * 4,
        scratch_shapes=[pltpu.VMEM((2, PEER_KEYS, tt), F32)],
        compiler_params=_cparams(("parallel",)),
        name="peer_select",
    )(qt, sub_keys)


def _experts_kernel(zero_ref, h_ref, a_ref, c_ref, b_ref, r_ref, u0_ref, un_ref, vt_ref, x1_ref, o_ref,
                    acc_ref, g_ref, abc_ref, stage_ref, hs_ref):
    e = pl.program_id(1)
    last = pl.num_programs(1) - 1
    eb = un_ref.shape[0]
    nsub = eb // PEER_KEYS
    tt = h_ref.shape[1]
    half = tt // 2
    lt_per_half = half // LANES
    nlt = 2 * lt_per_half
    nv = EXPERT_JCHUNK // SUBLANES
    njc = PEER_KEYS // EXPERT_JCHUNK
    plane = [zero_ref[k] + k for k in range(4)]
    zero_bits = jnp.full((SUBLANES, LANES), zero_ref[0], jnp.int32)

    def zero_bits_after(x):
        return pltpu.bitcast(x, jnp.int32) & zero_bits

    def gate_weights(lt, last_chunk_init=None):
        hf, ll = divmod(lt, lt_per_half)
        ls = slice(lt * LANES, (lt + 1) * LANES)
        for jc in range(njc):
            j0 = jc * EXPERT_JCHUNK
            init = jnp.zeros((SUBLANES, LANES), F32)
            if last_chunk_init is not None and jc == njc - 1:
                init = last_chunk_init
            w = [[init] * nv for _ in range(nsub)]
            for h in range(PEER_HEADS):
                bs = [b_ref[h, j0 + v * SUBLANES:j0 + (v + 1) * SUBLANES, ls] for v in range(nv)]
                rs = [r_ref[h, j0 + v * SUBLANES:j0 + (v + 1) * SUBLANES, ls] for v in range(nv)]
                for ii in range(nsub):
                    a8 = abc_ref[0, ii, h, :, ls]
                    c8 = abc_ref[1, ii, h, :, ls]
                    for v in range(nv):
                        w[ii][v] = w[ii][v] + jnp.where(rs[v] < c8, a8 * bs[v], 0.0)
            for ii in range(nsub):
                r0 = ii * PEER_KEYS + j0
                stage_ref[plane[2 + hf], r0:r0 + EXPERT_JCHUNK, ll * LANES:(ll + 1) * LANES] = (
                    jnp.concatenate(w[ii], axis=0))
        return w[0][0]

    def stage_block(u_ref, blk):
        for ii in range(nsub):
            i = blk * nsub + ii
            for h in range(PEER_HEADS):
                abc_ref[0, ii, h] = jnp.broadcast_to(a_ref[h, pl.ds(i, 1), :], (SUBLANES, tt))
                abc_ref[1, ii, h] = jnp.broadcast_to(c_ref[h, pl.ds(i, 1), :], (SUBLANES, tt))
        first = gate_weights(0)
        head = (slice(0, 2 * SUBLANES), slice(0, LANES))
        hs_ref[head] = pltpu.bitcast(
            pltpu.bitcast(hs_ref[head], jnp.int32) | zero_bits_after(first), BF16)
        for hf in range(2):
            hs = slice(hf * half, (hf + 1) * half)
            stage_ref[plane[hf]] = jnp.dot(u_ref[...], hs_ref[:, hs],
                                           preferred_element_type=F32)
        for lt in range(1, nlt - 1):
            gate_weights(lt)
        tail = stage_ref[plane[1], eb - SUBLANES:eb, half - LANES:half]
        gate_weights(nlt - 1, pltpu.bitcast(zero_bits_after(tail), F32))

    @pl.when(e == 0)
    def _():
        acc_ref[...] = jnp.zeros(acc_ref.shape, F32)
        hs_ref[...] = h_ref[...]
        stage_block(u0_ref, 0)

    def consume_block():
        for lt in range(nlt):
            hf, ll = divmod(lt, lt_per_half)
            x = stage_ref[plane[hf], :, ll * LANES:(ll + 1) * LANES]
            gelu = x * (1.0 + lax.erf(x * math.sqrt(0.5)))
            w = stage_ref[plane[2 + hf], :, ll * LANES:(ll + 1) * LANES]
            g_ref[:, lt * LANES:(lt + 1) * LANES] = (w * gelu).astype(BF16)
        for hf in range(2):
            hs = slice(hf * half, (hf + 1) * half)
            acc_ref[:, hs] += jnp.dot(vt_ref[...], g_ref[:, hs], preferred_element_type=F32)

    @pl.when(e < last)
    def _():
        consume_block()
        stage_block(un_ref, e + 1)

    @pl.when(e == last)
    def _():
        consume_block()
        o_ref[...] = x1_ref[...] + acc_ref[...].T


def _peer_experts(h2t, a, c0, b, r1, u, vt, x1):
    tt, eb = EXPERT_TT, EXPERT_EB
    assert SELECT_TT == tt and h2t.shape[2] == tt
    t = h2t.shape[0] * tt
    nblk = PEER_EXPERTS // eb
    sel = pl.BlockSpec((None, PEER_HEADS, PEER_KEYS, tt), lambda i, e: (i, 0, 0, 0))
    row = pl.BlockSpec((tt, D_MODEL), lambda i, e: (i, 0))
    return pl.pallas_call(
        _experts_kernel,
        grid=(t // tt, nblk),
        in_specs=[pl.BlockSpec(memory_space=pltpu.SMEM),
                  pl.BlockSpec((None, D_MODEL, tt), lambda i, e: (i, 0, 0)), sel, sel, sel, sel,
                  pl.BlockSpec((eb, D_MODEL), lambda i, e: (0, 0)),
                  pl.BlockSpec((eb, D_MODEL), lambda i, e: (jnp.minimum(e + 1, nblk - 1), 0)),
                  pl.BlockSpec((None, D_MODEL, eb), lambda i, e: (e, 0, 0)),
                  row],
        out_specs=row,
        out_shape=jax.ShapeDtypeStruct((t, D_MODEL), F32),
        scratch_shapes=[pltpu.VMEM((D_MODEL, tt), F32), pltpu.VMEM((eb, tt), BF16),
                        pltpu.VMEM((2, eb // PEER_KEYS, PEER_HEADS, SUBLANES, tt), F32),
                        pltpu.VMEM((4, eb, tt // 2), F32), pltpu.VMEM((D_MODEL, tt), BF16)],
        compiler_params=_cparams(("parallel", "arbitrary")),
        name="peer_experts",
    )(jnp.zeros((4,), jnp.int32), h2t, a, c0, b, r1, u, u, vt, x1)


def _layer(x2, batch, seq, mix_gain, w_in, gate_bias, q_gain, k_gain, sinks,
           w_up_swa, w_up_sb, w_out, ffn_gain, w_q, sub_keys, u, v):
    qa, ka, va, qb, kb, vb, gl = _inproj(x2, mix_gain.reshape(1, -1), w_in.astype(BF16))
    ya = _swa(qa, ka, va, q_gain, k_gain, sinks, batch, seq)
    yb = _stickbreak(qb, kb, vb, batch, seq)
    x1, h2t, qt = _merge(ya, yb, gl, gate_bias.reshape(1, -1), x2,
                         w_up_swa.astype(BF16), w_up_sb.astype(BF16), w_out.astype(BF16),
                         ffn_gain.reshape(1, -1), w_q.T.astype(BF16))
    a, c0, b, r1 = _peer_select(qt, sub_keys.astype(BF16))
    vt = v.reshape(PEER_EXPERTS // EXPERT_EB, EXPERT_EB, D_MODEL).transpose(0, 2, 1).astype(BF16)
    return _peer_experts(h2t, a, c0, b, r1, u.astype(BF16), vt, x1)


def kernel(x, mix_norm_gain, w_in, gate_bias, swa_q_gain, swa_k_gain, swa_sinks, w_up_swa, w_up_sb,
           w_out, ffn_norm_gain, peer_w_q, peer_sub_keys, peer_u, peer_v):
    batch, seq, d = x.shape
    x2 = x.reshape(batch * seq, d)
    for layer in range(mix_norm_gain.shape[0]):
        x2 = _layer(x2, batch, seq, mix_norm_gain[layer], w_in[layer], gate_bias[layer],
                    swa_q_gain[layer], swa_k_gain[layer], swa_sinks[layer], w_up_swa[layer],
                    w_up_sb[layer], w_out[layer], ffn_norm_gain[layer], peer_w_q[layer],
                    peer_sub_keys[layer], peer_u[layer], peer_v[layer])
    return x2.reshape(batch, seq, d)
```

```python
import functools
import math

import jax
import jax.numpy as jnp
from jax import lax
from jax.experimental import pallas as pl
from jax.experimental.pallas import tpu as pltpu

F32 = jnp.float32
BF16 = jnp.bfloat16

D_MODEL = 1024
HEAD_DIM = 64
CHUNK = 64
RMS_EPS = 1e-6
NEG_INF = -1e30

SWA_Q_HEADS = 8
SWA_GROUP = 4
SWA_WIDTH = 512
SWA_KV_WIDTH = 128
SB_HEADS = 8
SB_WIDTH = 512
GATE_WIDTH = 2 * D_MODEL
IN_WIDTH = SWA_WIDTH + 2 * SWA_KV_WIDTH + 3 * SB_WIDTH + GATE_WIDTH

PEER_HEADS = 8
PEER_KEYS = 128
PEER_HALF = 128
PEER_TOPK = 16
PEER_EXPERTS = PEER_KEYS * PEER_KEYS

LANES = 128
SUBLANES = 8
BLK = 128

INPROJ_TM = 512
MERGE_TM = 512
SELECT_TT = 512
SELECT_LANES = 256
EXPERT_TT = 512
EXPERT_EB = 1024
EXPERT_JCHUNK = 32
SB_QROWS = 2048
VMEM_LIMIT = 56 * 1024 * 1024

_NT = (((1,), (1,)), ((), ()))


def _cparams(sem):
    return pltpu.CompilerParams(dimension_semantics=sem, vmem_limit_bytes=VMEM_LIMIT)


def _inproj_kernel(x_ref, g_ref, w_ref, qa_ref, ka_ref, va_ref, qb_ref, kb_ref, vb_ref, gl_ref):
    x = x_ref[...]
    ms = jnp.mean(x * x, axis=-1, keepdims=True)
    h = (x * lax.rsqrt(ms + RMS_EPS) * g_ref[...]).astype(BF16)

    def proj(lo, hi):
        return jnp.dot(h, w_ref[:, lo:hi], preferred_element_type=F32)

    o = 0
    qa_ref[...] = proj(o, o + SWA_WIDTH); o += SWA_WIDTH
    ka_ref[...] = proj(o, o + SWA_KV_WIDTH); o += SWA_KV_WIDTH
    va_ref[...] = proj(o, o + SWA_KV_WIDTH).astype(BF16); o += SWA_KV_WIDTH
    qb_ref[...] = (proj(o, o + SB_WIDTH) * (HEAD_DIM ** -0.5)).astype(BF16); o += SB_WIDTH
    kb_ref[...] = proj(o, o + SB_WIDTH).astype(BF16); o += SB_WIDTH
    vb_ref[...] = proj(o, o + SB_WIDTH).astype(BF16); o += SB_WIDTH
    gl_ref[...] = proj(o, o + GATE_WIDTH)


def _inproj(x2, gain, w_in):
    t = x2.shape[0]
    tm = INPROJ_TM
    row = lambda w: pl.BlockSpec((tm, w), lambda i: (i, 0))
    full = lambda a: pl.BlockSpec(a.shape, lambda i: (0,) * a.ndim)
    widths = (SWA_WIDTH, SWA_KV_WIDTH, SWA_KV_WIDTH, SB_WIDTH, SB_WIDTH, SB_WIDTH, GATE_WIDTH)
    dtypes = (F32, F32, BF16, BF16, BF16, BF16, F32)
    return pl.pallas_call(
        _inproj_kernel,
        grid=(t // tm,),
        in_specs=[row(D_MODEL), full(gain), full(w_in)],
        out_specs=[row(w) for w in widths],
        out_shape=[jax.ShapeDtypeStruct((t, w), dt) for w, dt in zip(widths, dtypes)],
        compiler_params=_cparams(("parallel",)),
        name="inproj",
    )(x2, gain, w_in)


def _half_rms(x, lo):
    sq = x * x
    s_lo = jnp.sum(jnp.where(lo, sq, 0.0), axis=-1, keepdims=True)
    s_hi = jnp.sum(jnp.where(lo, 0.0, sq), axis=-1, keepdims=True)
    inv = jnp.where(lo, lax.rsqrt(s_lo / HEAD_DIM + RMS_EPS), lax.rsqrt(s_hi / HEAD_DIM + RMS_EPS))
    return x * inv


def _swa_kernel(sink_ref, q_ref, k_ref, v_ref, qg_ref, kg_ref, o_ref,
                kl0, kh0, kl1, kh1, vl0, vh0, vl1, vh1):
    lo = lax.broadcasted_iota(jnp.int32, (BLK, LANES), 1) < HEAD_DIM
    for r in (kl0, kh0, kl1, kh1, vl0, vh0, vl1, vh1):
        r[0:BLK, :] = jnp.zeros((BLK, LANES), BF16)

    qi = lax.broadcasted_iota(jnp.int32, (BLK, 2 * BLK), 0)
    kj = lax.broadcasted_iota(jnp.int32, (BLK, 2 * BLK), 1)
    dist = jnp.abs(qi + BLK - kj).astype(F32)
    qc = qi // CHUNK
    kc = kj // CHUNK
    band = (kc >= qc) & (kc <= qc + 2)

    def prepare(p):
        cur = pl.ds(pl.multiple_of(p * BLK, BLK), BLK)
        row = pl.ds(pl.multiple_of((p + 1) * BLK, BLK), BLK)
        kn = _half_rms(k_ref[cur, :], lo) * kg_ref[...]
        kr = pltpu.roll(kn, HEAD_DIM, 1)
        v = v_ref[cur, :].astype(F32)
        vr = pltpu.roll(v, HEAD_DIM, 1)
        kl0[row, :] = jnp.where(lo, kn, 0.0).astype(BF16)
        kh0[row, :] = jnp.where(lo, 0.0, kr).astype(BF16)
        kl1[row, :] = jnp.where(lo, kr, 0.0).astype(BF16)
        kh1[row, :] = jnp.where(lo, 0.0, kn).astype(BF16)
        vl0[row, :] = jnp.where(lo, v, 0.0).astype(BF16)
        vh0[row, :] = jnp.where(lo, 0.0, vr).astype(BF16)
        vl1[row, :] = jnp.where(lo, vr, 0.0).astype(BF16)
        vh1[row, :] = jnp.where(lo, 0.0, v).astype(BF16)

    def attend(p):
        cur = pl.ds(pl.multiple_of(p * BLK, BLK), BLK)
        vis = band & ((kj >= BLK) | (p > 0))
        win = pl.ds(pl.multiple_of(p * BLK, BLK), 2 * BLK)
        for m in range(SWA_Q_HEADS // 2):
            g = (2 * m) // SWA_GROUP
            kv = ((kl0, vl0), (kh0, vh0)) if g == 0 else ((kl1, vl1), (kh1, vh1))
            qn = _half_rms(q_ref[cur, m * LANES:(m + 1) * LANES], lo) * qg_ref[...]
            qs = (qn * (HEAD_DIM ** -0.5)).astype(BF16)
            acc = jnp.zeros((BLK, LANES), F32)
            for which, (kref, vref) in enumerate(kv):
                h = 2 * m + which
                slope = 2.0 ** (-(h + 1))
                logits = lax.dot_general(qs, kref[win, :], _NT, preferred_element_type=F32)
                logits = jnp.where(vis, logits - slope * dist, NEG_INF)
                sink = sink_ref[h]
                mx = jnp.maximum(jnp.max(logits, axis=-1, keepdims=True), sink)
                pe = jnp.exp(logits - mx)
                den = jnp.sum(pe, axis=-1, keepdims=True) + jnp.exp(sink - mx)
                probs = (pe / den).astype(BF16)
                acc = acc + jnp.dot(probs, vref[win, :], preferred_element_type=F32)
            o_ref[cur, m * LANES:(m + 1) * LANES] = acc.astype(BF16)

    def two_blocks(pp, carry):
        prepare(2 * pp)
        prepare(2 * pp + 1)
        attend(2 * pp)
        attend(2 * pp + 1)
        return carry

    lax.fori_loop(0, q_ref.shape[0] // (2 * BLK), two_blocks, 0)


def _swa(qa, ka, va, qgain, kgain, sinks, batch, seq):
    qg2 = jnp.concatenate([qgain, qgain]).reshape(1, LANES)
    kg2 = jnp.concatenate([kgain, kgain]).reshape(1, LANES)
    blk = lambda w: pl.BlockSpec((seq, w), lambda b: (b, 0))
    one = pl.BlockSpec((1, LANES), lambda b: (0, 0))
    return pl.pallas_call(
        _swa_kernel,
        grid=(batch,),
        in_specs=[pl.BlockSpec(memory_space=pltpu.SMEM), blk(SWA_WIDTH), blk(SWA_KV_WIDTH),
                  blk(SWA_KV_WIDTH), one, one],
        out_specs=blk(SWA_WIDTH),
        out_shape=jax.ShapeDtypeStruct((batch * seq, SWA_WIDTH), BF16),
        scratch_shapes=[pltpu.VMEM((seq + BLK, LANES), BF16)] * 8,
        compiler_params=_cparams(("parallel",)),
        name="swa",
    )(sinks, qa, ka, va, qg2, kg2)


def _sb_kernel(q_ref, k_ref, v_ref, tri_ref, o_ref, kk, vv, carry, acc):
    seq = q_ref.shape[0]
    nb = seq // BLK
    per = SB_QROWS // BLK
    lo = lax.broadcasted_iota(jnp.int32, (nb, BLK, LANES), 2) < HEAD_DIM
    k3 = k_ref[...].reshape(nb, BLK, LANES)
    v3 = v_ref[...].reshape(nb, BLK, LANES)
    zero = jnp.zeros_like(k3)
    kk[:, 0:BLK, :] = jnp.where(lo, k3, zero)
    kk[:, BLK:2 * BLK, :] = jnp.where(lo, zero, k3)
    vv[:, 0:BLK, :] = jnp.where(lo, v3, zero)
    vv[:, BLK:2 * BLK, :] = jnp.where(lo, zero, v3)

    def tile(qs, kj, d):
        r0 = 0 if d is None else d * BLK
        m = SB_QROWS - r0
        q = q_ref[pl.ds(pl.multiple_of(qs * SB_QROWS + r0, BLK), m), :]
        z2 = lax.dot_general(q, kk[kj], _NT, preferred_element_type=F32)
        if d is not None:
            before = (lax.broadcasted_iota(jnp.int32, (m, BLK), 1)
                      < lax.broadcasted_iota(jnp.int32, (m, BLK), 0))
        ws = []
        for hh in range(2):
            z = z2[:, hh * BLK:(hh + 1) * BLK]
            log_keep = -(jnp.maximum(z, 0.0) + jnp.log(1.0 + jnp.exp(-jnp.abs(z))))
            log_beta = log_keep + z
            if d is not None:
                log_keep = jnp.where(before, log_keep, 0.0)
            hi = log_keep.astype(BF16)
            lw = (log_keep - hi.astype(F32)).astype(BF16)
            sc = jnp.dot(jnp.concatenate([hi, lw], axis=1), tri_ref[...], preferred_element_type=F32)
            suffix = sc[:, :BLK] + carry[hh, r0:, :]
            carry[hh, r0:, :] = carry[hh, r0:, :] + sc[:, BLK:]
            w = jnp.exp(log_beta + suffix)
            if d is not None:
                w = jnp.where(before, w, 0.0)
            ws.append(w.astype(BF16))
        acc[r0:, :] = acc[r0:, :] + jnp.dot(jnp.concatenate(ws, axis=1), vv[kj],
                                           preferred_element_type=F32)

    def qblock(qs, c):
        carry[...] = jnp.zeros(carry.shape, F32)
        acc[...] = jnp.zeros(acc.shape, F32)
        for d in reversed(range(per)):
            tile(qs, qs * per + d, d)

        def inner(t, c2):
            tile(qs, qs * per - 1 - t, None)
            return c2

        lax.fori_loop(0, qs * per, inner, 0)
        o_ref[pl.ds(pl.multiple_of(qs * SB_QROWS, BLK), SB_QROWS), :] = acc[...].astype(BF16)
        return c

    lax.fori_loop(0, seq // SB_QROWS, qblock, 0)


def _suffix_matrix():
    j = jnp.arange(BLK)[:, None]
    s = jnp.arange(BLK)[None, :]
    strict = (j > s).astype(BF16)
    half = jnp.concatenate([strict, jnp.ones((BLK, BLK), BF16)], axis=1)
    return jnp.concatenate([half, half], axis=0)


def _stickbreak(qb, kb, vb, batch, seq):
    npair = SB_HEADS // 2
    blk = pl.BlockSpec((seq, LANES), lambda b, m: (b, m))
    tri = _suffix_matrix()
    return pl.pallas_call(
        _sb_kernel,
        grid=(batch, npair),
        in_specs=[blk, blk, blk, pl.BlockSpec(tri.shape, lambda b, m: (0, 0))],
        out_specs=blk,
        out_shape=jax.ShapeDtypeStruct((batch * seq, SB_WIDTH), BF16),
        scratch_shapes=[pltpu.VMEM((seq // BLK, 2 * BLK, LANES), BF16)] * 2
        + [pltpu.VMEM((2, SB_QROWS, BLK), F32), pltpu.VMEM((SB_QROWS, LANES), F32)],
        compiler_params=_cparams(("parallel", "parallel")),
        name="stickbreak",
    )(qb, kb, vb, tri)


def _merge_kernel(ya_ref, yb_ref, gl_ref, bias_ref, x_ref, wa_ref, wb_ref, wo_ref, g2_ref, wq_ref,
                  x1_ref, h2_ref, qt_ref):
    ua = jnp.dot(ya_ref[...], wa_ref[...], preferred_element_type=F32)
    ub = jnp.dot(yb_ref[...], wb_ref[...], preferred_element_type=F32)
    gates = jax.nn.sigmoid(gl_ref[...] + bias_ref[...])
    merged = gates[:, :D_MODEL] * ua + gates[:, D_MODEL:] * ub
    x1 = x_ref[...] + jnp.dot(merged.astype(BF16), wo_ref[...], preferred_element_type=F32)
    x1_ref[...] = x1
    ms = jnp.mean(x1 * x1, axis=-1, keepdims=True)
    h2t = (x1 * lax.rsqrt(ms + RMS_EPS) * g2_ref[...]).T.astype(BF16)
    h2_ref[...] = h2t
    qt_ref[...] = jnp.dot(wq_ref[...], h2t, preferred_element_type=F32).astype(BF16)


def _merge(ya, yb, gl, bias, x2, wa, wb, wo, g2, wq_t):
    t = x2.shape[0]
    tm = MERGE_TM
    row = lambda w: pl.BlockSpec((tm, w), lambda i: (i, 0))
    full = lambda a: pl.BlockSpec(a.shape, lambda i: (0,) * a.ndim)
    nq = wq_t.shape[0]
    per = EXPERT_TT // tm
    return pl.pallas_call(
        _merge_kernel,
        grid=(t // tm,),
        in_specs=[row(SWA_WIDTH), row(SB_WIDTH), row(GATE_WIDTH), full(bias), row(D_MODEL),
                  full(wa), full(wb), full(wo), full(g2), full(wq_t)],
        out_specs=[row(D_MODEL),
                   pl.BlockSpec((None, D_MODEL, tm), lambda i: (i // per, 0, i % per)),
                   pl.BlockSpec((nq, tm), lambda i: (0, i))],
        out_shape=[jax.ShapeDtypeStruct((t, D_MODEL), F32),
                   jax.ShapeDtypeStruct((t // EXPERT_TT, D_MODEL, EXPERT_TT), BF16),
                   jax.ShapeDtypeStruct((nq, t), BF16)],
        compiler_params=_cparams(("parallel",)),
        name="merge",
    )(ya, yb, gl, bias, x2, wa, wb, wo, g2, wq_t)


_CAND_GROUPS = ((0, 16), (1, 8), (2, 5), (3, 4), (4, 3), (5, 2), (6, 2), (7, 2))
_CAND_ROWS = 16 + 7 * SUBLANES + SUBLANES


def _top16(s, exact_ties):
    n, tt = s.shape
    rank = jnp.full((n, tt), float(PEER_TOPK), F32)
    if exact_ties:
        pos = lax.broadcasted_iota(jnp.int32, (n, tt), 0).astype(F32)
    vals = []
    for r in range(PEER_TOPK):
        m = jnp.max(s, axis=0, keepdims=True)
        hit = s == m
        if exact_ties:
            hit = pos == jnp.min(jnp.where(hit, pos, float(n)), axis=0, keepdims=True)
        rank = jnp.where(hit, float(r), rank)
        s = jnp.where(hit, -jnp.inf, s)
        vals.append(m)
    return vals, rank


def _select_lanes(s0, s1, exact_ties):
    tt = s0.shape[1]
    v0, rank0 = _top16(s0, exact_ties)
    v1, rank1 = _top16(s1, exact_ties)

    row8 = lax.broadcasted_iota(jnp.int32, (SUBLANES, tt), 0)
    stack8 = lambda rows: functools.reduce(
        lambda acc, kv: jnp.where(row8 == kv[0], kv[1], acc), enumerate(rows), jnp.zeros((SUBLANES, tt), F32))
    v1_lo, v1_hi = stack8(v1[:8]), stack8(v1[8:])
    v0_hi = stack8(v0[8:])

    pieces = []
    for r0, n in _CAND_GROUPS:
        lo_piece = v0[r0] + v1_lo
        pieces.append(lo_piece if n >= SUBLANES else jnp.where(row8 < n, lo_piece, -jnp.inf))
        if n > SUBLANES:
            pieces.append(v0[r0] + v1_hi)
    pieces.append(v0_hi + v1[0])
    cand = jnp.concatenate(pieces, axis=0)

    if exact_ties:
        pos = lax.broadcasted_iota(jnp.int32, cand.shape, 0).astype(F32)
    chosen = jnp.zeros(cand.shape, F32)
    c = cand
    for _ in range(PEER_TOPK):
        m = jnp.max(c, axis=0, keepdims=True)
        hit = c == m
        if exact_ties:
            hit = pos == jnp.min(jnp.where(hit, pos, float(_CAND_ROWS)), axis=0, keepdims=True)
        chosen = jnp.where(hit, 1.0, chosen)
        c = jnp.where(hit, -jnp.inf, c)

    best = v0[0] + v1[0]
    z = jnp.sum(jnp.where(chosen > 0.0, jnp.exp(cand - best), 0.0), axis=0, keepdims=True)

    counts = []
    off = 0
    for r0, n in _CAND_GROUPS:
        rows = SUBLANES * ((n + SUBLANES - 1) // SUBLANES)
        counts.append(jnp.sum(chosen[off:off + rows], axis=0, keepdims=True))
        off += rows
    for r in range(SUBLANES):
        counts.append(chosen[off + r:off + r + 1])

    c0 = jnp.zeros(s0.shape, F32)
    for r in range(PEER_TOPK):
        c0 = jnp.where(rank0 == float(r), counts[r], c0)

    taken = lambda rk: jnp.sum(jnp.where(rk < float(PEER_TOPK), 1.0, 0.0), axis=0, keepdims=True)
    unique = ((taken(rank0) == float(PEER_TOPK)) & (taken(rank1) == float(PEER_TOPK))
              & (jnp.sum(chosen, axis=0, keepdims=True) == float(PEER_TOPK)))
    return (jnp.exp(s0 - v0[0]) * (0.5 / z), c0, jnp.exp(s1 - v1[0]), rank1), unique


def _select_kernel(q_ref, sk_ref, a_ref, c_ref, b_ref, r_ref, s_ref):
    passes = q_ref.shape[1] // SELECT_LANES

    def head_pass(n, carry):
        h = n // passes
        ls = pl.ds(pl.multiple_of((n % passes) * SELECT_LANES, SELECT_LANES), SELECT_LANES)

        @pl.when(n % passes == 0)
        def _():
            for p in range(2):
                rows = pl.ds(pl.multiple_of((2 * h + p) * PEER_HALF, PEER_HALF), PEER_HALF)
                s_ref[p] = jnp.dot(sk_ref[h, p], q_ref[rows, :], preferred_element_type=F32)

        def run(exact_ties):
            outs, unique = _select_lanes(s_ref[0, :, ls], s_ref[1, :, ls], exact_ties)
            for ref, val in zip((a_ref, c_ref, b_ref, r_ref), outs):
                ref[h, :, ls] = val
            return unique

        unique = run(False)

        @pl.when(jnp.min(jnp.where(unique, 1.0, 0.0)) < 0.5)
        def _():
            run(True)

        return carry

    lax.fori_loop(0, PEER_HEADS * passes, head_pass, 0)


def _peer_select(qt, sub_keys):
    t = qt.shape[1]
    tt = SELECT_TT
    out = pl.BlockSpec((None, PEER_HEADS, PEER_KEYS, tt), lambda i: (i, 0, 0, 0))
    shp = lambda dt: jax.ShapeDtypeStruct((t // tt, PEER_HEADS, PEER_KEYS, tt), dt)
    return pl.pallas_call(
        _select_kernel,
        grid=(t // tt,),
        in_specs=[pl.BlockSpec((PEER_HEADS * 2 * PEER_HALF, tt), lambda i: (0, i)),
                  pl.BlockSpec(sub_keys.shape, lambda i: (0, 0, 0, 0))],
        out_specs=[out] * 4,
        out_shape=[shp(F32)] * 4,
        scratch_shapes=[pltpu.VMEM((2, PEER_KEYS, tt), F32)],
        compiler_params=_cparams(("parallel",)),
        name="peer_select",
    )(qt, sub_keys)


def _experts_kernel(zero_ref, h_ref, a_ref, c_ref, b_ref, r_ref, u0_ref, un_ref, vt_ref, x1_ref, o_ref,
                    acc_ref, g_ref, abc_ref, stage_ref, hs_ref):
    e = pl.program_id(1)
    last = pl.num_programs(1) - 1
    eb = un_ref.shape[0]
    nsub = eb // PEER_KEYS
    tt = h_ref.shape[1]
    half = tt // 2
    lt_per_half = half // LANES
    nlt = 2 * lt_per_half
    nv = EXPERT_JCHUNK // SUBLANES
    njc = PEER_KEYS // EXPERT_JCHUNK
    plane = [zero_ref[k] + k for k in range(4)]
    zero_bits = jnp.full((SUBLANES, LANES), zero_ref[0], jnp.int32)

    def zero_bits_after(x):
        return pltpu.bitcast(x, jnp.int32) & zero_bits

    def gate_weights(lt, last_chunk_init=None):
        hf, ll = divmod(lt, lt_per_half)
        ls = slice(lt * LANES, (lt + 1) * LANES)
        for jc in range(njc):
            j0 = jc * EXPERT_JCHUNK
            init = jnp.zeros((SUBLANES, LANES), F32)
            if last_chunk_init is not None and jc == njc - 1:
                init = last_chunk_init
            w = [[init] * nv for _ in range(nsub)]
            for h in range(PEER_HEADS):
                bs = [b_ref[h, j0 + v * SUBLANES:j0 + (v + 1) * SUBLANES, ls] for v in range(nv)]
                rs = [r_ref[h, j0 + v * SUBLANES:j0 + (v + 1) * SUBLANES, ls] for v in range(nv)]
                for ii in range(nsub):
                    a8 = abc_ref[0, ii, h, :, ls]
                    c8 = abc_ref[1, ii, h, :, ls]
                    for v in range(nv):
                        w[ii][v] = w[ii][v] + jnp.where(rs[v] < c8, a8 * bs[v], 0.0)
            for ii in range(nsub):
                r0 = ii * PEER_KEYS + j0
                stage_ref[plane[2 + hf], r0:r0 + EXPERT_JCHUNK, ll * LANES:(ll + 1) * LANES] = (
                    jnp.concatenate(w[ii], axis=0))
        return w[0][0]

    def stage_block(u_ref, blk):
        for ii in range(nsub):
            i = blk * nsub + ii
            for h in range(PEER_HEADS):
                abc_ref[0, ii, h] = jnp.broadcast_to(a_ref[h, pl.ds(i, 1), :], (SUBLANES, tt))
                abc_ref[1, ii, h] = jnp.broadcast_to(c_ref[h, pl.ds(i, 1), :], (SUBLANES, tt))
        first = gate_weights(0)
        head = (slice(0, 2 * SUBLANES), slice(0, LANES))
        hs_ref[head] = pltpu.bitcast(
            pltpu.bitcast(hs_ref[head], jnp.int32) | zero_bits_after(first), BF16)
        for hf in range(2):
            hs = slice(hf * half, (hf + 1) * half)
            stage_ref[plane[hf]] = jnp.dot(u_ref[...], hs_ref[:, hs],
                                           preferred_element_type=F32)
        for lt in range(1, nlt - 1):
            gate_weights(lt)
        tail = stage_ref[plane[1], eb - SUBLANES:eb, half - LANES:half]
        gate_weights(nlt - 1, pltpu.bitcast(zero_bits_after(tail), F32))

    @pl.when(e == 0)
    def _():
        acc_ref[...] = jnp.zeros(acc_ref.shape, F32)
        hs_ref[...] = h_ref[...]
        stage_block(u0_ref, 0)

    for lt in range(nlt):
        hf, ll = divmod(lt, lt_per_half)
        x = stage_ref[plane[hf], :, ll * LANES:(ll + 1) * LANES]
        gelu = x * (1.0 + lax.erf(x * math.sqrt(0.5)))
        w = stage_ref[plane[2 + hf], :, ll * LANES:(ll + 1) * LANES]
        g_ref[:, lt * LANES:(lt + 1) * LANES] = (w * gelu).astype(BF16)
    for hf in range(2):
        hs = slice(hf * half, (hf + 1) * half)
        acc_ref[:, hs] += jnp.dot(vt_ref[...], g_ref[:, hs], preferred_element_type=F32)

    stage_block(un_ref, jnp.minimum(e + 1, last))

    @pl.when(e == last)
    def _():
        o_ref[...] = x1_ref[...] + acc_ref[...].T


def _peer_experts(h2t, a, c0, b, r1, u, vt, x1):
    tt, eb = EXPERT_TT, EXPERT_EB
    assert SELECT_TT == tt and h2t.shape[2] == tt
    t = h2t.shape[0] * tt
    nblk = PEER_EXPERTS // eb
    sel = pl.BlockSpec((None, PEER_HEADS, PEER_KEYS, tt), lambda i, e: (i, 0, 0, 0))
    row = pl.BlockSpec((tt, D_MODEL), lambda i, e: (i, 0))
    return pl.pallas_call(
        _experts_kernel,
        grid=(t // tt, nblk),
        in_specs=[pl.BlockSpec(memory_space=pltpu.SMEM),
                  pl.BlockSpec((None, D_MODEL, tt), lambda i, e: (i, 0, 0)), sel, sel, sel, sel,
                  pl.BlockSpec((eb, D_MODEL), lambda i, e: (0, 0)),
                  pl.BlockSpec((eb, D_MODEL), lambda i, e: (jnp.minimum(e + 1, nblk - 1), 0)),
                  pl.BlockSpec((None, D_MODEL, eb), lambda i, e: (e, 0, 0)),
                  row],
        out_specs=row,
        out_shape=jax.ShapeDtypeStruct((t, D_MODEL), F32),
        scratch_shapes=[pltpu.VMEM((D_MODEL, tt), F32), pltpu.VMEM((eb, tt), BF16),
                        pltpu.VMEM((2, eb // PEER_KEYS, PEER_HEADS, SUBLANES, tt), F32),
                        pltpu.VMEM((4, eb, tt // 2), F32), pltpu.VMEM((D_MODEL, tt), BF16)],
        compiler_params=_cparams(("parallel", "arbitrary")),
        name="peer_experts",
    )(jnp.zeros((4,), jnp.int32), h2t, a, c0, b, r1, u, u, vt, x1)


def _layer(x2, batch, seq, mix_gain, w_in, gate_bias, q_gain, k_gain, sinks,
           w_up_swa, w_up_sb, w_out, ffn_gain, w_q, sub_keys, u, v):
    qa, ka, va, qb, kb, vb, gl = _inproj(x2, mix_gain.reshape(1, -1), w_in.astype(BF16))
    ya = _swa(qa, ka, va, q_gain, k_gain, sinks, batch, seq)
    yb = _stickbreak(qb, kb, vb, batch, seq)
    x1, h2t, qt = _merge(ya, yb, gl, gate_bias.reshape(1, -1), x2,
                         w_up_swa.astype(BF16), w_up_sb.astype(BF16), w_out.astype(BF16),
                         ffn_gain.reshape(1, -1), w_q.T.astype(BF16))
    a, c0, b, r1 = _peer_select(qt, sub_keys.astype(BF16))
    vt = v.reshape(PEER_EXPERTS // EXPERT_EB, EXPERT_EB, D_MODEL).transpose(0, 2, 1).astype(BF16)
    return _peer_experts(h2t, a, c0, b, r1, u.astype(BF16), vt, x1)


def kernel(x, mix_norm_gain, w_in, gate_bias, swa_q_gain, swa_k_gain, swa_sinks, w_up_swa, w_up_sb,
           w_out, ffn_norm_gain, peer_w_q, peer_sub_keys, peer_u, peer_v):
    batch, seq, d = x.shape
    x2 = x.reshape(batch * seq, d)
    for layer in range(mix_norm_gain.shape[0]):
        x2 = _layer(x2, batch, seq, mix_norm_gain[layer], w_in[layer], gate_bias[layer],
                    swa_q_gain[layer], swa_k_gain[layer], swa_sinks[layer], w_up_swa[layer],
                    w_up_sb[layer], w_out[layer], ffn_norm_gain[layer], peer_w_q[layer],
                    peer_sub_keys[layer], peer_u[layer], peer_v[layer])
    return x2.reshape(batch, seq, d)
```

```python
import functools
import math

import jax
import jax.numpy as jnp
from jax import lax
from jax.experimental import pallas as pl
from jax.experimental.pallas import tpu as pltpu

F32 = jnp.float32
BF16 = jnp.bfloat16

D_MODEL = 1024
HEAD_DIM = 64
CHUNK = 64
RMS_EPS = 1e-6
NEG_INF = -1e30

SWA_Q_HEADS = 8
SWA_GROUP = 4
SWA_WIDTH = 512
SWA_KV_WIDTH = 128
SB_HEADS = 8
SB_WIDTH = 512
GATE_WIDTH = 2 * D_MODEL
IN_WIDTH = SWA_WIDTH + 2 * SWA_KV_WIDTH + 3 * SB_WIDTH + GATE_WIDTH

PEER_HEADS = 8
PEER_KEYS = 128
PEER_HALF = 128
PEER_TOPK = 16
PEER_EXPERTS = PEER_KEYS * PEER_KEYS

LANES = 128
SUBLANES = 8
BLK = 128

INPROJ_TM = 512
MERGE_TM = 512
SELECT_TT = 512
SELECT_LANES = 256
EXPERT_TT = 512
EXPERT_EB = 1024
EXPERT_JCHUNK = 32
SB_QROWS = 2048
VMEM_LIMIT = 56 * 1024 * 1024

_NT = (((1,), (1,)), ((), ()))


def _cparams(sem):
    return pltpu.CompilerParams(dimension_semantics=sem, vmem_limit_bytes=VMEM_LIMIT)


def _inproj_kernel(x_ref, g_ref, w_ref, qa_ref, ka_ref, va_ref, qb_ref, kb_ref, vb_ref, gl_ref):
    x = x_ref[...]
    ms = jnp.mean(x * x, axis=-1, keepdims=True)
    h = (x * lax.rsqrt(ms + RMS_EPS) * g_ref[...]).astype(BF16)

    def proj(lo, hi):
        return jnp.dot(h, w_ref[:, lo:hi], preferred_element_type=F32)

    o = 0
    qa_ref[...] = proj(o, o + SWA_WIDTH); o += SWA_WIDTH
    ka_ref[...] = proj(o, o + SWA_KV_WIDTH); o += SWA_KV_WIDTH
    va_ref[...] = proj(o, o + SWA_KV_WIDTH).astype(BF16); o += SWA_KV_WIDTH
    qb_ref[...] = (proj(o, o + SB_WIDTH) * (HEAD_DIM ** -0.5)).astype(BF16); o += SB_WIDTH
    kb_ref[...] = proj(o, o + SB_WIDTH).astype(BF16); o += SB_WIDTH
    vb_ref[...] = proj(o, o + SB_WIDTH).astype(BF16); o += SB_WIDTH
    gl_ref[...] = proj(o, o + GATE_WIDTH)


def _inproj(x2, gain, w_in):
    t = x2.shape[0]
    tm = INPROJ_TM
    row = lambda w: pl.BlockSpec((tm, w), lambda i: (i, 0))
    full = lambda a: pl.BlockSpec(a.shape, lambda i: (0,) * a.ndim)
    widths = (SWA_WIDTH, SWA_KV_WIDTH, SWA_KV_WIDTH, SB_WIDTH, SB_WIDTH, SB_WIDTH, GATE_WIDTH)
    dtypes = (F32, F32, BF16, BF16, BF16, BF16, F32)
    return pl.pallas_call(
        _inproj_kernel,
        grid=(t // tm,),
        in_specs=[row(D_MODEL), full(gain), full(w_in)],
        out_specs=[row(w) for w in widths],
        out_shape=[jax.ShapeDtypeStruct((t, w), dt) for w, dt in zip(widths, dtypes)],
        compiler_params=_cparams(("parallel",)),
        name="inproj",
    )(x2, gain, w_in)


def _half_rms(x, lo):
    sq = x * x
    s_lo = jnp.sum(jnp.where(lo, sq, 0.0), axis=-1, keepdims=True)
    s_hi = jnp.sum(jnp.where(lo, 0.0, sq), axis=-1, keepdims=True)
    inv = jnp.where(lo, lax.rsqrt(s_lo / HEAD_DIM + RMS_EPS), lax.rsqrt(s_hi / HEAD_DIM + RMS_EPS))
    return x * inv


def _swa_kernel(sink_ref, q_ref, k_ref, v_ref, qg_ref, kg_ref, o_ref,
                kl0, kh0, kl1, kh1, vl0, vh0, vl1, vh1):
    lo = lax.broadcasted_iota(jnp.int32, (BLK, LANES), 1) < HEAD_DIM
    for r in (kl0, kh0, kl1, kh1, vl0, vh0, vl1, vh1):
        r[0:BLK, :] = jnp.zeros((BLK, LANES), BF16)

    qi = lax.broadcasted_iota(jnp.int32, (BLK, 2 * BLK), 0)
    kj = lax.broadcasted_iota(jnp.int32, (BLK, 2 * BLK), 1)
    dist = jnp.abs(qi + BLK - kj).astype(F32)
    qc = qi // CHUNK
    kc = kj // CHUNK
    band = (kc >= qc) & (kc <= qc + 2)

    def prepare(p):
        cur = pl.ds(pl.multiple_of(p * BLK, BLK), BLK)
        row = pl.ds(pl.multiple_of((p + 1) * BLK, BLK), BLK)
        kn = _half_rms(k_ref[cur, :], lo) * kg_ref[...]
        kr = pltpu.roll(kn, HEAD_DIM, 1)
        v = v_ref[cur, :].astype(F32)
        vr = pltpu.roll(v, HEAD_DIM, 1)
        kl0[row, :] = jnp.where(lo, kn, 0.0).astype(BF16)
        kh0[row, :] = jnp.where(lo, 0.0, kr).astype(BF16)
        kl1[row, :] = jnp.where(lo, kr, 0.0).astype(BF16)
        kh1[row, :] = jnp.where(lo, 0.0, kn).astype(BF16)
        vl0[row, :] = jnp.where(lo, v, 0.0).astype(BF16)
        vh0[row, :] = jnp.where(lo, 0.0, vr).astype(BF16)
        vl1[row, :] = jnp.where(lo, vr, 0.0).astype(BF16)
        vh1[row, :] = jnp.where(lo, 0.0, v).astype(BF16)

    def attend(p):
        cur = pl.ds(pl.multiple_of(p * BLK, BLK), BLK)
        vis = band & ((kj >= BLK) | (p > 0))
        win = pl.ds(pl.multiple_of(p * BLK, BLK), 2 * BLK)
        for m in range(SWA_Q_HEADS // 2):
            g = (2 * m) // SWA_GROUP
            kv = ((kl0, vl0), (kh0, vh0)) if g == 0 else ((kl1, vl1), (kh1, vh1))
            qn = _half_rms(q_ref[cur, m * LANES:(m + 1) * LANES], lo) * qg_ref[...]
            qs = (qn * (HEAD_DIM ** -0.5)).astype(BF16)
            acc = jnp.zeros((BLK, LANES), F32)
            for which, (kref, vref) in enumerate(kv):
                h = 2 * m + which
                slope = 2.0 ** (-(h + 1))
                logits = lax.dot_general(qs, kref[win, :], _NT, preferred_element_type=F32)
                logits = jnp.where(vis, logits - slope * dist, NEG_INF)
                sink = sink_ref[h]
                mx = jnp.maximum(jnp.max(logits, axis=-1, keepdims=True), sink)
                pe = jnp.exp(logits - mx)
                den = jnp.sum(pe, axis=-1, keepdims=True) + jnp.exp(sink - mx)
                probs = (pe / den).astype(BF16)
                acc = acc + jnp.dot(probs, vref[win, :], preferred_element_type=F32)
            o_ref[cur, m * LANES:(m + 1) * LANES] = acc.astype(BF16)

    def two_blocks(pp, carry):
        prepare(2 * pp)
        prepare(2 * pp + 1)
        attend(2 * pp)
        attend(2 * pp + 1)
        return carry

    lax.fori_loop(0, q_ref.shape[0] // (2 * BLK), two_blocks, 0)


def _swa(qa, ka, va, qgain, kgain, sinks, batch, seq):
    qg2 = jnp.concatenate([qgain, qgain]).reshape(1, LANES)
    kg2 = jnp.concatenate([kgain, kgain]).reshape(1, LANES)
    blk = lambda w: pl.BlockSpec((seq, w), lambda b: (b, 0))
    one = pl.BlockSpec((1, LANES), lambda b: (0, 0))
    return pl.pallas_call(
        _swa_kernel,
        grid=(batch,),
        in_specs=[pl.BlockSpec(memory_space=pltpu.SMEM), blk(SWA_WIDTH), blk(SWA_KV_WIDTH),
                  blk(SWA_KV_WIDTH), one, one],
        out_specs=blk(SWA_WIDTH),
        out_shape=jax.ShapeDtypeStruct((batch * seq, SWA_WIDTH), BF16),
        scratch_shapes=[pltpu.VMEM((seq + BLK, LANES), BF16)] * 8,
        compiler_params=_cparams(("parallel",)),
        name="swa",
    )(sinks, qa, ka, va, qg2, kg2)


def _sb_kernel(q_ref, k_ref, v_ref, tri_ref, o_ref, kk, vv, carry, acc):
    seq = q_ref.shape[0]
    nb = seq // BLK
    per = SB_QROWS // BLK
    lo = lax.broadcasted_iota(jnp.int32, (nb, BLK, LANES), 2) < HEAD_DIM
    k3 = k_ref[...].reshape(nb, BLK, LANES)
    v3 = v_ref[...].reshape(nb, BLK, LANES)
    zero = jnp.zeros_like(k3)
    kk[:, 0:BLK, :] = jnp.where(lo, k3, zero)
    kk[:, BLK:2 * BLK, :] = jnp.where(lo, zero, k3)
    vv[:, 0:BLK, :] = jnp.where(lo, v3, zero)
    vv[:, BLK:2 * BLK, :] = jnp.where(lo, zero, v3)

    def tile(qs, kj, d):
        r0 = 0 if d is None else d * BLK
        m = SB_QROWS - r0
        q = q_ref[pl.ds(pl.multiple_of(qs * SB_QROWS + r0, BLK), m), :]
        z2 = lax.dot_general(q, kk[kj], _NT, preferred_element_type=F32)
        if d is not None:
            before = (lax.broadcasted_iota(jnp.int32, (m, BLK), 1)
                      < lax.broadcasted_iota(jnp.int32, (m, BLK), 0))
        ws = []
        for hh in range(2):
            z = z2[:, hh * BLK:(hh + 1) * BLK]
            log_keep = -(jnp.maximum(z, 0.0) + jnp.log(1.0 + jnp.exp(-jnp.abs(z))))
            log_beta = log_keep + z
            if d is not None:
                log_keep = jnp.where(before, log_keep, 0.0)
            hi = log_keep.astype(BF16)
            lw = (log_keep - hi.astype(F32)).astype(BF16)
            sc = jnp.dot(jnp.concatenate([hi, lw], axis=1), tri_ref[...], preferred_element_type=F32)
            suffix = sc[:, :BLK] + carry[hh, r0:, :]
            carry[hh, r0:, :] = carry[hh, r0:, :] + sc[:, BLK:]
            w = jnp.exp(log_beta + suffix)
            if d is not None:
                w = jnp.where(before, w, 0.0)
            ws.append(w.astype(BF16))
        acc[r0:, :] = acc[r0:, :] + jnp.dot(jnp.concatenate(ws, axis=1), vv[kj],
                                           preferred_element_type=F32)

    def qblock(qs, c):
        carry[...] = jnp.zeros(carry.shape, F32)
        acc[...] = jnp.zeros(acc.shape, F32)
        for d in reversed(range(per)):
            tile(qs, qs * per + d, d)

        def inner(t, c2):
            tile(qs, qs * per - 1 - t, None)
            return c2

        lax.fori_loop(0, qs * per, inner, 0)
        o_ref[pl.ds(pl.multiple_of(qs * SB_QROWS, BLK), SB_QROWS), :] = acc[...].astype(BF16)
        return c

    lax.fori_loop(0, seq // SB_QROWS, qblock, 0)


def _suffix_matrix():
    j = jnp.arange(BLK)[:, None]
    s = jnp.arange(BLK)[None, :]
    strict = (j > s).astype(BF16)
    half = jnp.concatenate([strict, jnp.ones((BLK, BLK), BF16)], axis=1)
    return jnp.concatenate([half, half], axis=0)


def _stickbreak(qb, kb, vb, batch, seq):
    npair = SB_HEADS // 2
    blk = pl.BlockSpec((seq, LANES), lambda b, m: (b, m))
    tri = _suffix_matrix()
    return pl.pallas_call(
        _sb_kernel,
        grid=(batch, npair),
        in_specs=[blk, blk, blk, pl.BlockSpec(tri.shape, lambda b, m: (0, 0))],
        out_specs=blk,
        out_shape=jax.ShapeDtypeStruct((batch * seq, SB_WIDTH), BF16),
        scratch_shapes=[pltpu.VMEM((seq // BLK, 2 * BLK, LANES), BF16)] * 2
        + [pltpu.VMEM((2, SB_QROWS, BLK), F32), pltpu.VMEM((SB_QROWS, LANES), F32)],
        compiler_params=_cparams(("parallel", "parallel")),
        name="stickbreak",
    )(qb, kb, vb, tri)


def _merge_kernel(ya_ref, yb_ref, gl_ref, bias_ref, x_ref, wa_ref, wb_ref, wo_ref, g2_ref, wq_ref,
                  x1_ref, h2_ref, qt_ref):
    ua = jnp.dot(ya_ref[...], wa_ref[...], preferred_element_type=F32)
    ub = jnp.dot(yb_ref[...], wb_ref[...], preferred_element_type=F32)
    gates = jax.nn.sigmoid(gl_ref[...] + bias_ref[...])
    merged = gates[:, :D_MODEL] * ua + gates[:, D_MODEL:] * ub
    x1 = x_ref[...] + jnp.dot(merged.astype(BF16), wo_ref[...], preferred_element_type=F32)
    x1_ref[...] = x1
    ms = jnp.mean(x1 * x1, axis=-1, keepdims=True)
    h2t = (x1 * lax.rsqrt(ms + RMS_EPS) * g2_ref[...]).T.astype(BF16)
    h2_ref[...] = h2t
    qt_ref[...] = jnp.dot(wq_ref[...], h2t, preferred_element_type=F32).astype(BF16)


def _merge(ya, yb, gl, bias, x2, wa, wb, wo, g2, wq_t):
    t = x2.shape[0]
    tm = MERGE_TM
    row = lambda w: pl.BlockSpec((tm, w), lambda i: (i, 0))
    full = lambda a: pl.BlockSpec(a.shape, lambda i: (0,) * a.ndim)
    nq = wq_t.shape[0]
    per = EXPERT_TT // tm
    return pl.pallas_call(
        _merge_kernel,
        grid=(t // tm,),
        in_specs=[row(SWA_WIDTH), row(SB_WIDTH), row(GATE_WIDTH), full(bias), row(D_MODEL),
                  full(wa), full(wb), full(wo), full(g2), full(wq_t)],
        out_specs=[row(D_MODEL),
                   pl.BlockSpec((None, D_MODEL, tm), lambda i: (i // per, 0, i % per)),
                   pl.BlockSpec((nq, tm), lambda i: (0, i))],
        out_shape=[jax.ShapeDtypeStruct((t, D_MODEL), F32),
                   jax.ShapeDtypeStruct((t // EXPERT_TT, D_MODEL, EXPERT_TT), BF16),
                   jax.ShapeDtypeStruct((nq, t), BF16)],
        compiler_params=_cparams(("parallel",)),
        name="merge",
    )(ya, yb, gl, bias, x2, wa, wb, wo, g2, wq_t)


_CAND_GROUPS = ((0, 16), (1, 8), (2, 5), (3, 4), (4, 3), (5, 2), (6, 2), (7, 2))
_CAND_ROWS = 16 + 7 * SUBLANES + SUBLANES


def _top16(s, exact_ties):
    n, tt = s.shape
    rank = jnp.full((n, tt), float(PEER_TOPK), F32)
    if exact_ties:
        pos = lax.broadcasted_iota(jnp.int32, (n, tt), 0).astype(F32)
    vals = []
    for r in range(PEER_TOPK):
        m = jnp.max(s, axis=0, keepdims=True)
        hit = s == m
        if exact_ties:
            hit = pos == jnp.min(jnp.where(hit, pos, float(n)), axis=0, keepdims=True)
        rank = jnp.where(hit, float(r), rank)
        s = jnp.where(hit, -jnp.inf, s)
        vals.append(m)
    return vals, rank


def _select_lanes(s0, s1, exact_ties):
    tt = s0.shape[1]
    v0, rank0 = _top16(s0, exact_ties)
    v1, rank1 = _top16(s1, exact_ties)

    row8 = lax.broadcasted_iota(jnp.int32, (SUBLANES, tt), 0)
    stack8 = lambda rows: functools.reduce(
        lambda acc, kv: jnp.where(row8 == kv[0], kv[1], acc), enumerate(rows), jnp.zeros((SUBLANES, tt), F32))
    v1_lo, v1_hi = stack8(v1[:8]), stack8(v1[8:])
    v0_hi = stack8(v0[8:])

    pieces = []
    for r0, n in _CAND_GROUPS:
        lo_piece = v0[r0] + v1_lo
        pieces.append(lo_piece if n >= SUBLANES else jnp.where(row8 < n, lo_piece, -jnp.inf))
        if n > SUBLANES:
            pieces.append(v0[r0] + v1_hi)
    pieces.append(v0_hi + v1[0])
    cand = jnp.concatenate(pieces, axis=0)

    if exact_ties:
        pos = lax.broadcasted_iota(jnp.int32, cand.shape, 0).astype(F32)
    chosen = jnp.zeros(cand.shape, F32)
    c = cand
    for _ in range(PEER_TOPK):
        m = jnp.max(c, axis=0, keepdims=True)
        hit = c == m
        if exact_ties:
            hit = pos == jnp.min(jnp.where(hit, pos, float(_CAND_ROWS)), axis=0, keepdims=True)
        chosen = jnp.where(hit, 1.0, chosen)
        c = jnp.where(hit, -jnp.inf, c)

    best = v0[0] + v1[0]
    z = jnp.sum(jnp.where(chosen > 0.0, jnp.exp(cand - best), 0.0), axis=0, keepdims=True)

    counts = []
    off = 0
    for r0, n in _CAND_GROUPS:
        rows = SUBLANES * ((n + SUBLANES - 1) // SUBLANES)
        counts.append(jnp.sum(chosen[off:off + rows], axis=0, keepdims=True))
        off += rows
    for r in range(SUBLANES):
        counts.append(chosen[off + r:off + r + 1])

    c0 = jnp.zeros(s0.shape, F32)
    for r in range(SUBLANES):
        c0 = jnp.where(rank0 == float(r), counts[r], c0)
    rows_used = functools.reduce(lambda acc, cnt: acc + jnp.where(cnt > 0.0, 1.0, 0.0), counts, 0.0)
    c0 = jnp.where(rank0 < rows_used, jnp.maximum(c0, 1.0), c0)

    taken = lambda rk: jnp.sum(jnp.where(rk < float(PEER_TOPK), 1.0, 0.0), axis=0, keepdims=True)
    unique = ((taken(rank0) == float(PEER_TOPK)) & (taken(rank1) == float(PEER_TOPK))
              & (jnp.sum(chosen, axis=0, keepdims=True) == float(PEER_TOPK)))
    return (jnp.exp(s0 - v0[0]) * (0.5 / z), c0, jnp.exp(s1 - v1[0]), rank1), unique


def _select_kernel(q_ref, sk_ref, a_ref, c_ref, b_ref, r_ref, s_ref):
    passes = q_ref.shape[1] // SELECT_LANES

    def head_pass(n, carry):
        h = n // passes
        ls = pl.ds(pl.multiple_of((n % passes) * SELECT_LANES, SELECT_LANES), SELECT_LANES)

        @pl.when(n % passes == 0)
        def _():
            for p in range(2):
                rows = pl.ds(pl.multiple_of((2 * h + p) * PEER_HALF, PEER_HALF), PEER_HALF)
                s_ref[p] = jnp.dot(sk_ref[h, p], q_ref[rows, :], preferred_element_type=F32)

        def run(exact_ties):
            outs, unique = _select_lanes(s_ref[0, :, ls], s_ref[1, :, ls], exact_ties)
            for ref, val in zip((a_ref, c_ref, b_ref, r_ref), outs):
                ref[h, :, ls] = val
            return unique

        unique = run(False)

        @pl.when(jnp.min(jnp.where(unique, 1.0, 0.0)) < 0.5)
        def _():
            run(True)

        return carry

    lax.fori_loop(0, PEER_HEADS * passes, head_pass, 0)


def _peer_select(qt, sub_keys):
    t = qt.shape[1]
    tt = SELECT_TT
    out = pl.BlockSpec((None, PEER_HEADS, PEER_KEYS, tt), lambda i: (i, 0, 0, 0))
    shp = lambda dt: jax.ShapeDtypeStruct((t // tt, PEER_HEADS, PEER_KEYS, tt), dt)
    return pl.pallas_call(
        _select_kernel,
        grid=(t // tt,),
        in_specs=[pl.BlockSpec((PEER_HEADS * 2 * PEER_HALF, tt), lambda i: (0, i)),
                  pl.BlockSpec(sub_keys.shape, lambda i: (0, 0, 0, 0))],
        out_specs=[out] * 4,
        out_shape=[shp(F32)] * 4,
        scratch_shapes=[pltpu.VMEM((2, PEER_KEYS, tt), F32)],
        compiler_params=_cparams(("parallel",)),
        name="peer_select",
    )(qt, sub_keys)


def _experts_kernel(zero_ref, h_ref, a_ref, c_ref, b_ref, r_ref, u0_ref, un_ref, vt_ref, x1_ref, o_ref,
                    acc_ref, g_ref, abc_ref, stage_ref, hs_ref):
    e = pl.program_id(1)
    last = pl.num_programs(1) - 1
    eb = un_ref.shape[0]
    nsub = eb // PEER_KEYS
    tt = h_ref.shape[1]
    half = tt // 2
    lt_per_half = half // LANES
    nlt = 2 * lt_per_half
    nv = EXPERT_JCHUNK // SUBLANES
    njc = PEER_KEYS // EXPERT_JCHUNK
    plane = [zero_ref[k] + k for k in range(4)]
    zero_bits = jnp.full((SUBLANES, LANES), zero_ref[0], jnp.int32)

    def zero_bits_after(x):
        return pltpu.bitcast(x, jnp.int32) & zero_bits

    def gate_weights(lt, last_chunk_init=None):
        hf, ll = divmod(lt, lt_per_half)
        ls = slice(lt * LANES, (lt + 1) * LANES)
        for jc in range(njc):
            j0 = jc * EXPERT_JCHUNK
            init = jnp.zeros((SUBLANES, LANES), F32)
            if last_chunk_init is not None and jc == njc - 1:
                init = last_chunk_init
            w = [[init] * nv for _ in range(nsub)]
            for h in range(PEER_HEADS):
                bs = [b_ref[h, j0 + v * SUBLANES:j0 + (v + 1) * SUBLANES, ls] for v in range(nv)]
                rs = [r_ref[h, j0 + v * SUBLANES:j0 + (v + 1) * SUBLANES, ls] for v in range(nv)]
                for ii in range(nsub):
                    a8 = abc_ref[0, ii, h, :, ls]
                    c8 = abc_ref[1, ii, h, :, ls]
                    for v in range(nv):
                        w[ii][v] = w[ii][v] + jnp.where(rs[v] < c8, a8 * bs[v], 0.0)
            for ii in range(nsub):
                r0 = ii * PEER_KEYS + j0
                stage_ref[plane[2 + hf], r0:r0 + EXPERT_JCHUNK, ll * LANES:(ll + 1) * LANES] = (
                    jnp.concatenate(w[ii], axis=0))
        return w[0][0]

    def stage_block(u_ref, blk):
        for ii in range(nsub):
            i = blk * nsub + ii
            for h in range(PEER_HEADS):
                abc_ref[0, ii, h] = jnp.broadcast_to(a_ref[h, pl.ds(i, 1), :], (SUBLANES, tt))
                abc_ref[1, ii, h] = jnp.broadcast_to(c_ref[h, pl.ds(i, 1), :], (SUBLANES, tt))
        first = gate_weights(0)
        head = (slice(0, 2 * SUBLANES), slice(0, LANES))
        hs_ref[head] = pltpu.bitcast(
            pltpu.bitcast(hs_ref[head], jnp.int32) | zero_bits_after(first), BF16)
        for hf in range(2):
            hs = slice(hf * half, (hf + 1) * half)
            stage_ref[plane[hf]] = jnp.dot(u_ref[...], hs_ref[:, hs],
                                           preferred_element_type=F32)
        for lt in range(1, nlt - 1):
            gate_weights(lt)
        tail = stage_ref[plane[1], eb - SUBLANES:eb, half - LANES:half]
        gate_weights(nlt - 1, pltpu.bitcast(zero_bits_after(tail), F32))

    @pl.when(e == 0)
    def _():
        acc_ref[...] = jnp.zeros(acc_ref.shape, F32)
        hs_ref[...] = h_ref[...]
        stage_block(u0_ref, 0)

    for lt in range(nlt):
        hf, ll = divmod(lt, lt_per_half)
        x = stage_ref[plane[hf], :, ll * LANES:(ll + 1) * LANES]
        gelu = x * (1.0 + lax.erf(x * math.sqrt(0.5)))
        w = stage_ref[plane[2 + hf], :, ll * LANES:(ll + 1) * LANES]
        g_ref[:, lt * LANES:(lt + 1) * LANES] = (w * gelu).astype(BF16)
    for hf in range(2):
        hs = slice(hf * half, (hf + 1) * half)
        acc_ref[:, hs] += jnp.dot(vt_ref[...], g_ref[:, hs], preferred_element_type=F32)

    stage_block(un_ref, jnp.minimum(e + 1, last))

    @pl.when(e == last)
    def _():
        o_ref[...] = x1_ref[...] + acc_ref[...].T


def _peer_experts(h2t, a, c0, b, r1, u, vt, x1):
    tt, eb = EXPERT_TT, EXPERT_EB
    assert SELECT_TT == tt and h2t.shape[2] == tt
    t = h2t.shape[0] * tt
    nblk = PEER_EXPERTS // eb
    sel = pl.BlockSpec((None, PEER_HEADS, PEER_KEYS, tt), lambda i, e: (i, 0, 0, 0))
    row = pl.BlockSpec((tt, D_MODEL), lambda i, e: (i, 0))
    return pl.pallas_call(
        _experts_kernel,
        grid=(t // tt, nblk),
        in_specs=[pl.BlockSpec(memory_space=pltpu.SMEM),
                  pl.BlockSpec((None, D_MODEL, tt), lambda i, e: (i, 0, 0)), sel, sel, sel, sel,
                  pl.BlockSpec((eb, D_MODEL), lambda i, e: (0, 0)),
                  pl.BlockSpec((eb, D_MODEL), lambda i, e: (jnp.minimum(e + 1, nblk - 1), 0)),
                  pl.BlockSpec((None, D_MODEL, eb), lambda i, e: (e, 0, 0)),
                  row],
        out_specs=row,
        out_shape=jax.ShapeDtypeStruct((t, D_MODEL), F32),
        scratch_shapes=[pltpu.VMEM((D_MODEL, tt), F32), pltpu.VMEM((eb, tt), BF16),
                        pltpu.VMEM((2, eb // PEER_KEYS, PEER_HEADS, SUBLANES, tt), F32),
                        pltpu.VMEM((4, eb, tt // 2), F32), pltpu.VMEM((D_MODEL, tt), BF16)],
        compiler_params=_cparams(("parallel", "arbitrary")),
        name="peer_experts",
    )(jnp.zeros((4,), jnp.int32), h2t, a, c0, b, r1, u, u, vt, x1)


def _layer(x2, batch, seq, mix_gain, w_in, gate_bias, q_gain, k_gain, sinks,
           w_up_swa, w_up_sb, w_out, ffn_gain, w_q, sub_keys, u, v):
    qa, ka, va, qb, kb, vb, gl = _inproj(x2, mix_gain.reshape(1, -1), w_in.astype(BF16))
    ya = _swa(qa, ka, va, q_gain, k_gain, sinks, batch, seq)
    yb = _stickbreak(qb, kb, vb, batch, seq)
    x1, h2t, qt = _merge(ya, yb, gl, gate_bias.reshape(1, -1), x2,
                         w_up_swa.astype(BF16), w_up_sb.astype(BF16), w_out.astype(BF16),
                         ffn_gain.reshape(1, -1), w_q.T.astype(BF16))
    a, c0, b, r1 = _peer_select(qt, sub_keys.astype(BF16))
    vt = v.reshape(PEER_EXPERTS // EXPERT_EB, EXPERT_EB, D_MODEL).transpose(0, 2, 1).astype(BF16)
    return _peer_experts(h2t, a, c0, b, r1, u.astype(BF16), vt, x1)


def kernel(x, mix_norm_gain, w_in, gate_bias, swa_q_gain, swa_k_gain, swa_sinks, w_up_swa, w_up_sb,
           w_out, ffn_norm_gain, peer_w_q, peer_sub_keys, peer_u, peer_v):
    batch, seq, d = x.shape
    x2 = x.reshape(batch * seq, d)
    for layer in range(mix_norm_gain.shape[0]):
        x2 = _layer(x2, batch, seq, mix_norm_gain[layer], w_in[layer], gate_bias[layer],
                    swa_q_gain[layer], swa_k_gain[layer], swa_sinks[layer], w_up_swa[layer],
                    w_up_sb[layer], w_out[layer], ffn_norm_gain[layer], peer_w_q[layer],
                    peer_sub_keys[layer], peer_u[layer], peer_v[layer])
    return x2.reshape(batch, seq, d)
```

```python
import functools
import math

import jax
import jax.numpy as jnp
from jax import lax
from jax.experimental import pallas as pl
from jax.experimental.pallas import tpu as pltpu

F32 = jnp.float32
BF16 = jnp.bfloat16

D_MODEL = 1024
HEAD_DIM = 64
CHUNK = 64
RMS_EPS = 1e-6
NEG_INF = -1e30

SWA_Q_HEADS = 8
SWA_GROUP = 4
SWA_WIDTH = 512
SWA_KV_WIDTH = 128
SB_HEADS = 8
SB_WIDTH = 512
GATE_WIDTH = 2 * D_MODEL
IN_WIDTH = SWA_WIDTH + 2 * SWA_KV_WIDTH + 3 * SB_WIDTH + GATE_WIDTH

PEER_HEADS = 8
PEER_KEYS = 128
PEER_HALF = 128
PEER_TOPK = 16
PEER_EXPERTS = PEER_KEYS * PEER_KEYS

LANES = 128
SUBLANES = 8
BLK = 128

INPROJ_TM = 512
MERGE_TM = 512
SELECT_TT = 512
SELECT_LANES = 512
EXPERT_TT = 512
EXPERT_EB = 1024
EXPERT_JCHUNK = 32
SB_QROWS = 2048
VMEM_LIMIT = 56 * 1024 * 1024

_NT = (((1,), (1,)), ((), ()))


def _cparams(sem):
    return pltpu.CompilerParams(dimension_semantics=sem, vmem_limit_bytes=VMEM_LIMIT)


def _inproj_kernel(x_ref, g_ref, w_ref, qa_ref, ka_ref, va_ref, qb_ref, kb_ref, vb_ref, gl_ref):
    x = x_ref[...]
    ms = jnp.mean(x * x, axis=-1, keepdims=True)
    h = (x * lax.rsqrt(ms + RMS_EPS) * g_ref[...]).astype(BF16)

    def proj(lo, hi):
        return jnp.dot(h, w_ref[:, lo:hi], preferred_element_type=F32)

    o = 0
    qa_ref[...] = proj(o, o + SWA_WIDTH); o += SWA_WIDTH
    ka_ref[...] = proj(o, o + SWA_KV_WIDTH); o += SWA_KV_WIDTH
    va_ref[...] = proj(o, o + SWA_KV_WIDTH).astype(BF16); o += SWA_KV_WIDTH
    qb_ref[...] = (proj(o, o + SB_WIDTH) * (HEAD_DIM ** -0.5)).astype(BF16); o += SB_WIDTH
    kb_ref[...] = proj(o, o + SB_WIDTH).astype(BF16); o += SB_WIDTH
    vb_ref[...] = proj(o, o + SB_WIDTH).astype(BF16); o += SB_WIDTH
    gl_ref[...] = proj(o, o + GATE_WIDTH)


def _inproj(x2, gain, w_in):
    t = x2.shape[0]
    tm = INPROJ_TM
    row = lambda w: pl.BlockSpec((tm, w), lambda i: (i, 0))
    full = lambda a: pl.BlockSpec(a.shape, lambda i: (0,) * a.ndim)
    widths = (SWA_WIDTH, SWA_KV_WIDTH, SWA_KV_WIDTH, SB_WIDTH, SB_WIDTH, SB_WIDTH, GATE_WIDTH)
    dtypes = (F32, F32, BF16, BF16, BF16, BF16, F32)
    return pl.pallas_call(
        _inproj_kernel,
        grid=(t // tm,),
        in_specs=[row(D_MODEL), full(gain), full(w_in)],
        out_specs=[row(w) for w in widths],
        out_shape=[jax.ShapeDtypeStruct((t, w), dt) for w, dt in zip(widths, dtypes)],
        compiler_params=_cparams(("parallel",)),
        name="inproj",
    )(x2, gain, w_in)


def _half_rms(x, lo):
    sq = x * x
    s_lo = jnp.sum(jnp.where(lo, sq, 0.0), axis=-1, keepdims=True)
    s_hi = jnp.sum(jnp.where(lo, 0.0, sq), axis=-1, keepdims=True)
    inv = jnp.where(lo, lax.rsqrt(s_lo / HEAD_DIM + RMS_EPS), lax.rsqrt(s_hi / HEAD_DIM + RMS_EPS))
    return x * inv


def _swa_kernel(sink_ref, q_ref, k_ref, v_ref, qg_ref, kg_ref, o_ref,
                kl0, kh0, kl1, kh1, vl0, vh0, vl1, vh1):
    lo = lax.broadcasted_iota(jnp.int32, (BLK, LANES), 1) < HEAD_DIM
    for r in (kl0, kh0, kl1, kh1, vl0, vh0, vl1, vh1):
        r[0:BLK, :] = jnp.zeros((BLK, LANES), BF16)

    qi = lax.broadcasted_iota(jnp.int32, (BLK, 2 * BLK), 0)
    kj = lax.broadcasted_iota(jnp.int32, (BLK, 2 * BLK), 1)
    dist = jnp.abs(qi + BLK - kj).astype(F32)
    qc = qi // CHUNK
    kc = kj // CHUNK
    band = (kc >= qc) & (kc <= qc + 2)

    def prepare(p):
        cur = pl.ds(pl.multiple_of(p * BLK, BLK), BLK)
        row = pl.ds(pl.multiple_of((p + 1) * BLK, BLK), BLK)
        kn = _half_rms(k_ref[cur, :], lo) * kg_ref[...]
        kr = pltpu.roll(kn, HEAD_DIM, 1)
        v = v_ref[cur, :].astype(F32)
        vr = pltpu.roll(v, HEAD_DIM, 1)
        kl0[row, :] = jnp.where(lo, kn, 0.0).astype(BF16)
        kh0[row, :] = jnp.where(lo, 0.0, kr).astype(BF16)
        kl1[row, :] = jnp.where(lo, kr, 0.0).astype(BF16)
        kh1[row, :] = jnp.where(lo, 0.0, kn).astype(BF16)
        vl0[row, :] = jnp.where(lo, v, 0.0).astype(BF16)
        vh0[row, :] = jnp.where(lo, 0.0, vr).astype(BF16)
        vl1[row, :] = jnp.where(lo, vr, 0.0).astype(BF16)
        vh1[row, :] = jnp.where(lo, 0.0, v).astype(BF16)

    def attend(p):
        cur = pl.ds(pl.multiple_of(p * BLK, BLK), BLK)
        vis = band & ((kj >= BLK) | (p > 0))
        win = pl.ds(pl.multiple_of(p * BLK, BLK), 2 * BLK)
        for m in range(SWA_Q_HEADS // 2):
            g = (2 * m) // SWA_GROUP
            kv = ((kl0, vl0), (kh0, vh0)) if g == 0 else ((kl1, vl1), (kh1, vh1))
            qn = _half_rms(q_ref[cur, m * LANES:(m + 1) * LANES], lo) * qg_ref[...]
            qs = (qn * (HEAD_DIM ** -0.5)).astype(BF16)
            acc = jnp.zeros((BLK, LANES), F32)
            for which, (kref, vref) in enumerate(kv):
                h = 2 * m + which
                slope = 2.0 ** (-(h + 1))
                logits = lax.dot_general(qs, kref[win, :], _NT, preferred_element_type=F32)
                logits = jnp.where(vis, logits - slope * dist, NEG_INF)
                sink = sink_ref[h]
                mx = jnp.maximum(jnp.max(logits, axis=-1, keepdims=True), sink)
                pe = jnp.exp(logits - mx)
                den = jnp.sum(pe, axis=-1, keepdims=True) + jnp.exp(sink - mx)
                probs = (pe / den).astype(BF16)
                acc = acc + jnp.dot(probs, vref[win, :], preferred_element_type=F32)
            o_ref[cur, m * LANES:(m + 1) * LANES] = acc.astype(BF16)

    def two_blocks(pp, carry):
        prepare(2 * pp)
        prepare(2 * pp + 1)
        attend(2 * pp)
        attend(2 * pp + 1)
        return carry

    lax.fori_loop(0, q_ref.shape[0] // (2 * BLK), two_blocks, 0)


def _swa(qa, ka, va, qgain, kgain, sinks, batch, seq):
    qg2 = jnp.concatenate([qgain, qgain]).reshape(1, LANES)
    kg2 = jnp.concatenate([kgain, kgain]).reshape(1, LANES)
    blk = lambda w: pl.BlockSpec((seq, w), lambda b: (b, 0))
    one = pl.BlockSpec((1, LANES), lambda b: (0, 0))
    return pl.pallas_call(
        _swa_kernel,
        grid=(batch,),
        in_specs=[pl.BlockSpec(memory_space=pltpu.SMEM), blk(SWA_WIDTH), blk(SWA_KV_WIDTH),
                  blk(SWA_KV_WIDTH), one, one],
        out_specs=blk(SWA_WIDTH),
        out_shape=jax.ShapeDtypeStruct((batch * seq, SWA_WIDTH), BF16),
        scratch_shapes=[pltpu.VMEM((seq + BLK, LANES), BF16)] * 8,
        compiler_params=_cparams(("parallel",)),
        name="swa",
    )(sinks, qa, ka, va, qg2, kg2)


def _sb_kernel(q_ref, k_ref, v_ref, tri_ref, o_ref, kk, vv, carry, acc):
    seq = q_ref.shape[0]
    nb = seq // BLK
    per = SB_QROWS // BLK
    lo = lax.broadcasted_iota(jnp.int32, (nb, BLK, LANES), 2) < HEAD_DIM
    k3 = k_ref[...].reshape(nb, BLK, LANES)
    v3 = v_ref[...].reshape(nb, BLK, LANES)
    zero = jnp.zeros_like(k3)
    kk[:, 0:BLK, :] = jnp.where(lo, k3, zero)
    kk[:, BLK:2 * BLK, :] = jnp.where(lo, zero, k3)
    vv[:, 0:BLK, :] = jnp.where(lo, v3, zero)
    vv[:, BLK:2 * BLK, :] = jnp.where(lo, zero, v3)

    def tile(qs, kj, d):
        r0 = 0 if d is None else d * BLK
        m = SB_QROWS - r0
        q = q_ref[pl.ds(pl.multiple_of(qs * SB_QROWS + r0, BLK), m), :]
        z2 = lax.dot_general(q, kk[kj], _NT, preferred_element_type=F32)
        if d is not None:
            before = (lax.broadcasted_iota(jnp.int32, (m, BLK), 1)
                      < lax.broadcasted_iota(jnp.int32, (m, BLK), 0))
        ws = []
        for hh in range(2):
            z = z2[:, hh * BLK:(hh + 1) * BLK]
            log_keep = -(jnp.maximum(z, 0.0) + jnp.log(1.0 + jnp.exp(-jnp.abs(z))))
            log_beta = log_keep + z
            if d is not None:
                log_keep = jnp.where(before, log_keep, 0.0)
            hi = log_keep.astype(BF16)
            lw = (log_keep - hi.astype(F32)).astype(BF16)
            sc = jnp.dot(jnp.concatenate([hi, lw], axis=1), tri_ref[...], preferred_element_type=F32)
            suffix = sc[:, :BLK] + carry[hh, r0:, :]
            carry[hh, r0:, :] = carry[hh, r0:, :] + sc[:, BLK:]
            w = jnp.exp(log_beta + suffix)
            if d is not None:
                w = jnp.where(before, w, 0.0)
            ws.append(w.astype(BF16))
        acc[r0:, :] = acc[r0:, :] + jnp.dot(jnp.concatenate(ws, axis=1), vv[kj],
                                           preferred_element_type=F32)

    def qblock(qs, c):
        carry[...] = jnp.zeros(carry.shape, F32)
        acc[...] = jnp.zeros(acc.shape, F32)
        for d in reversed(range(per)):
            tile(qs, qs * per + d, d)

        def inner(t, c2):
            tile(qs, qs * per - 1 - t, None)
            return c2

        lax.fori_loop(0, qs * per, inner, 0)
        o_ref[pl.ds(pl.multiple_of(qs * SB_QROWS, BLK), SB_QROWS), :] = acc[...].astype(BF16)
        return c

    lax.fori_loop(0, seq // SB_QROWS, qblock, 0)


def _suffix_matrix():
    j = jnp.arange(BLK)[:, None]
    s = jnp.arange(BLK)[None, :]
    strict = (j > s).astype(BF16)
    half = jnp.concatenate([strict, jnp.ones((BLK, BLK), BF16)], axis=1)
    return jnp.concatenate([half, half], axis=0)


def _stickbreak(qb, kb, vb, batch, seq):
    npair = SB_HEADS // 2
    blk = pl.BlockSpec((seq, LANES), lambda b, m: (b, m))
    tri = _suffix_matrix()
    return pl.pallas_call(
        _sb_kernel,
        grid=(batch, npair),
        in_specs=[blk, blk, blk, pl.BlockSpec(tri.shape, lambda b, m: (0, 0))],
        out_specs=blk,
        out_shape=jax.ShapeDtypeStruct((batch * seq, SB_WIDTH), BF16),
        scratch_shapes=[pltpu.VMEM((seq // BLK, 2 * BLK, LANES), BF16)] * 2
        + [pltpu.VMEM((2, SB_QROWS, BLK), F32), pltpu.VMEM((SB_QROWS, LANES), F32)],
        compiler_params=_cparams(("parallel", "parallel")),
        name="stickbreak",
    )(qb, kb, vb, tri)


def _merge_kernel(ya_ref, yb_ref, gl_ref, bias_ref, x_ref, wa_ref, wb_ref, wo_ref, g2_ref, wq_ref,
                  x1_ref, h2_ref, qt_ref):
    ua = jnp.dot(ya_ref[...], wa_ref[...], preferred_element_type=F32)
    ub = jnp.dot(yb_ref[...], wb_ref[...], preferred_element_type=F32)
    gates = jax.nn.sigmoid(gl_ref[...] + bias_ref[...])
    merged = gates[:, :D_MODEL] * ua + gates[:, D_MODEL:] * ub
    x1 = x_ref[...] + jnp.dot(merged.astype(BF16), wo_ref[...], preferred_element_type=F32)
    x1_ref[...] = x1
    ms = jnp.mean(x1 * x1, axis=-1, keepdims=True)
    h2t = (x1 * lax.rsqrt(ms + RMS_EPS) * g2_ref[...]).T.astype(BF16)
    h2_ref[...] = h2t
    qt_ref[...] = jnp.dot(wq_ref[...], h2t, preferred_element_type=F32).astype(BF16)


def _merge(ya, yb, gl, bias, x2, wa, wb, wo, g2, wq_t):
    t = x2.shape[0]
    tm = MERGE_TM
    row = lambda w: pl.BlockSpec((tm, w), lambda i: (i, 0))
    full = lambda a: pl.BlockSpec(a.shape, lambda i: (0,) * a.ndim)
    nq = wq_t.shape[0]
    per = EXPERT_TT // tm
    return pl.pallas_call(
        _merge_kernel,
        grid=(t // tm,),
        in_specs=[row(SWA_WIDTH), row(SB_WIDTH), row(GATE_WIDTH), full(bias), row(D_MODEL),
                  full(wa), full(wb), full(wo), full(g2), full(wq_t)],
        out_specs=[row(D_MODEL),
                   pl.BlockSpec((None, D_MODEL, tm), lambda i: (i // per, 0, i % per)),
                   pl.BlockSpec((nq, tm), lambda i: (0, i))],
        out_shape=[jax.ShapeDtypeStruct((t, D_MODEL), F32),
                   jax.ShapeDtypeStruct((t // EXPERT_TT, D_MODEL, EXPERT_TT), BF16),
                   jax.ShapeDtypeStruct((nq, t), BF16)],
        compiler_params=_cparams(("parallel",)),
        name="merge",
    )(ya, yb, gl, bias, x2, wa, wb, wo, g2, wq_t)


_CAND_GROUPS = ((0, 16), (1, 8), (2, 5), (3, 4), (4, 3), (5, 2), (6, 2), (7, 2))
_CAND_ROWS = 16 + 7 * SUBLANES + SUBLANES


def _top16(s, exact_ties):
    n, tt = s.shape
    rank = jnp.full((n, tt), float(PEER_TOPK), F32)
    if exact_ties:
        pos = lax.broadcasted_iota(jnp.int32, (n, tt), 0).astype(F32)
    vals = []
    for r in range(PEER_TOPK):
        m = jnp.max(s, axis=0, keepdims=True)
        hit = s == m
        if exact_ties:
            hit = pos == jnp.min(jnp.where(hit, pos, float(n)), axis=0, keepdims=True)
        rank = jnp.where(hit, float(r), rank)
        s = jnp.where(hit, -jnp.inf, s)
        vals.append(m)
    return vals, rank


def _select_lanes(s0, s1, exact_ties):
    tt = s0.shape[1]
    v0, rank0 = _top16(s0, exact_ties)
    v1, rank1 = _top16(s1, exact_ties)

    row8 = lax.broadcasted_iota(jnp.int32, (SUBLANES, tt), 0)
    stack8 = lambda rows: functools.reduce(
        lambda acc, kv: jnp.where(row8 == kv[0], kv[1], acc), enumerate(rows), jnp.zeros((SUBLANES, tt), F32))
    v1_lo, v1_hi = stack8(v1[:8]), stack8(v1[8:])
    v0_hi = stack8(v0[8:])

    pieces = []
    for r0, n in _CAND_GROUPS:
        lo_piece = v0[r0] + v1_lo
        pieces.append(lo_piece if n >= SUBLANES else jnp.where(row8 < n, lo_piece, -jnp.inf))
        if n > SUBLANES:
            pieces.append(v0[r0] + v1_hi)
    pieces.append(v0_hi + v1[0])
    cand = jnp.concatenate(pieces, axis=0)

    if exact_ties:
        pos = lax.broadcasted_iota(jnp.int32, cand.shape, 0).astype(F32)
    chosen = jnp.zeros(cand.shape, F32)
    c = cand
    for _ in range(PEER_TOPK):
        m = jnp.max(c, axis=0, keepdims=True)
        hit = c == m
        if exact_ties:
            hit = pos == jnp.min(jnp.where(hit, pos, float(_CAND_ROWS)), axis=0, keepdims=True)
        chosen = jnp.where(hit, 1.0, chosen)
        c = jnp.where(hit, -jnp.inf, c)

    best = v0[0] + v1[0]
    z = jnp.sum(jnp.where(chosen > 0.0, jnp.exp(cand - best), 0.0), axis=0, keepdims=True)

    counts = []
    off = 0
    for r0, n in _CAND_GROUPS:
        rows = SUBLANES * ((n + SUBLANES - 1) // SUBLANES)
        counts.append(jnp.sum(chosen[off:off + rows], axis=0, keepdims=True))
        off += rows
    for r in range(SUBLANES):
        counts.append(chosen[off + r:off + r + 1])

    c0 = jnp.zeros(s0.shape, F32)
    for r in range(SUBLANES):
        c0 = jnp.where(rank0 == float(r), counts[r], c0)
    rows_used = functools.reduce(lambda acc, cnt: acc + jnp.where(cnt > 0.0, 1.0, 0.0), counts, 0.0)
    c0 = jnp.where(rank0 < rows_used, jnp.maximum(c0, 1.0), c0)

    taken = lambda rk: jnp.sum(jnp.where(rk < float(PEER_TOPK), 1.0, 0.0), axis=0, keepdims=True)
    unique = ((taken(rank0) == float(PEER_TOPK)) & (taken(rank1) == float(PEER_TOPK))
              & (jnp.sum(chosen, axis=0, keepdims=True) == float(PEER_TOPK)))
    return (jnp.exp(s0 - v0[0]) * (0.5 / z), c0, jnp.exp(s1 - v1[0]), rank1), unique


def _select_kernel(q_ref, sk_ref, a_ref, c_ref, b_ref, r_ref, s_ref):
    passes = q_ref.shape[1] // SELECT_LANES

    def head_pass(n, carry):
        h = n // passes
        ls = pl.ds(pl.multiple_of((n % passes) * SELECT_LANES, SELECT_LANES), SELECT_LANES)

        @pl.when(n % passes == 0)
        def _():
            for p in range(2):
                rows = pl.ds(pl.multiple_of((2 * h + p) * PEER_HALF, PEER_HALF), PEER_HALF)
                s_ref[p] = jnp.dot(sk_ref[h, p], q_ref[rows, :], preferred_element_type=F32)

        def run(exact_ties):
            outs, unique = _select_lanes(s_ref[0, :, ls], s_ref[1, :, ls], exact_ties)
            for ref, val in zip((a_ref, c_ref, b_ref, r_ref), outs):
                ref[h, :, ls] = val
            return unique

        unique = run(False)

        @pl.when(jnp.min(jnp.where(unique, 1.0, 0.0)) < 0.5)
        def _():
            run(True)

        return carry

    lax.fori_loop(0, PEER_HEADS * passes, head_pass, 0)


def _peer_select(qt, sub_keys):
    t = qt.shape[1]
    tt = SELECT_TT
    out = pl.BlockSpec((None, PEER_HEADS, PEER_KEYS, tt), lambda i: (i, 0, 0, 0))
    shp = lambda dt: jax.ShapeDtypeStruct((t // tt, PEER_HEADS, PEER_KEYS, tt), dt)
    return pl.pallas_call(
        _select_kernel,
        grid=(t // tt,),
        in_specs=[pl.BlockSpec((PEER_HEADS * 2 * PEER_HALF, tt), lambda i: (0, i)),
                  pl.BlockSpec(sub_keys.shape, lambda i: (0, 0, 0, 0))],
        out_specs=[out] * 4,
        out_shape=[shp(F32)] * 4,
        scratch_shapes=[pltpu.VMEM((2, PEER_KEYS, tt), F32)],
        compiler_params=_cparams(("parallel",)),
        name="peer_select",
    )(qt, sub_keys)


def _experts_kernel(zero_ref, h_ref, a_ref, c_ref, b_ref, r_ref, u0_ref, un_ref, vt_ref, x1_ref, o_ref,
                    acc_ref, g_ref, abc_ref, stage_ref, hs_ref):
    e = pl.program_id(1)
    last = pl.num_programs(1) - 1
    eb = un_ref.shape[0]
    nsub = eb // PEER_KEYS
    tt = h_ref.shape[1]
    half = tt // 2
    lt_per_half = half // LANES
    nlt = 2 * lt_per_half
    nv = EXPERT_JCHUNK // SUBLANES
    njc = PEER_KEYS // EXPERT_JCHUNK
    plane = [zero_ref[k] + k for k in range(4)]
    zero_bits = jnp.full((SUBLANES, LANES), zero_ref[0], jnp.int32)

    def zero_bits_after(x):
        return pltpu.bitcast(x, jnp.int32) & zero_bits

    def gate_weights(lt, last_chunk_init=None):
        hf, ll = divmod(lt, lt_per_half)
        ls = slice(lt * LANES, (lt + 1) * LANES)
        for jc in range(njc):
            j0 = jc * EXPERT_JCHUNK
            init = jnp.zeros((SUBLANES, LANES), F32)
            if last_chunk_init is not None and jc == njc - 1:
                init = last_chunk_init
            w = [[init] * nv for _ in range(nsub)]
            for h in range(PEER_HEADS):
                bs = [b_ref[h, j0 + v * SUBLANES:j0 + (v + 1) * SUBLANES, ls] for v in range(nv)]
                rs = [r_ref[h, j0 + v * SUBLANES:j0 + (v + 1) * SUBLANES, ls] for v in range(nv)]
                for ii in range(nsub):
                    a8 = abc_ref[0, ii, h, :, ls]
                    c8 = abc_ref[1, ii, h, :, ls]
                    for v in range(nv):
                        w[ii][v] = w[ii][v] + jnp.where(rs[v] < c8, a8 * bs[v], 0.0)
            for ii in range(nsub):
                r0 = ii * PEER_KEYS + j0
                stage_ref[plane[2 + hf], r0:r0 + EXPERT_JCHUNK, ll * LANES:(ll + 1) * LANES] = (
                    jnp.concatenate(w[ii], axis=0))
        return w[0][0]

    def stage_block(u_ref, blk):
        for ii in range(nsub):
            i = blk * nsub + ii
            for h in range(PEER_HEADS):
                abc_ref[0, ii, h] = jnp.broadcast_to(a_ref[h, pl.ds(i, 1), :], (SUBLANES, tt))
                abc_ref[1, ii, h] = jnp.broadcast_to(c_ref[h, pl.ds(i, 1), :], (SUBLANES, tt))
        first = gate_weights(0)
        head = (slice(0, 2 * SUBLANES), slice(0, LANES))
        hs_ref[head] = pltpu.bitcast(
            pltpu.bitcast(hs_ref[head], jnp.int32) | zero_bits_after(first), BF16)
        for hf in range(2):
            hs = slice(hf * half, (hf + 1) * half)
            stage_ref[plane[hf]] = jnp.dot(u_ref[...], hs_ref[:, hs],
                                           preferred_element_type=F32)
        for lt in range(1, nlt - 1):
            gate_weights(lt)
        tail = stage_ref[plane[1], eb - SUBLANES:eb, half - LANES:half]
        gate_weights(nlt - 1, pltpu.bitcast(zero_bits_after(tail), F32))

    @pl.when(e == 0)
    def _():
        acc_ref[...] = jnp.zeros(acc_ref.shape, F32)
        hs_ref[...] = h_ref[...]
        stage_block(u0_ref, 0)

    for lt in range(nlt):
        hf, ll = divmod(lt, lt_per_half)
        x = stage_ref[plane[hf], :, ll * LANES:(ll + 1) * LANES]
        gelu = x * (1.0 + lax.erf(x * math.sqrt(0.5)))
        w = stage_ref[plane[2 + hf], :, ll * LANES:(ll + 1) * LANES]
        g_ref[:, lt * LANES:(lt + 1) * LANES] = (w * gelu).astype(BF16)
    for hf in range(2):
        hs = slice(hf * half, (hf + 1) * half)
        acc_ref[:, hs] += jnp.dot(vt_ref[...], g_ref[:, hs], preferred_element_type=F32)

    stage_block(un_ref, jnp.minimum(e + 1, last))

    @pl.when(e == last)
    def _():
        o_ref[...] = x1_ref[...] + acc_ref[...].T


def _peer_experts(h2t, a, c0, b, r1, u, vt, x1):
    tt, eb = EXPERT_TT, EXPERT_EB
    assert SELECT_TT == tt and h2t.shape[2] == tt
    t = h2t.shape[0] * tt
    nblk = PEER_EXPERTS // eb
    sel = pl.BlockSpec((None, PEER_HEADS, PEER_KEYS, tt), lambda i, e: (i, 0, 0, 0))
    row = pl.BlockSpec((tt, D_MODEL), lambda i, e: (i, 0))
    return pl.pallas_call(
        _experts_kernel,
        grid=(t // tt, nblk),
        in_specs=[pl.BlockSpec(memory_space=pltpu.SMEM),
                  pl.BlockSpec((None, D_MODEL, tt), lambda i, e: (i, 0, 0)), sel, sel, sel, sel,
                  pl.BlockSpec((eb, D_MODEL), lambda i, e: (0, 0)),
                  pl.BlockSpec((eb, D_MODEL), lambda i, e: (jnp.minimum(e + 1, nblk - 1), 0)),
                  pl.BlockSpec((None, D_MODEL, eb), lambda i, e: (e, 0, 0)),
                  row],
        out_specs=row,
        out_shape=jax.ShapeDtypeStruct((t, D_MODEL), F32),
        scratch_shapes=[pltpu.VMEM((D_MODEL, tt), F32), pltpu.VMEM((eb, tt), BF16),
                        pltpu.VMEM((2, eb // PEER_KEYS, PEER_HEADS, SUBLANES, tt), F32),
                        pltpu.VMEM((4, eb, tt // 2), F32), pltpu.VMEM((D_MODEL, tt), BF16)],
        compiler_params=_cparams(("parallel", "arbitrary")),
        name="peer_experts",
    )(jnp.zeros((4,), jnp.int32), h2t, a, c0, b, r1, u, u, vt, x1)


def _layer(x2, batch, seq, mix_gain, w_in, gate_bias, q_gain, k_gain, sinks,
           w_up_swa, w_up_sb, w_out, ffn_gain, w_q, sub_keys, u, v):
    qa, ka, va, qb, kb, vb, gl = _inproj(x2, mix_gain.reshape(1, -1), w_in.astype(BF16))
    ya = _swa(qa, ka, va, q_gain, k_gain, sinks, batch, seq)
    yb = _stickbreak(qb, kb, vb, batch, seq)
    x1, h2t, qt = _merge(ya, yb, gl, gate_bias.reshape(1, -1), x2,
                         w_up_swa.astype(BF16), w_up_sb.astype(BF16), w_out.astype(BF16),
                         ffn_gain.reshape(1, -1), w_q.T.astype(BF16))
    a, c0, b, r1 = _peer_select(qt, sub_keys.astype(BF16))
    vt = v.reshape(PEER_EXPERTS // EXPERT_EB, EXPERT_EB, D_MODEL).transpose(0, 2, 1).astype(BF16)
    return _peer_experts(h2t, a, c0, b, r1, u.astype(BF16), vt, x1)


def kernel(x, mix_norm_gain, w_in, gate_bias, swa_q_gain, swa_k_gain, swa_sinks, w_up_swa, w_up_sb,
           w_out, ffn_norm_gain, peer_w_q, peer_sub_keys, peer_u, peer_v):
    batch, seq, d = x.shape
    x2 = x.reshape(batch * seq, d)
    for layer in range(mix_norm_gain.shape[0]):
        x2 = _layer(x2, batch, seq, mix_norm_gain[layer], w_in[layer], gate_bias[layer],
                    swa_q_gain[layer], swa_k_gain[layer], swa_sinks[layer], w_up_swa[layer],
                    w_up_sb[layer], w_out[layer], ffn_norm_gain[layer], peer_w_q[layer],
                    peer_sub_keys[layer], peer_u[layer], peer_v[layer])
    return x2.reshape(batch, seq, d)
```

```python
import functools
import math

import jax
import jax.numpy as jnp
from jax import lax
from jax.experimental import pallas as pl
from jax.experimental.pallas import tpu as pltpu

F32 = jnp.float32
BF16 = jnp.bfloat16

D_MODEL = 1024
HEAD_DIM = 64
CHUNK = 64
RMS_EPS = 1e-6
NEG_INF = -1e30

SWA_Q_HEADS = 8
SWA_GROUP = 4
SWA_WIDTH = 512
SWA_KV_WIDTH = 128
SB_HEADS = 8
SB_WIDTH = 512
GATE_WIDTH = 2 * D_MODEL
IN_WIDTH = SWA_WIDTH + 2 * SWA_KV_WIDTH + 3 * SB_WIDTH + GATE_WIDTH

PEER_HEADS = 8
PEER_KEYS = 128
PEER_HALF = 128
PEER_TOPK = 16
PEER_EXPERTS = PEER_KEYS * PEER_KEYS

LANES = 128
SUBLANES = 8
BLK = 128

INPROJ_TM = 512
MERGE_TM = 512
SELECT_TT = 512
SELECT_LANES = 512
EXPERT_TT = 512
EXPERT_EB = 1024
EXPERT_JCHUNK = 32
SB_QROWS = 2048
VMEM_LIMIT = 56 * 1024 * 1024

_NT = (((1,), (1,)), ((), ()))


def _cparams(sem):
    return pltpu.CompilerParams(dimension_semantics=sem, vmem_limit_bytes=VMEM_LIMIT)


def _inproj_kernel(x_ref, g_ref, w_ref, qa_ref, ka_ref, va_ref, qb_ref, kb_ref, vb_ref):
    x = x_ref[...]
    ms = jnp.mean(x * x, axis=-1, keepdims=True)
    h = (x * lax.rsqrt(ms + RMS_EPS) * g_ref[...]).astype(BF16)

    def proj(lo, hi):
        return jnp.dot(h, w_ref[:, lo:hi], preferred_element_type=F32)

    o = 0
    qa_ref[...] = proj(o, o + SWA_WIDTH); o += SWA_WIDTH
    ka_ref[...] = proj(o, o + SWA_KV_WIDTH); o += SWA_KV_WIDTH
    va_ref[...] = proj(o, o + SWA_KV_WIDTH).astype(BF16); o += SWA_KV_WIDTH
    qb_ref[...] = (proj(o, o + SB_WIDTH) * (HEAD_DIM ** -0.5)).astype(BF16); o += SB_WIDTH
    kb_ref[...] = proj(o, o + SB_WIDTH).astype(BF16); o += SB_WIDTH
    vb_ref[...] = proj(o, o + SB_WIDTH).astype(BF16); o += SB_WIDTH


def _inproj(x2, gain, w_in):
    t = x2.shape[0]
    tm = INPROJ_TM
    row = lambda w: pl.BlockSpec((tm, w), lambda i: (i, 0))
    full = lambda a: pl.BlockSpec(a.shape, lambda i: (0,) * a.ndim)
    widths = (SWA_WIDTH, SWA_KV_WIDTH, SWA_KV_WIDTH, SB_WIDTH, SB_WIDTH, SB_WIDTH)
    dtypes = (F32, F32, BF16, BF16, BF16, BF16)
    return pl.pallas_call(
        _inproj_kernel,
        grid=(t // tm,),
        in_specs=[row(D_MODEL), full(gain), full(w_in)],
        out_specs=[row(w) for w in widths],
        out_shape=[jax.ShapeDtypeStruct((t, w), dt) for w, dt in zip(widths, dtypes)],
        compiler_params=_cparams(("parallel",)),
        name="inproj",
    )(x2, gain, w_in)


def _half_rms(x, lo):
    sq = x * x
    s_lo = jnp.sum(jnp.where(lo, sq, 0.0), axis=-1, keepdims=True)
    s_hi = jnp.sum(jnp.where(lo, 0.0, sq), axis=-1, keepdims=True)
    inv = jnp.where(lo, lax.rsqrt(s_lo / HEAD_DIM + RMS_EPS), lax.rsqrt(s_hi / HEAD_DIM + RMS_EPS))
    return x * inv


def _swa_kernel(sink_ref, q_ref, k_ref, v_ref, qg_ref, kg_ref, o_ref,
                kl0, kh0, kl1, kh1, vl0, vh0, vl1, vh1):
    lo = lax.broadcasted_iota(jnp.int32, (BLK, LANES), 1) < HEAD_DIM
    for r in (kl0, kh0, kl1, kh1, vl0, vh0, vl1, vh1):
        r[0:BLK, :] = jnp.zeros((BLK, LANES), BF16)

    qi = lax.broadcasted_iota(jnp.int32, (BLK, 2 * BLK), 0)
    kj = lax.broadcasted_iota(jnp.int32, (BLK, 2 * BLK), 1)
    dist = jnp.abs(qi + BLK - kj).astype(F32)
    qc = qi // CHUNK
    kc = kj // CHUNK
    band = (kc >= qc) & (kc <= qc + 2)

    def prepare(p):
        cur = pl.ds(pl.multiple_of(p * BLK, BLK), BLK)
        row = pl.ds(pl.multiple_of((p + 1) * BLK, BLK), BLK)
        kn = _half_rms(k_ref[cur, :], lo) * kg_ref[...]
        kr = pltpu.roll(kn, HEAD_DIM, 1)
        v = v_ref[cur, :].astype(F32)
        vr = pltpu.roll(v, HEAD_DIM, 1)
        kl0[row, :] = jnp.where(lo, kn, 0.0).astype(BF16)
        kh0[row, :] = jnp.where(lo, 0.0, kr).astype(BF16)
        kl1[row, :] = jnp.where(lo, kr, 0.0).astype(BF16)
        kh1[row, :] = jnp.where(lo, 0.0, kn).astype(BF16)
        vl0[row, :] = jnp.where(lo, v, 0.0).astype(BF16)
        vh0[row, :] = jnp.where(lo, 0.0, vr).astype(BF16)
        vl1[row, :] = jnp.where(lo, vr, 0.0).astype(BF16)
        vh1[row, :] = jnp.where(lo, 0.0, v).astype(BF16)

    def attend(p):
        cur = pl.ds(pl.multiple_of(p * BLK, BLK), BLK)
        vis = band & ((kj >= BLK) | (p > 0))
        win = pl.ds(pl.multiple_of(p * BLK, BLK), 2 * BLK)
        for m in range(SWA_Q_HEADS // 2):
            g = (2 * m) // SWA_GROUP
            kv = ((kl0, vl0), (kh0, vh0)) if g == 0 else ((kl1, vl1), (kh1, vh1))
            qn = _half_rms(q_ref[cur, m * LANES:(m + 1) * LANES], lo) * qg_ref[...]
            qs = (qn * (HEAD_DIM ** -0.5)).astype(BF16)
            acc = jnp.zeros((BLK, LANES), F32)
            for which, (kref, vref) in enumerate(kv):
                h = 2 * m + which
                slope = 2.0 ** (-(h + 1))
                logits = lax.dot_general(qs, kref[win, :], _NT, preferred_element_type=F32)
                logits = jnp.where(vis, logits - slope * dist, NEG_INF)
                sink = sink_ref[h]
                mx = jnp.maximum(jnp.max(logits, axis=-1, keepdims=True), sink)
                pe = jnp.exp(logits - mx)
                den = jnp.sum(pe, axis=-1, keepdims=True) + jnp.exp(sink - mx)
                probs = (pe / den).astype(BF16)
                acc = acc + jnp.dot(probs, vref[win, :], preferred_element_type=F32)
            o_ref[cur, m * LANES:(m + 1) * LANES] = acc.astype(BF16)

    def two_blocks(pp, carry):
        prepare(2 * pp)
        prepare(2 * pp + 1)
        attend(2 * pp)
        attend(2 * pp + 1)
        return carry

    lax.fori_loop(0, q_ref.shape[0] // (2 * BLK), two_blocks, 0)


def _swa(qa, ka, va, qgain, kgain, sinks, batch, seq):
    qg2 = jnp.concatenate([qgain, qgain]).reshape(1, LANES)
    kg2 = jnp.concatenate([kgain, kgain]).reshape(1, LANES)
    blk = lambda w: pl.BlockSpec((seq, w), lambda b: (b, 0))
    one = pl.BlockSpec((1, LANES), lambda b: (0, 0))
    return pl.pallas_call(
        _swa_kernel,
        grid=(batch,),
        in_specs=[pl.BlockSpec(memory_space=pltpu.SMEM), blk(SWA_WIDTH), blk(SWA_KV_WIDTH),
                  blk(SWA_KV_WIDTH), one, one],
        out_specs=blk(SWA_WIDTH),
        out_shape=jax.ShapeDtypeStruct((batch * seq, SWA_WIDTH), BF16),
        scratch_shapes=[pltpu.VMEM((seq + BLK, LANES), BF16)] * 8,
        compiler_params=_cparams(("parallel",)),
        name="swa",
    )(sinks, qa, ka, va, qg2, kg2)


def _sb_kernel(q_ref, k_ref, v_ref, tri_ref, o_ref, kk, vv, carry, acc):
    seq = q_ref.shape[0]
    nb = seq // BLK
    per = SB_QROWS // BLK
    lo = lax.broadcasted_iota(jnp.int32, (nb, BLK, LANES), 2) < HEAD_DIM
    k3 = k_ref[...].reshape(nb, BLK, LANES)
    v3 = v_ref[...].reshape(nb, BLK, LANES)
    zero = jnp.zeros_like(k3)
    kk[:, 0:BLK, :] = jnp.where(lo, k3, zero)
    kk[:, BLK:2 * BLK, :] = jnp.where(lo, zero, k3)
    vv[:, 0:BLK, :] = jnp.where(lo, v3, zero)
    vv[:, BLK:2 * BLK, :] = jnp.where(lo, zero, v3)

    def tile(qs, kj, d):
        r0 = 0 if d is None else d * BLK
        m = SB_QROWS - r0
        q = q_ref[pl.ds(pl.multiple_of(qs * SB_QROWS + r0, BLK), m), :]
        z2 = lax.dot_general(q, kk[kj], _NT, preferred_element_type=F32)
        if d is not None:
            before = (lax.broadcasted_iota(jnp.int32, (m, BLK), 1)
                      < lax.broadcasted_iota(jnp.int32, (m, BLK), 0))
        ws = []
        for hh in range(2):
            z = z2[:, hh * BLK:(hh + 1) * BLK]
            log_keep = -(jnp.maximum(z, 0.0) + jnp.log(1.0 + jnp.exp(-jnp.abs(z))))
            log_beta = log_keep + z
            if d is not None:
                log_keep = jnp.where(before, log_keep, 0.0)
            hi = log_keep.astype(BF16)
            lw = (log_keep - hi.astype(F32)).astype(BF16)
            sc = jnp.dot(jnp.concatenate([hi, lw], axis=1), tri_ref[...], preferred_element_type=F32)
            suffix = sc[:, :BLK] + carry[hh, r0:, :]
            carry[hh, r0:, :] = carry[hh, r0:, :] + sc[:, BLK:]
            w = jnp.exp(log_beta + suffix)
            if d is not None:
                w = jnp.where(before, w, 0.0)
            ws.append(w.astype(BF16))
        acc[r0:, :] = acc[r0:, :] + jnp.dot(jnp.concatenate(ws, axis=1), vv[kj],
                                           preferred_element_type=F32)

    def qblock(qs, c):
        carry[...] = jnp.zeros(carry.shape, F32)
        acc[...] = jnp.zeros(acc.shape, F32)
        for d in reversed(range(per)):
            tile(qs, qs * per + d, d)

        def inner(t, c2):
            tile(qs, qs * per - 1 - t, None)
            return c2

        lax.fori_loop(0, qs * per, inner, 0)
        o_ref[pl.ds(pl.multiple_of(qs * SB_QROWS, BLK), SB_QROWS), :] = acc[...].astype(BF16)
        return c

    lax.fori_loop(0, seq // SB_QROWS, qblock, 0)


def _suffix_matrix():
    j = jnp.arange(BLK)[:, None]
    s = jnp.arange(BLK)[None, :]
    strict = (j > s).astype(BF16)
    half = jnp.concatenate([strict, jnp.ones((BLK, BLK), BF16)], axis=1)
    return jnp.concatenate([half, half], axis=0)


def _stickbreak(qb, kb, vb, batch, seq):
    npair = SB_HEADS // 2
    blk = pl.BlockSpec((seq, LANES), lambda b, m: (b, m))
    tri = _suffix_matrix()
    return pl.pallas_call(
        _sb_kernel,
        grid=(batch, npair),
        in_specs=[blk, blk, blk, pl.BlockSpec(tri.shape, lambda b, m: (0, 0))],
        out_specs=blk,
        out_shape=jax.ShapeDtypeStruct((batch * seq, SB_WIDTH), BF16),
        scratch_shapes=[pltpu.VMEM((seq // BLK, 2 * BLK, LANES), BF16)] * 2
        + [pltpu.VMEM((2, SB_QROWS, BLK), F32), pltpu.VMEM((SB_QROWS, LANES), F32)],
        compiler_params=_cparams(("parallel", "parallel")),
        name="stickbreak",
    )(qb, kb, vb, tri)


def _merge_kernel(ya_ref, yb_ref, g1_ref, wg_ref, bias_ref, x_ref, wa_ref, wb_ref, wo_ref, g2_ref,
                  wq_ref, x1_ref, h2_ref, qt_ref):
    ua = jnp.dot(ya_ref[...], wa_ref[...], preferred_element_type=F32)
    ub = jnp.dot(yb_ref[...], wb_ref[...], preferred_element_type=F32)
    x = x_ref[...]
    h = (x * lax.rsqrt(jnp.mean(x * x, axis=-1, keepdims=True) + RMS_EPS) * g1_ref[...]).astype(BF16)
    gates = jax.nn.sigmoid(jnp.dot(h, wg_ref[...], preferred_element_type=F32) + bias_ref[...])
    merged = gates[:, :D_MODEL] * ua + gates[:, D_MODEL:] * ub
    x1 = x + jnp.dot(merged.astype(BF16), wo_ref[...], preferred_element_type=F32)
    x1_ref[...] = x1
    ms = jnp.mean(x1 * x1, axis=-1, keepdims=True)
    h2t = (x1 * lax.rsqrt(ms + RMS_EPS) * g2_ref[...]).T.astype(BF16)
    h2_ref[...] = h2t
    qt_ref[...] = jnp.dot(wq_ref[...], h2t, preferred_element_type=F32).astype(BF16)


def _merge(ya, yb, g1, wg, bias, x2, wa, wb, wo, g2, wq_t):
    t = x2.shape[0]
    tm = MERGE_TM
    row = lambda w: pl.BlockSpec((tm, w), lambda i: (i, 0))
    full = lambda a: pl.BlockSpec(a.shape, lambda i: (0,) * a.ndim)
    nq = wq_t.shape[0]
    per = EXPERT_TT // tm
    return pl.pallas_call(
        _merge_kernel,
        grid=(t // tm,),
        in_specs=[row(SWA_WIDTH), row(SB_WIDTH), full(g1), full(wg), full(bias), row(D_MODEL),
                  full(wa), full(wb), full(wo), full(g2), full(wq_t)],
        out_specs=[row(D_MODEL),
                   pl.BlockSpec((None, D_MODEL, tm), lambda i: (i // per, 0, i % per)),
                   pl.BlockSpec((nq, tm), lambda i: (0, i))],
        out_shape=[jax.ShapeDtypeStruct((t, D_MODEL), F32),
                   jax.ShapeDtypeStruct((t // EXPERT_TT, D_MODEL, EXPERT_TT), BF16),
                   jax.ShapeDtypeStruct((nq, t), BF16)],
        compiler_params=_cparams(("parallel",)),
        name="merge",
    )(ya, yb, g1, wg, bias, x2, wa, wb, wo, g2, wq_t)


_CAND_GROUPS = ((0, 16), (1, 8), (2, 5), (3, 4), (4, 3), (5, 2), (6, 2), (7, 2))
_CAND_ROWS = 16 + 7 * SUBLANES + SUBLANES


def _top16(s, exact_ties):
    n, tt = s.shape
    rank = jnp.full((n, tt), float(PEER_TOPK), F32)
    if exact_ties:
        pos = lax.broadcasted_iota(jnp.int32, (n, tt), 0).astype(F32)
    vals = []
    for r in range(PEER_TOPK):
        m = jnp.max(s, axis=0, keepdims=True)
        hit = s == m
        if exact_ties:
            hit = pos == jnp.min(jnp.where(hit, pos, float(n)), axis=0, keepdims=True)
        rank = jnp.where(hit, float(r), rank)
        s = jnp.where(hit, -jnp.inf, s)
        vals.append(m)
    return vals, rank


def _select_lanes(s0, s1, exact_ties):
    tt = s0.shape[1]
    v0, rank0 = _top16(s0, exact_ties)
    v1, rank1 = _top16(s1, exact_ties)

    row8 = lax.broadcasted_iota(jnp.int32, (SUBLANES, tt), 0)
    stack8 = lambda rows: functools.reduce(
        lambda acc, kv: jnp.where(row8 == kv[0], kv[1], acc), enumerate(rows), jnp.zeros((SUBLANES, tt), F32))
    v1_lo, v1_hi = stack8(v1[:8]), stack8(v1[8:])
    v0_hi = stack8(v0[8:])

    pieces = []
    for r0, n in _CAND_GROUPS:
        lo_piece = v0[r0] + v1_lo
        pieces.append(lo_piece if n >= SUBLANES else jnp.where(row8 < n, lo_piece, -jnp.inf))
        if n > SUBLANES:
            pieces.append(v0[r0] + v1_hi)
    pieces.append(v0_hi + v1[0])
    cand = jnp.concatenate(pieces, axis=0)

    if exact_ties:
        pos = lax.broadcasted_iota(jnp.int32, cand.shape, 0).astype(F32)
    chosen = jnp.zeros(cand.shape, F32)
    c = cand
    for _ in range(PEER_TOPK):
        m = jnp.max(c, axis=0, keepdims=True)
        hit = c == m
        if exact_ties:
            hit = pos == jnp.min(jnp.where(hit, pos, float(_CAND_ROWS)), axis=0, keepdims=True)
        chosen = jnp.where(hit, 1.0, chosen)
        c = jnp.where(hit, -jnp.inf, c)

    best = v0[0] + v1[0]
    z = jnp.sum(jnp.where(chosen > 0.0, jnp.exp(cand - best), 0.0), axis=0, keepdims=True)

    counts = []
    off = 0
    for r0, n in _CAND_GROUPS:
        rows = SUBLANES * ((n + SUBLANES - 1) // SUBLANES)
        counts.append(jnp.sum(chosen[off:off + rows], axis=0, keepdims=True))
        off += rows
    for r in range(SUBLANES):
        counts.append(chosen[off + r:off + r + 1])

    c0 = jnp.zeros(s0.shape, F32)
    for r in range(SUBLANES):
        c0 = jnp.where(rank0 == float(r), counts[r], c0)
    rows_used = functools.reduce(lambda acc, cnt: acc + jnp.where(cnt > 0.0, 1.0, 0.0), counts, 0.0)
    c0 = jnp.where(rank0 < rows_used, jnp.maximum(c0, 1.0), c0)

    taken = lambda rk: jnp.sum(jnp.where(rk < float(PEER_TOPK), 1.0, 0.0), axis=0, keepdims=True)
    unique = ((taken(rank0) == float(PEER_TOPK)) & (taken(rank1) == float(PEER_TOPK))
              & (jnp.sum(chosen, axis=0, keepdims=True) == float(PEER_TOPK)))
    return (jnp.exp(s0 - v0[0]) * (0.5 / z), c0, jnp.exp(s1 - v1[0]), rank1), unique


def _select_kernel(q_ref, sk_ref, a_ref, c_ref, b_ref, r_ref, s_ref):
    passes = q_ref.shape[1] // SELECT_LANES

    def head_pass(n, carry):
        h = n // passes
        ls = pl.ds(pl.multiple_of((n % passes) * SELECT_LANES, SELECT_LANES), SELECT_LANES)

        @pl.when(n % passes == 0)
        def _():
            for p in range(2):
                rows = pl.ds(pl.multiple_of((2 * h + p) * PEER_HALF, PEER_HALF), PEER_HALF)
                s_ref[p] = jnp.dot(sk_ref[h, p], q_ref[rows, :], preferred_element_type=F32)

        def run(exact_ties):
            outs, unique = _select_lanes(s_ref[0, :, ls], s_ref[1, :, ls], exact_ties)
            for ref, val in zip((a_ref, c_ref, b_ref, r_ref), outs):
                ref[h, :, ls] = val
            return unique

        unique = run(False)

        @pl.when(jnp.min(jnp.where(unique, 1.0, 0.0)) < 0.5)
        def _():
            run(True)

        return carry

    lax.fori_loop(0, PEER_HEADS * passes, head_pass, 0)


def _peer_select(qt, sub_keys):
    t = qt.shape[1]
    tt = SELECT_TT
    out = pl.BlockSpec((None, PEER_HEADS, PEER_KEYS, tt), lambda i: (i, 0, 0, 0))
    shp = lambda dt: jax.ShapeDtypeStruct((t // tt, PEER_HEADS, PEER_KEYS, tt), dt)
    return pl.pallas_call(
        _select_kernel,
        grid=(t // tt,),
        in_specs=[pl.BlockSpec((PEER_HEADS * 2 * PEER_HALF, tt), lambda i: (0, i)),
                  pl.BlockSpec(sub_keys.shape, lambda i: (0, 0, 0, 0))],
        out_specs=[out] * 4,
        out_shape=[shp(F32)] * 4,
        scratch_shapes=[pltpu.VMEM((2, PEER_KEYS, tt), F32)],
        compiler_params=_cparams(("parallel",)),
        name="peer_select",
    )(qt, sub_keys)


def _experts_kernel(zero_ref, h_ref, a_ref, c_ref, b_ref, r_ref, u0_ref, un_ref, vt_ref, x1_ref, o_ref,
                    acc_ref, g_ref, abc_ref, stage_ref, hs_ref):
    e = pl.program_id(1)
    last = pl.num_programs(1) - 1
    eb = un_ref.shape[0]
    nsub = eb // PEER_KEYS
    tt = h_ref.shape[1]
    half = tt // 2
    lt_per_half = half // LANES
    nlt = 2 * lt_per_half
    nv = EXPERT_JCHUNK // SUBLANES
    njc = PEER_KEYS // EXPERT_JCHUNK
    plane = [zero_ref[k] + k for k in range(4)]
    zero_bits = jnp.full((SUBLANES, LANES), zero_ref[0], jnp.int32)

    def zero_bits_after(x):
        return pltpu.bitcast(x, jnp.int32) & zero_bits

    def gate_weights(lt, last_chunk_init=None):
        hf, ll = divmod(lt, lt_per_half)
        ls = slice(lt * LANES, (lt + 1) * LANES)
        for jc in range(njc):
            j0 = jc * EXPERT_JCHUNK
            init = jnp.zeros((SUBLANES, LANES), F32)
            if last_chunk_init is not None and jc == njc - 1:
                init = last_chunk_init
            w = [[init] * nv for _ in range(nsub)]
            for h in range(PEER_HEADS):
                bs = [b_ref[h, j0 + v * SUBLANES:j0 + (v + 1) * SUBLANES, ls] for v in range(nv)]
                rs = [r_ref[h, j0 + v * SUBLANES:j0 + (v + 1) * SUBLANES, ls] for v in range(nv)]
                for ii in range(nsub):
                    a8 = abc_ref[0, ii, h, :, ls]
                    c8 = abc_ref[1, ii, h, :, ls]
                    for v in range(nv):
                        w[ii][v] = w[ii][v] + jnp.where(rs[v] < c8, a8 * bs[v], 0.0)
            for ii in range(nsub):
                r0 = ii * PEER_KEYS + j0
                stage_ref[plane[2 + hf], r0:r0 + EXPERT_JCHUNK, ll * LANES:(ll + 1) * LANES] = (
                    jnp.concatenate(w[ii], axis=0))
        return w[0][0]

    def stage_block(u_ref, blk):
        for ii in range(nsub):
            i = blk * nsub + ii
            for h in range(PEER_HEADS):
                abc_ref[0, ii, h] = jnp.broadcast_to(a_ref[h, pl.ds(i, 1), :], (SUBLANES, tt))
                abc_ref[1, ii, h] = jnp.broadcast_to(c_ref[h, pl.ds(i, 1), :], (SUBLANES, tt))
        first = gate_weights(0)
        head = (slice(0, 2 * SUBLANES), slice(0, LANES))
        hs_ref[head] = pltpu.bitcast(
            pltpu.bitcast(hs_ref[head], jnp.int32) | zero_bits_after(first), BF16)
        for hf in range(2):
            hs = slice(hf * half, (hf + 1) * half)
            stage_ref[plane[hf]] = jnp.dot(u_ref[...], hs_ref[:, hs],
                                           preferred_element_type=F32)
        for lt in range(1, nlt - 1):
            gate_weights(lt)
        tail = stage_ref[plane[1], eb - SUBLANES:eb, half - LANES:half]
        gate_weights(nlt - 1, pltpu.bitcast(zero_bits_after(tail), F32))

    @pl.when(e == 0)
    def _():
        acc_ref[...] = jnp.zeros(acc_ref.shape, F32)
        hs_ref[...] = h_ref[...]
        stage_block(u0_ref, 0)

    for lt in range(nlt):
        hf, ll = divmod(lt, lt_per_half)
        x = stage_ref[plane[hf], :, ll * LANES:(ll + 1) * LANES]
        gelu = x * (1.0 + lax.erf(x * math.sqrt(0.5)))
        w = stage_ref[plane[2 + hf], :, ll * LANES:(ll + 1) * LANES]
        g_ref[:, lt * LANES:(lt + 1) * LANES] = (w * gelu).astype(BF16)
    for hf in range(2):
        hs = slice(hf * half, (hf + 1) * half)
        acc_ref[:, hs] += jnp.dot(vt_ref[...], g_ref[:, hs], preferred_element_type=F32)

    stage_block(un_ref, jnp.minimum(e + 1, last))

    @pl.when(e == last)
    def _():
        o_ref[...] = x1_ref[...] + acc_ref[...].T


def _peer_experts(h2t, a, c0, b, r1, u, vt, x1):
    tt, eb = EXPERT_TT, EXPERT_EB
    assert SELECT_TT == tt and h2t.shape[2] == tt
    t = h2t.shape[0] * tt
    nblk = PEER_EXPERTS // eb
    sel = pl.BlockSpec((None, PEER_HEADS, PEER_KEYS, tt), lambda i, e: (i, 0, 0, 0))
    row = pl.BlockSpec((tt, D_MODEL), lambda i, e: (i, 0))
    return pl.pallas_call(
        _experts_kernel,
        grid=(t // tt, nblk),
        in_specs=[pl.BlockSpec(memory_space=pltpu.SMEM),
                  pl.BlockSpec((None, D_MODEL, tt), lambda i, e: (i, 0, 0)), sel, sel, sel, sel,
                  pl.BlockSpec((eb, D_MODEL), lambda i, e: (0, 0)),
                  pl.BlockSpec((eb, D_MODEL), lambda i, e: (jnp.minimum(e + 1, nblk - 1), 0)),
                  pl.BlockSpec((None, D_MODEL, eb), lambda i, e: (e, 0, 0)),
                  row],
        out_specs=row,
        out_shape=jax.ShapeDtypeStruct((t, D_MODEL), F32),
        scratch_shapes=[pltpu.VMEM((D_MODEL, tt), F32), pltpu.VMEM((eb, tt), BF16),
                        pltpu.VMEM((2, eb // PEER_KEYS, PEER_HEADS, SUBLANES, tt), F32),
                        pltpu.VMEM((4, eb, tt // 2), F32), pltpu.VMEM((D_MODEL, tt), BF16)],
        compiler_params=_cparams(("parallel", "arbitrary")),
        name="peer_experts",
    )(jnp.zeros((4,), jnp.int32), h2t, a, c0, b, r1, u, u, vt, x1)


def _layer(x2, batch, seq, mix_gain, w_in, gate_bias, q_gain, k_gain, sinks,
           w_up_swa, w_up_sb, w_out, ffn_gain, w_q, sub_keys, u, v):
    w_in = w_in.astype(BF16)
    g1 = mix_gain.reshape(1, -1)
    qa, ka, va, qb, kb, vb = _inproj(x2, g1, w_in[:, :IN_WIDTH - GATE_WIDTH])
    ya = _swa(qa, ka, va, q_gain, k_gain, sinks, batch, seq)
    yb = _stickbreak(qb, kb, vb, batch, seq)
    x1, h2t, qt = _merge(ya, yb, g1, w_in[:, IN_WIDTH - GATE_WIDTH:], gate_bias.reshape(1, -1), x2,
                         w_up_swa.astype(BF16), w_up_sb.astype(BF16), w_out.astype(BF16),
                         ffn_gain.reshape(1, -1), w_q.T.astype(BF16))
    a, c0, b, r1 = _peer_select(qt, sub_keys.astype(BF16))
    vt = v.reshape(PEER_EXPERTS // EXPERT_EB, EXPERT_EB, D_MODEL).transpose(0, 2, 1).astype(BF16)
    return _peer_experts(h2t, a, c0, b, r1, u.astype(BF16), vt, x1)


def kernel(x, mix_norm_gain, w_in, gate_bias, swa_q_gain, swa_k_gain, swa_sinks, w_up_swa, w_up_sb,
           w_out, ffn_norm_gain, peer_w_q, peer_sub_keys, peer_u, peer_v):
    batch, seq, d = x.shape
    x2 = x.reshape(batch * seq, d)
    for layer in range(mix_norm_gain.shape[0]):
        x2 = _layer(x2, batch, seq, mix_norm_gain[layer], w_in[layer], gate_bias[layer],
                    swa_q_gain[layer], swa_k_gain[layer], swa_sinks[layer], w_up_swa[layer],
                    w_up_sb[layer], w_out[layer], ffn_norm_gain[layer], peer_w_q[layer],
                    peer_sub_keys[layer], peer_u[layer], peer_v[layer])
    return x2.reshape(batch, seq, d)
```
